```python
import math
import jax, jax.numpy as jnp
from jax import lax
import numpy as np

D_MODEL = 1024
BATCH = 8
SEQ = 4096
DEPTH = 2

ATTN_HEADS = 8
HEAD_DIM = 64
ATTN_WIDTH = ATTN_HEADS * HEAD_DIM
Q_BLOCK = 128
POOL_WINDOWS = (2, 4, 8, 16)
POOL_GROUPS = len(POOL_WINDOWS)
POOL_WIDTH = D_MODEL - ATTN_WIDTH
POOL_GROUP_DIM = POOL_WIDTH // POOL_GROUPS
IN_WIDTH = 3 * ATTN_WIDTH + ATTN_HEADS + POOL_WIDTH
N_EXPERT_GROUPS = 4
EXPERTS_PER_GROUP = 8
TOP_K_EXPERT = 2
D_EXPERT = 256
N_MOD = 6
EPS = 1e-6
NEG_INF = -1e30
FGATE_BIAS_INIT = 3.0

kernel_name = "hybrid_fox_pool_hmoe_adaln"


def rmsnorm(x, g):
    xf = x.astype(jnp.float32)
    y = xf * lax.rsqrt(jnp.mean(xf * xf, axis=-1, keepdims=True) + EPS)
    return (y * g.astype(jnp.float32)).astype(x.dtype)


def modulate(h, shift, scale):
    return h * (1.0 + scale[:, None, :]) + shift[:, None, :]


def forgetting_attention(q, k, v, log_f):
    B, S, H, dh = q.shape
    nb = S // Q_BLOCK
    qh = q.transpose(0, 2, 1, 3)
    kh = k.transpose(0, 2, 1, 3)
    vh = v.transpose(0, 2, 1, 3)
    F = jnp.cumsum(log_f, axis=1).transpose(0, 2, 1)
    q_blocks = qh.reshape(B, H, nb, Q_BLOCK, dh).transpose(2, 0, 1, 3, 4)
    fq_blocks = F.reshape(B, H, nb, Q_BLOCK).transpose(2, 0, 1, 3)
    starts = jnp.arange(nb, dtype=jnp.int32) * Q_BLOCK
    key_pos = jnp.arange(S, dtype=jnp.int32)
    scale = 1.0 / math.sqrt(dh)

    def one_block(args):
        qb, fqb, start = args
        s = jnp.einsum('bhqd,bhkd->bhqk', qb, kh).astype(jnp.float32) * scale
        s = s + fqb[..., None] - F[:, :, None, :]
        qpos = start + jnp.arange(Q_BLOCK, dtype=jnp.int32)
        causal = key_pos[None, :] <= qpos[:, None]
        s = jnp.where(causal[None, None], s, NEG_INF)
        p = jax.nn.softmax(s, axis=-1).astype(vh.dtype)
        return jnp.einsum('bhqk,bhkd->bhqd', p, vh)

    out = lax.map(one_block, (q_blocks, fq_blocks, starts))
    return out.transpose(1, 0, 3, 2, 4).reshape(B, S, H * dh)


def trailing_mean(cs, w):
    S = cs.shape[1]
    prev = jnp.pad(cs[:, :S - w], ((0, 0), (w, 0), (0, 0)))
    cnt = jnp.minimum(jnp.arange(1, S + 1), w).astype(jnp.float32)
    return (cs - prev) / cnt[None, :, None]


def multiscale_pool(u, w_pool, pool_scale):
    B, S, _ = u.shape
    ug = u.reshape(B, S, POOL_GROUPS, POOL_GROUP_DIM)
    cs = jnp.cumsum(ug.astype(jnp.float32), axis=1)
    pooled = jnp.stack(
        [trailing_mean(cs[:, :, g, :], w) for g, w in enumerate(POOL_WINDOWS)], axis=2)
    diff = (pooled - ug.astype(jnp.float32)).astype(u.dtype)
    mixed = jnp.einsum('bsgc,gcd->bsgd', diff, w_pool)
    return mixed.reshape(B, S, POOL_WIDTH) * pool_scale


def hierarchical_moe(h, w_rg, b_rg, w_re, b_re, w_gate, w_up, w_down):
    lg = jnp.matmul(h, w_rg).astype(jnp.float32) + b_rg
    pg = jax.nn.softmax(lg, axis=-1)
    top_p, top_g = lax.top_k(pg, 1)
    le = jnp.einsum('td,gde->tge', h, w_re).astype(jnp.float32) + b_re
    le_sel = jnp.take_along_axis(le, top_g[:, :, None], axis=1)[:, 0]
    ev, ei = lax.top_k(le_sel, TOP_K_EXPERT)
    ew = jax.nn.softmax(ev, axis=-1)
    e_comb = jnp.sum(jax.nn.one_hot(ei, EXPERTS_PER_GROUP, dtype=jnp.float32) * ew[..., None], axis=1)
    g_comb = jax.nn.one_hot(top_g[:, 0], N_EXPERT_GROUPS, dtype=jnp.float32) * top_p
    comb = (g_comb[:, :, None] * e_comb[:, None, :]).astype(h.dtype)
    y = jnp.zeros_like(h)
    for g in range(N_EXPERT_GROUPS):
        a = jnp.einsum('td,edf->tef', h, w_gate[g])
        b = jnp.einsum('td,edf->tef', h, w_up[g])
        act = jax.nn.silu(a) * b * comb[:, g, :, None]
        y = y + jnp.einsum('tef,efd->td', act, w_down[g])
    return y


def setup_inputs(seed: int = 0) -> dict:
    key = jax.random.key(seed)
    ks = jax.random.split(key, 20)
    f32 = jnp.float32
    D, L = D_MODEL, DEPTH
    G, E, F = N_EXPERT_GROUPS, EXPERTS_PER_GROUP, D_EXPERT

    def nrm(k, shape, scale):
        return jax.random.normal(k, shape, f32) * scale

    return {
        "x": nrm(ks[0], (BATCH, SEQ, D), 1.0),
        "c": nrm(ks[1], (BATCH, D), 1.0),
        "norm_mix_g": 1.0 + nrm(ks[2], (L, D), 0.02),
        "norm_ffn_g": 1.0 + nrm(ks[3], (L, D), 0.02),
        "norm_final_g": 1.0 + nrm(ks[4], (D,), 0.02),
        "w_ada": nrm(ks[5], (L, D, N_MOD * D), 0.5 * D ** -0.5),
        "b_ada": nrm(ks[6], (L, N_MOD * D), 0.02),
        "w_in": nrm(ks[7], (L, D, IN_WIDTH), D ** -0.5),
        "b_fgate": FGATE_BIAS_INIT + nrm(ks[8], (L, ATTN_HEADS), 0.1),
        "w_pool": nrm(ks[9], (L, POOL_GROUPS, POOL_GROUP_DIM, POOL_GROUP_DIM), POOL_GROUP_DIM ** -0.5),
        "pool_scale": 1.0 + nrm(ks[10], (L, POOL_WIDTH), 0.1),
        "w_out": nrm(ks[11], (L, D, D), D ** -0.5),
        "w_router_group": nrm(ks[12], (L, D, G), D ** -0.5),
        "b_router_group": nrm(ks[13], (L, G), 0.01),
        "w_router_expert": nrm(ks[14], (L, G, D, E), D ** -0.5),
        "b_router_expert": nrm(ks[15], (L, G, E), 0.01),
        "w_expert_gate": nrm(ks[16], (L, G, E, D, F), D ** -0.5),
        "w_expert_up": nrm(ks[17], (L, G, E, D, F), D ** -0.5),
        "w_expert_down": nrm(ks[18], (L, G, E, F, D), F ** -0.5),
    }


def reference(x, c, norm_mix_g, norm_ffn_g, norm_final_g, w_ada, b_ada, w_in, b_fgate,
              w_pool, pool_scale, w_out, w_router_group, b_router_group, w_router_expert,
              b_router_expert, w_expert_gate, w_expert_up, w_expert_down):
    B, S, D = x.shape
    split_pts = [ATTN_WIDTH, 2 * ATTN_WIDTH, 3 * ATTN_WIDTH, 3 * ATTN_WIDTH + ATTN_HEADS]
    c_act = jax.nn.silu(c)
    for l in range(DEPTH):
        mod = jnp.matmul(c_act, w_ada[l]) + b_ada[l]
        sh_m, sc_m, gt_m, sh_f, sc_f, gt_f = jnp.split(mod, N_MOD, axis=-1)

        h = modulate(rmsnorm(x, norm_mix_g[l]), sh_m, sc_m)
        proj = jnp.matmul(h, w_in[l])
        q, k, v, f_logit, u = jnp.split(proj, split_pts, axis=-1)
        q = q.reshape(B, S, ATTN_HEADS, HEAD_DIM)
        k = k.reshape(B, S, ATTN_HEADS, HEAD_DIM)
        v = v.reshape(B, S, ATTN_HEADS, HEAD_DIM)
        log_f = jax.nn.log_sigmoid(f_logit.astype(jnp.float32) + b_fgate[l].astype(jnp.float32))
        attn_out = forgetting_attention(q, k, v, log_f)
        pool_out = multiscale_pool(u, w_pool[l], pool_scale[l])
        mix = jnp.matmul(jnp.concatenate([attn_out, pool_out.astype(attn_out.dtype)], axis=-1), w_out[l])
        x = x + gt_m[:, None, :] * mix

        h = modulate(rmsnorm(x, norm_ffn_g[l]), sh_f, sc_f)
        ffn = hierarchical_moe(h.reshape(B * S, D), w_router_group[l], b_router_group[l],
                               w_router_expert[l], b_router_expert[l], w_expert_gate[l],
                               w_expert_up[l], w_expert_down[l])
        x = x + gt_f[:, None, :] * ffn.reshape(B, S, D)
    return rmsnorm(x, norm_final_g)
```

```python
import functools

import jax
import jax.numpy as jnp
from jax import lax
from jax.experimental import pallas as pl
from jax.experimental.pallas import tpu as pltpu

ATTN_HEADS = 8
HEAD_DIM = 64
ATTN_WIDTH = ATTN_HEADS * HEAD_DIM
POOL_WINDOWS = (2, 4, 8, 16)
POOL_GROUP_DIM = 128
POOL_WIDTH = POOL_GROUP_DIM * len(POOL_WINDOWS)
POOL_HALO = 16
N_EXPERT_GROUPS = 4
EXPERTS_PER_GROUP = 8
N_EXPERTS = N_EXPERT_GROUPS * EXPERTS_PER_GROUP
N_MOD = 6
EPS = 1e-6
NEG_INF = -1e30

LANES = 128
SUBLANES = 8
ROUTER_ROWS = 40
VMEM_LIMIT = 48 * 1024 * 1024

F32 = jnp.float32
BF16 = jnp.bfloat16


def _silu(a):
    return a * jax.nn.sigmoid(a)


def _nt_dot(a, b):
    return lax.dot_general(a, b, (((1,), (1,)), ((), ())), preferred_element_type=F32)


def _rms_modulate(x, g, shift, scale):
    ms = jnp.mean(x * x, axis=-1, keepdims=True)
    y = x * lax.rsqrt(ms + EPS) * g
    return y * (1.0 + scale) + shift


def _ada_kernel(c_ref, w_ref, b_ref, o_ref):
    ca = _silu(c_ref[...])
    o_ref[0] = jnp.dot(ca, w_ref[0], precision=lax.Precision.HIGHEST, preferred_element_type=F32) + b_ref[0]


def _ada_modulation(c, w_ada, b_ada):
    L, D, W = w_ada.shape
    B = c.shape[0]
    tn = W // 4
    return pl.pallas_call(
        _ada_kernel,
        grid=(L, W // tn),
        in_specs=[
            pl.BlockSpec((B, D), lambda l, n: (0, 0)),
            pl.BlockSpec((1, D, tn), lambda l, n: (l, 0, n)),
            pl.BlockSpec((1, 1, tn), lambda l, n: (l, 0, n)),
        ],
        out_specs=pl.BlockSpec((1, B, tn), lambda l, n: (l, 0, n)),
        out_shape=jax.ShapeDtypeStruct((L, B, W), F32),
        compiler_params=pltpu.CompilerParams(vmem_limit_bytes=VMEM_LIMIT),
        name="ada_modulation",
    )(c, w_ada, b_ada.reshape(L, 1, W))


def _premix_kernel(x_ref, mod_ref, g_ref, wqkv_ref, wf_ref, bf_ref, wu_ref,
                   q_ref, k_ref, v_ref, f_ref, diff_ref, carry_ref, ubuf_ref, *, tm):
    si = pl.program_id(1)

    @pl.when(si == 0)
    def _():
        carry_ref[...] = jnp.zeros_like(carry_ref)
        ubuf_ref[0:POOL_HALO, :] = jnp.zeros((POOL_HALO, POOL_WIDTH), F32)

    h = _rms_modulate(x_ref[0], g_ref[...], mod_ref[0, 0:1, :], mod_ref[0, 1:2, :]).astype(BF16)
    qkv = jnp.dot(h, wqkv_ref[...], preferred_element_type=F32)
    q_ref[0] = (qkv[:, :ATTN_WIDTH] * (HEAD_DIM ** -0.5)).astype(BF16)
    k_ref[0] = qkv[:, ATTN_WIDTH:2 * ATTN_WIDTH].astype(BF16)
    v_ref[0] = qkv[:, 2 * ATTN_WIDTH:].astype(BF16)

    fl = jnp.dot(h, wf_ref[...], preferred_element_type=F32) + bf_ref[...]
    lf = jnp.minimum(fl, 0.0) - jnp.log1p(jnp.exp(-jnp.abs(fl)))
    row = lax.broadcasted_iota(jnp.int32, (tm, tm), 0)
    col = lax.broadcasted_iota(jnp.int32, (tm, tm), 1)
    tri = (row >= col).astype(BF16)
    l0 = lf.astype(BF16)
    r1 = lf - l0.astype(F32)
    l1 = r1.astype(BF16)
    l2 = (r1 - l1.astype(F32)).astype(BF16)
    cs = (jnp.dot(tri, l0, preferred_element_type=F32) + jnp.dot(tri, l1, preferred_element_type=F32)
          + jnp.dot(tri, l2, preferred_element_type=F32))
    f_cum = cs + carry_ref[...]
    carry_ref[...] = f_cum[tm - 1:tm, :]
    f_ref[0] = f_cum[:, :ATTN_HEADS]

    u = jnp.dot(h, wu_ref[...], preferred_element_type=F32)
    ubuf_ref[POOL_HALO:POOL_HALO + tm, :] = u
    pos = si * tm + lax.broadcasted_iota(jnp.int32, (tm, POOL_GROUP_DIM), 0)
    diffs = []
    for g, w in enumerate(POOL_WINDOWS):
        c0 = g * POOL_GROUP_DIM
        ug = u[:, c0:c0 + POOL_GROUP_DIM]
        acc = ug
        for j in range(1, w):
            acc = acc + ubuf_ref[POOL_HALO - j:POOL_HALO - j + tm, c0:c0 + POOL_GROUP_DIM]
        cnt = jnp.minimum(pos + 1, w).astype(F32)
        diffs.append((acc / cnt - ug).astype(BF16))
    diff_ref[0] = jnp.concatenate(diffs, axis=1)
    ubuf_ref[0:POOL_HALO, :] = u[tm - POOL_HALO:, :]


def _premix(x, mod, g, wqkv, wf, bf, wu, *, tm):
    B, S, D = x.shape
    row_spec = lambda w: pl.BlockSpec((1, tm, w), lambda b, s: (b, s, 0))
    const2 = lambda shape: pl.BlockSpec(shape, lambda b, s: (0, 0))
    return pl.pallas_call(
        functools.partial(_premix_kernel, tm=tm),
        grid=(B, S // tm),
        in_specs=[
            row_spec(D),
            pl.BlockSpec((1, N_MOD, D), lambda b, s: (b, 0, 0)),
            const2((1, D)),
            const2(wqkv.shape),
            const2(wf.shape),
            const2(bf.shape),
            const2(wu.shape),
        ],
        out_specs=[row_spec(ATTN_WIDTH), row_spec(ATTN_WIDTH), row_spec(ATTN_WIDTH),
                   row_spec(ATTN_HEADS), row_spec(POOL_WIDTH)],
        out_shape=[
            jax.ShapeDtypeStruct((B, S, ATTN_WIDTH), BF16),
            jax.ShapeDtypeStruct((B, S, ATTN_WIDTH), BF16),
            jax.ShapeDtypeStruct((B, S, ATTN_WIDTH), BF16),
            jax.ShapeDtypeStruct((B, S, ATTN_HEADS), F32),
            jax.ShapeDtypeStruct((B, S, POOL_WIDTH), BF16),
        ],
        scratch_shapes=[pltpu.VMEM((1, LANES), F32), pltpu.VMEM((POOL_HALO + tm, POOL_WIDTH), F32)],
        compiler_params=pltpu.CompilerParams(
            dimension_semantics=("arbitrary", "arbitrary"), vmem_limit_bytes=VMEM_LIMIT),
        name="premix",
    )(x, mod, g, wqkv, wf, bf, wu)


def _attn_kernel(q_ref, k_ref, v_ref, ft_ref, o_ref, m_ref, l_ref, acc_ref, *, tq):
    qi = pl.program_id(2)
    tk = tq
    q = q_ref[0]
    lo = lax.broadcasted_iota(jnp.int32, (1, LANES), 1) < HEAD_DIM
    zero = jnp.zeros_like(q)
    qh = (jnp.where(lo, q, zero), jnp.where(lo, zero, q))
    q0 = pl.multiple_of(qi * tq, tq)
    fref = [ft_ref[0, 0, h:h + 1, pl.ds(q0, LANES)][:, 0:1] for h in range(2)]

    m_ref[...] = jnp.full(m_ref.shape, NEG_INF, F32)
    l_ref[...] = jnp.zeros(l_ref.shape, F32)
    acc_ref[...] = jnp.zeros(acc_ref.shape, F32)

    def block(j, masked):
        k0 = pl.multiple_of(j * tk, tk)
        kb = k_ref[0, pl.ds(k0, tk), :]
        vb = v_ref[0, pl.ds(k0, tk), :]
        zv = jnp.zeros_like(vb)
        vh = (jnp.where(lo, vb, zv), jnp.where(lo, zv, vb))
        pv = None
        alphas = []
        for h in range(2):
            bias = fref[h] - ft_ref[0, 0, h:h + 1, pl.ds(k0, tk)]
            s = _nt_dot(qh[h], kb) + bias
            if masked:
                r = lax.broadcasted_iota(jnp.int32, (tq, tk), 0)
                c = lax.broadcasted_iota(jnp.int32, (tq, tk), 1)
                s = jnp.where(r >= c, s, NEG_INF)
            m_prev = m_ref[h]
            m_new = jnp.maximum(m_prev, jnp.max(s, axis=-1, keepdims=True))
            p = jnp.exp(s - m_new[:, 0:1])
            alpha = jnp.exp(m_prev - m_new)
            l_ref[h] = alpha * l_ref[h] + jnp.sum(p, axis=-1, keepdims=True)
            m_ref[h] = m_new
            alphas.append(alpha)
            d = jnp.dot(p.astype(BF16), vh[h], preferred_element_type=F32)
            pv = d if pv is None else pv + d
        acc_ref[...] = acc_ref[...] * jnp.where(lo, alphas[0], alphas[1]) + pv

    def body(j, c):
        block(j, False)
        return c

    lax.fori_loop(0, qi, body, 0)
    block(qi, True)
    o_ref[0] = (acc_ref[...] / jnp.where(lo, l_ref[0], l_ref[1])).astype(BF16)


def _attention(q, k, v, ft, *, tq):
    B, S, W = q.shape
    pairs = W // LANES
    kv_spec = pl.BlockSpec((1, S, LANES), lambda b, p, i: (b, 0, p))
    q_spec = pl.BlockSpec((1, tq, LANES), lambda b, p, i: (b, i, p))
    return pl.pallas_call(
        functools.partial(_attn_kernel, tq=tq),
        grid=(B, pairs, S // tq),
        in_specs=[q_spec, kv_spec, kv_spec,
                  pl.BlockSpec((1, 1, 2, S), lambda b, p, i: (b, p, 0, 0))],
        out_specs=q_spec,
        out_shape=jax.ShapeDtypeStruct((B, S, W), BF16),
        scratch_shapes=[pltpu.VMEM((2, tq, LANES), F32), pltpu.VMEM((2, tq, LANES), F32),
                        pltpu.VMEM((tq, LANES), F32)],
        compiler_params=pltpu.CompilerParams(
            dimension_semantics=("arbitrary", "arbitrary", "arbitrary"), vmem_limit_bytes=VMEM_LIMIT),
        name="fox_attention",
    )(q, k, v, ft)


def _postmix_kernel(attn_ref, diff_ref, x_ref, mod_ref, wpool_ref, pscale_ref, wout_ref, g_ref,
                    wr_hi_ref, wr_lo_ref, br_ref,
                    x1_ref, hslab_ref, ids_ref, wts_ref, *, tm):
    pooled = []
    for g in range(len(POOL_WINDOWS)):
        c0 = g * POOL_GROUP_DIM
        pooled.append(jnp.dot(diff_ref[0, :, c0:c0 + POOL_GROUP_DIM], wpool_ref[g], preferred_element_type=F32))
    pool_out = (jnp.concatenate(pooled, axis=1) * pscale_ref[...]).astype(BF16)
    cat = jnp.concatenate([attn_ref[0], pool_out], axis=1)
    mix = jnp.dot(cat, wout_ref[...], preferred_element_type=F32)
    x1 = x_ref[0] + mod_ref[0, 2:3, :] * mix
    x1_ref[0] = x1

    h = _rms_modulate(x1, g_ref[...], mod_ref[0, 3:4, :], mod_ref[0, 4:5, :])
    for j in range(h.shape[1] // LANES):
        hslab_ref[pl.ds(j, tm, stride=SUBLANES), :] = h[:, j * LANES:(j + 1) * LANES]

    h_hi = h.astype(BF16)
    h_lo = (h - h_hi.astype(F32)).astype(BF16)
    logits = (_nt_dot(wr_hi_ref[...], h_hi) + _nt_dot(wr_lo_ref[...], h_hi) + _nt_dot(wr_hi_ref[...], h_lo)
              + br_ref[...])
    sub = lax.broadcasted_iota(jnp.int32, (SUBLANES, tm), 0)
    lg = jnp.where(sub < N_EXPERT_GROUPS, logits[0:SUBLANES, :], NEG_INF)
    g_max = jnp.max(lg, axis=0, keepdims=True)
    top_p = 1.0 / jnp.sum(jnp.exp(lg - g_max), axis=0, keepdims=True)
    top_g = jnp.min(jnp.where(lg == g_max, sub, SUBLANES), axis=0, keepdims=True)
    le = logits[SUBLANES:2 * SUBLANES, :]
    for g in range(1, N_EXPERT_GROUPS):
        le = jnp.where(top_g == g, logits[(g + 1) * SUBLANES:(g + 2) * SUBLANES, :], le)
    v1 = jnp.max(le, axis=0, keepdims=True)
    i1 = jnp.min(jnp.where(le == v1, sub, SUBLANES), axis=0, keepdims=True)
    le2 = jnp.where(sub == i1, NEG_INF, le)
    v2 = jnp.max(le2, axis=0, keepdims=True)
    i2 = jnp.min(jnp.where(le2 == v2, sub, SUBLANES), axis=0, keepdims=True)
    e2 = jnp.exp(v2 - v1)
    w1 = top_p / (1.0 + e2)
    ids_ref[...] = jnp.concatenate([top_g * EXPERTS_PER_GROUP + i1, top_g * EXPERTS_PER_GROUP + i2], axis=0)
    wts_ref[...] = jnp.concatenate([w1, w1 * e2], axis=0)


def _postmix(attn, diff, x, mod, wpool, pscale, wout, g, wr_hi, wr_lo, br, *, tm):
    B, S, D = x.shape
    T = B * S
    nst = S // tm
    row_spec = lambda w: pl.BlockSpec((1, tm, w), lambda b, s: (b, s, 0))
    const = lambda a: pl.BlockSpec(a.shape, lambda b, s: (0,) * a.ndim)
    tok_spec = pl.BlockSpec((2, tm), lambda b, s: (0, b * nst + s))
    return pl.pallas_call(
        functools.partial(_postmix_kernel, tm=tm),
        grid=(B, nst),
        in_specs=[row_spec(ATTN_WIDTH), row_spec(POOL_WIDTH), row_spec(D),
                  pl.BlockSpec((1, N_MOD, D), lambda b, s: (b, 0, 0)),
                  const(wpool), const(pscale), const(wout), const(g), const(wr_hi), const(wr_lo), const(br)],
        out_specs=[row_spec(D),
                   pl.BlockSpec((tm * SUBLANES, LANES), lambda b, s: (b * nst + s, 0)),
                   tok_spec, tok_spec],
        out_shape=[
            jax.ShapeDtypeStruct((B, S, D), F32),
            jax.ShapeDtypeStruct((T * SUBLANES, LANES), F32),
            jax.ShapeDtypeStruct((2, T), jnp.int32),
            jax.ShapeDtypeStruct((2, T), F32),
        ],
        compiler_params=pltpu.CompilerParams(
            dimension_semantics=("arbitrary", "arbitrary"), vmem_limit_bytes=VMEM_LIMIT),
        name="postmix_router",
    )(attn, diff, x, mod, wpool, pscale, wout, g, wr_hi, wr_lo, br)


def _row_gather_start(idx_of, n_rows, src_hbm, dst, sem):
    def body(r, c):
        src0 = pl.multiple_of(idx_of(r) * SUBLANES, SUBLANES)
        dst0 = pl.multiple_of(r * SUBLANES, SUBLANES)
        pltpu.make_async_copy(src_hbm.at[pl.ds(src0, SUBLANES), :], dst.at[pl.ds(dst0, SUBLANES), :], sem).start()
        return c

    lax.fori_loop(0, n_rows, body, 0, unroll=8)


def _row_gather_wait(src_hbm, dst, sem):
    pltpu.make_async_copy(src_hbm.at[pl.ds(0, dst.shape[0]), :], dst, sem).wait()


def _slab_rows(ref, tm, width):
    return jnp.concatenate([ref[pl.ds(j, tm, stride=SUBLANES), :] for j in range(width // LANES)], axis=1)


def _expert_kernel(te_ref, tv_ref, idx_ref, h_hbm, wg_ref, wu_ref, wd_ref, o_ref, buf, sem, *, tm, nt):
    i = pl.program_id(0)

    @pl.when(jnp.logical_and(i < nt, tv_ref[jnp.minimum(i, nt - 1)] == 1))
    def _():
        slot = i % 2
        _row_gather_start(lambda r: idx_ref[0, 0, r], tm, h_hbm, buf.at[slot], sem.at[slot])

    @pl.when(i > 0)
    def _():
        t = i - 1
        slot = t % 2

        @pl.when(tv_ref[t] == 1)
        def _():
            _row_gather_wait(h_hbm, buf.at[slot], sem.at[slot])
            d = wg_ref.shape[1]
            x = _slab_rows(buf.at[slot], tm, d).astype(BF16)
            a = jnp.dot(x, wg_ref[0].astype(BF16), preferred_element_type=F32)
            b = jnp.dot(x, wu_ref[0].astype(BF16), preferred_element_type=F32)
            act = (_silu(a) * b).astype(BF16)
            o = jnp.dot(act, wd_ref[0].astype(BF16), preferred_element_type=F32)
            for j in range(d // LANES):
                o_ref[pl.ds(j, tm, stride=SUBLANES), :] = o[:, j * LANES:(j + 1) * LANES]

        @pl.when(tv_ref[t] == 0)
        def _():
            o_ref[...] = jnp.zeros(o_ref.shape, F32)


def _experts(tile_expert, tile_valid, tok_sorted, hslab, wg, wu, wd, *, tm):
    nt = tile_expert.shape[0]
    E, D, Fe = wg.shape
    prev = lambda i: jnp.maximum(i - 1, 0)
    grid_spec = pltpu.PrefetchScalarGridSpec(
        num_scalar_prefetch=2,
        grid=(nt + 1,),
        in_specs=[
            pl.BlockSpec((1, 1, tm), lambda i, te, tv: (jnp.minimum(i, nt - 1), 0, 0), memory_space=pltpu.SMEM),
            pl.BlockSpec(memory_space=pl.ANY),
            pl.BlockSpec((1, D, Fe), lambda i, te, tv: (te[prev(i)], 0, 0)),
            pl.BlockSpec((1, D, Fe), lambda i, te, tv: (te[prev(i)], 0, 0)),
            pl.BlockSpec((1, Fe, D), lambda i, te, tv: (te[prev(i)], 0, 0)),
        ],
        out_specs=pl.BlockSpec((tm * SUBLANES, LANES), lambda i, te, tv: (prev(i), 0)),
        scratch_shapes=[pltpu.VMEM((2, tm * SUBLANES, LANES), F32), pltpu.SemaphoreType.DMA((2,))],
    )
    return pl.pallas_call(
        functools.partial(_expert_kernel, tm=tm, nt=nt),
        grid_spec=grid_spec,
        out_shape=jax.ShapeDtypeStruct((nt * tm * SUBLANES, LANES), F32),
        compiler_params=pltpu.CompilerParams(dimension_semantics=("arbitrary",), vmem_limit_bytes=VMEM_LIMIT),
        name="moe_experts",
    )(tile_expert, tile_valid, tok_sorted.reshape(nt, 1, tm), hslab, wg, wu, wd)


def _combine_kernel(pos_ref, o_hbm, x1_ref, wts_ref, mod_ref, *rest, tm, nt, final):
    if final:
        gf_ref, out_ref, buf, sem = rest
    else:
        out_ref, buf, sem = rest
    i = pl.program_id(0)

    @pl.when(i < nt)
    def _():
        slot = i % 2
        for k in range(2):
            _row_gather_start(lambda r: pos_ref[0, k, r], tm, o_hbm, buf.at[slot, k], sem.at[slot, k])

    @pl.when(i > 0)
    def _():
        slot = (i - 1) % 2
        d = x1_ref.shape[2]
        y = None
        for k in range(2):
            _row_gather_wait(o_hbm, buf.at[slot, k], sem.at[slot, k])
            wk = jnp.broadcast_to(wts_ref[k:k + 1, :], (LANES, tm)).T
            ok = _slab_rows(buf.at[slot, k], tm, d) * jnp.concatenate([wk] * (d // LANES), axis=1)
            y = ok if y is None else y + ok
        x2 = x1_ref[0] + mod_ref[0, 5:6, :] * y
        if final:
            ms = jnp.mean(x2 * x2, axis=-1, keepdims=True)
            x2 = x2 * lax.rsqrt(ms + EPS) * gf_ref[...]
        out_ref[0] = x2


def _combine(pos, oslab, x1, wts, mod, g_final, *, tm):
    B, S, D = x1.shape
    nst = S // tm
    nt = B * nst
    prev = lambda i: jnp.maximum(i - 1, 0)
    final = g_final is not None
    in_specs = [
        pl.BlockSpec((1, 2, tm), lambda i: (jnp.minimum(i, nt - 1), 0, 0), memory_space=pltpu.SMEM),
        pl.BlockSpec(memory_space=pl.ANY),
        pl.BlockSpec((1, tm, D), lambda i: (prev(i) // nst, prev(i) % nst, 0)),
        pl.BlockSpec((2, tm), lambda i: (0, prev(i))),
        pl.BlockSpec((1, N_MOD, D), lambda i: (prev(i) // nst, 0, 0)),
    ]
    args = [pos.reshape(2, nt, tm).transpose(1, 0, 2), oslab, x1, wts, mod]
    if final:
        in_specs.append(pl.BlockSpec((1, D), lambda i: (0, 0)))
        args.append(g_final)
    return pl.pallas_call(
        functools.partial(_combine_kernel, tm=tm, nt=nt, final=final),
        grid=(nt + 1,),
        in_specs=in_specs,
        out_specs=pl.BlockSpec((1, tm, D), lambda i: (prev(i) // nst, prev(i) % nst, 0)),
        out_shape=jax.ShapeDtypeStruct((B, S, D), F32),
        scratch_shapes=[pltpu.VMEM((2, 2, tm * SUBLANES, LANES), F32), pltpu.SemaphoreType.DMA((2, 2))],
        compiler_params=pltpu.CompilerParams(dimension_semantics=("arbitrary",), vmem_limit_bytes=VMEM_LIMIT),
        name="moe_combine",
    )(*args)


def _dispatch_plan(ids, *, tm, nt):
    T = ids.shape[1]
    e = ids.reshape(-1)
    onehot = (e[:, None] == jnp.arange(N_EXPERTS, dtype=jnp.int32)[None, :]).astype(jnp.int32)
    csum = jnp.cumsum(onehot, axis=0)
    rank = jnp.sum(csum * onehot, axis=1) - 1
    counts = csum[-1]
    padded = ((counts + tm - 1) // tm) * tm
    ends = jnp.cumsum(padded)
    pos = (ends - padded)[e] + rank
    tok = jnp.tile(jnp.arange(T, dtype=jnp.int32), 2)
    tok_sorted = jnp.zeros((nt * tm,), jnp.int32).at[pos].set(tok)
    starts = jnp.arange(nt, dtype=jnp.int32) * tm
    tile_expert = jnp.minimum(jnp.searchsorted(ends, starts, side="right"), N_EXPERTS - 1).astype(jnp.int32)
    tile_valid = (starts < ends[-1]).astype(jnp.int32)
    return pos.astype(jnp.int32), tok_sorted, tile_expert, tile_valid


def kernel(x, c, norm_mix_g, norm_ffn_g, norm_final_g, w_ada, b_ada, w_in, b_fgate, w_pool, pool_scale, w_out,
           w_router_group, b_router_group, w_router_expert, b_router_expert, w_expert_gate, w_expert_up,
           w_expert_down):
    B, S, D = x.shape
    L = w_ada.shape[0]
    T = B * S
    tm_mix = min(512, S)
    tq = min(256, S)
    tm_e = 256
    tm_c = min(256, S)
    nt_e = (2 * T) // tm_e + N_EXPERTS

    mod_all = _ada_modulation(c, w_ada, b_ada).reshape(L, B, N_MOD, D)
    Fe = w_expert_gate.shape[-1]
    for l in range(L):
        mod = mod_all[l]
        w_in_l = w_in[l]
        wqkv = w_in_l[:, :3 * ATTN_WIDTH].astype(BF16)
        wf = jnp.pad(w_in_l[:, 3 * ATTN_WIDTH:3 * ATTN_WIDTH + ATTN_HEADS], ((0, 0), (0, LANES - ATTN_HEADS))).astype(BF16)
        bf = jnp.pad(b_fgate[l].astype(F32), (0, LANES - ATTN_HEADS)).reshape(1, LANES)
        wu = w_in_l[:, 3 * ATTN_WIDTH + ATTN_HEADS:].astype(BF16)
        q, k, v, f_cum, diff = _premix(x, mod, norm_mix_g[l].reshape(1, D), wqkv, wf, bf, wu, tm=tm_mix)
        ft = f_cum.transpose(0, 2, 1).reshape(B, ATTN_HEADS // 2, 2, S)
        attn = _attention(q, k, v, ft, tq=tq)

        wr = jnp.concatenate([
            jnp.pad(w_router_group[l].T, ((0, SUBLANES - N_EXPERT_GROUPS), (0, 0))),
            w_router_expert[l].transpose(0, 2, 1).reshape(N_EXPERTS, D)], axis=0)
        wr_hi = wr.astype(BF16)
        wr_lo = (wr - wr_hi.astype(F32)).astype(BF16)
        br = jnp.concatenate([jnp.pad(b_router_group[l], (0, SUBLANES - N_EXPERT_GROUPS)),
                              b_router_expert[l].reshape(N_EXPERTS)]).reshape(ROUTER_ROWS, 1).astype(F32)
        x1, hslab, ids, wts = _postmix(attn, diff, x, mod, w_pool[l].astype(BF16), pool_scale[l].reshape(1, POOL_WIDTH),
                                       w_out[l].astype(BF16), norm_ffn_g[l].reshape(1, D), wr_hi, wr_lo, br, tm=tm_mix)

        pos, tok_sorted, tile_expert, tile_valid = _dispatch_plan(ids, tm=tm_e, nt=nt_e)
        oslab = _experts(tile_expert, tile_valid, tok_sorted, hslab,
                         w_expert_gate[l].reshape(N_EXPERTS, D, Fe), w_expert_up[l].reshape(N_EXPERTS, D, Fe),
                         w_expert_down[l].reshape(N_EXPERTS, Fe, D), tm=tm_e)
        g_final = norm_final_g.reshape(1, D) if l == L - 1 else None
        x = _combine(pos, oslab, x1, wts, mod, g_final, tm=tm_c)
    return x
```

```python
import functools

import jax
import jax.numpy as jnp
import numpy as np
from jax import lax
from jax.experimental import pallas as pl
from jax.experimental.pallas import tpu as pltpu

ATTN_HEADS = 8
HEAD_DIM = 64
ATTN_WIDTH = ATTN_HEADS * HEAD_DIM
POOL_WINDOWS = (2, 4, 8, 16)
POOL_GROUP_DIM = 128
POOL_WIDTH = POOL_GROUP_DIM * len(POOL_WINDOWS)
POOL_HALO = 16
N_EXPERT_GROUPS = 4
EXPERTS_PER_GROUP = 8
N_EXPERTS = N_EXPERT_GROUPS * EXPERTS_PER_GROUP
N_MOD = 6
EPS = 1e-6
NEG_INF = -1e30

LANES = 128
SUBLANES = 8
AUG = 2 * LANES
DENOM_ROWS = 16
ROUTER_ROWS = 40
VMEM_LIMIT = 48 * 1024 * 1024

F32 = jnp.float32
BF16 = jnp.bfloat16


def _silu(a):
    return a * jax.nn.sigmoid(a)


def _nt_dot(a, b):
    return lax.dot_general(a, b, (((1,), (1,)), ((), ())), preferred_element_type=F32)


def _split3(a):
    t0 = a.astype(BF16)
    r1 = a - t0.astype(F32)
    t1 = r1.astype(BF16)
    t2 = (r1 - t1.astype(F32)).astype(BF16)
    return t0, t1, t2


def _rms_modulate(x, g, shift, scale):
    ms = jnp.mean(x * x, axis=-1, keepdims=True)
    y = x * lax.rsqrt(ms + EPS) * g
    return y * (1.0 + scale) + shift


def _ada_kernel(c_ref, w_ref, b_ref, o_ref):
    ca = _silu(c_ref[...])
    o_ref[0] = jnp.dot(ca, w_ref[0], precision=lax.Precision.HIGHEST, preferred_element_type=F32) + b_ref[0]


def _ada_modulation(c, w_ada, b_ada):
    L, D, W = w_ada.shape
    B = c.shape[0]
    tn = W // 4
    return pl.pallas_call(
        _ada_kernel,
        grid=(L, W // tn),
        in_specs=[
            pl.BlockSpec((B, D), lambda l, n: (0, 0)),
            pl.BlockSpec((1, D, tn), lambda l, n: (l, 0, n)),
            pl.BlockSpec((1, 1, tn), lambda l, n: (l, 0, n)),
        ],
        out_specs=pl.BlockSpec((1, B, tn), lambda l, n: (l, 0, n)),
        out_shape=jax.ShapeDtypeStruct((L, B, W), F32),
        compiler_params=pltpu.CompilerParams(vmem_limit_bytes=VMEM_LIMIT),
        name="ada_modulation",
    )(c, w_ada, b_ada.reshape(L, 1, W))


def _premix_kernel(x_ref, mod_ref, g_ref, wqkv_ref, wf_ref, bf_ref, wu_ref, sel_ref,
                   q_ref, kaug_ref, vt_ref, diff_ref, carry_ref, ubuf_ref, *, tm):
    si = pl.program_id(1)

    @pl.when(si == 0)
    def _():
        carry_ref[...] = jnp.zeros_like(carry_ref)
        ubuf_ref[0:POOL_HALO, :] = jnp.zeros((POOL_HALO, POOL_WIDTH), F32)

    h = _rms_modulate(x_ref[0], g_ref[...], mod_ref[0, 0:1, :], mod_ref[0, 1:2, :]).astype(BF16)
    qkv = jnp.dot(h, wqkv_ref[...], preferred_element_type=F32)
    q_ref[0] = (qkv[:, :ATTN_WIDTH] * (HEAD_DIM ** -0.5)).astype(BF16)
    vt_ref[0] = qkv[:, 2 * ATTN_WIDTH:].T.astype(BF16)

    fl = jnp.dot(h, wf_ref[...], preferred_element_type=F32) + bf_ref[...]
    lf = jnp.minimum(fl, 0.0) - jnp.log1p(jnp.exp(-jnp.abs(fl)))
    row = lax.broadcasted_iota(jnp.int32, (tm, tm), 0)
    col = lax.broadcasted_iota(jnp.int32, (tm, tm), 1)
    tri = (row >= col).astype(BF16)
    cs = None
    for term in _split3(lf):
        d = jnp.dot(tri, term, preferred_element_type=F32)
        cs = d if cs is None else cs + d
    f_cum = cs + carry_ref[...]
    carry_ref[...] = f_cum[tm - 1:tm, :]

    aug = None
    for i, term in enumerate(_split3(-f_cum)):
        d = jnp.dot(term, sel_ref[i], preferred_element_type=F32)
        aug = d if aug is None else aug + d
    parts = []
    for p in range(ATTN_WIDTH // LANES):
        parts.append(qkv[:, ATTN_WIDTH + p * LANES:ATTN_WIDTH + (p + 1) * LANES].astype(BF16))
        parts.append(aug[:, p * LANES:(p + 1) * LANES].astype(BF16))
    kaug_ref[0] = jnp.concatenate(parts, axis=1)

    u = jnp.dot(h, wu_ref[...], preferred_element_type=F32)
    ubuf_ref[POOL_HALO:POOL_HALO + tm, :] = u
    pos = si * tm + lax.broadcasted_iota(jnp.int32, (tm, POOL_GROUP_DIM), 0)
    diffs = []
    for g, w in enumerate(POOL_WINDOWS):
        c0 = g * POOL_GROUP_DIM
        ug = u[:, c0:c0 + POOL_GROUP_DIM]
        acc = ug
        for j in range(1, w):
            acc = acc + ubuf_ref[POOL_HALO - j:POOL_HALO - j + tm, c0:c0 + POOL_GROUP_DIM]
        cnt = jnp.minimum(pos + 1, w).astype(F32)
        diffs.append((acc / cnt - ug).astype(BF16))
    diff_ref[0] = jnp.concatenate(diffs, axis=1)
    ubuf_ref[0:POOL_HALO, :] = u[tm - POOL_HALO:, :]


def _premix(x, mod, g, wqkv, wf, bf, wu, sel, *, tm):
    B, S, D = x.shape
    row_spec = lambda w: pl.BlockSpec((1, tm, w), lambda b, s: (b, s, 0))
    const = lambda a: pl.BlockSpec(a.shape, lambda b, s: (0,) * a.ndim)
    kaug_w = (ATTN_WIDTH // LANES) * AUG
    return pl.pallas_call(
        functools.partial(_premix_kernel, tm=tm),
        grid=(B, S // tm),
        in_specs=[
            row_spec(D),
            pl.BlockSpec((1, N_MOD, D), lambda b, s: (b, 0, 0)),
            const(g), const(wqkv), const(wf), const(bf), const(wu), const(sel),
        ],
        out_specs=[row_spec(ATTN_WIDTH), row_spec(kaug_w),
                   pl.BlockSpec((1, ATTN_WIDTH, tm), lambda b, s: (b, 0, s)), row_spec(POOL_WIDTH)],
        out_shape=[
            jax.ShapeDtypeStruct((B, S, ATTN_WIDTH), BF16),
            jax.ShapeDtypeStruct((B, S, kaug_w), BF16),
            jax.ShapeDtypeStruct((B, ATTN_WIDTH, S), BF16),
            jax.ShapeDtypeStruct((B, S, POOL_WIDTH), BF16),
        ],
        scratch_shapes=[pltpu.VMEM((1, LANES), F32), pltpu.VMEM((POOL_HALO + tm, POOL_WIDTH), F32)],
        compiler_params=pltpu.CompilerParams(
            dimension_semantics=("arbitrary", "arbitrary"), vmem_limit_bytes=VMEM_LIMIT),
        name="premix",
    )(x, mod, g, wqkv, wf, bf, wu, sel)


def _forget_routing():
    sel = np.zeros((3, LANES, (ATTN_WIDTH // LANES) * LANES), np.float32)
    for i in range(3):
        for h in range(ATTN_HEADS):
            sel[i, h, (h // 2) * LANES + 3 * (h % 2) + i] = 1.0
    return jnp.asarray(sel, BF16)


def _attn_kernel(q_ref, kaug_ref, vt_ref, o_ref, qt_ref, m_ref, acc_ref, *, tq, n_pairs):
    tk = tq
    qi = pl.program_id(1)
    n_heads = 2 * n_pairs
    r128 = lax.broadcasted_iota(jnp.int32, (LANES, tq), 0)
    for p in range(n_pairs):
        qt = q_ref[0, :, p * LANES:(p + 1) * LANES].astype(F32).T.astype(BF16)
        for hh in range(2):
            top = jnp.where((r128 >= HEAD_DIM * hh) & (r128 < HEAD_DIM * (hh + 1)), qt, jnp.zeros_like(qt))
            bot = ((r128 >= 3 * hh) & (r128 < 3 * hh + 3)).astype(BF16)
            qt_ref[2 * p + hh] = jnp.concatenate([top, bot], axis=0)
    m_ref[...] = jnp.full(m_ref.shape, NEG_INF, F32)
    acc_ref[...] = jnp.zeros(acc_ref.shape, F32)
    ones = jnp.ones((DENOM_ROWS, tk), BF16)

    def block(j, masked):
        k0 = pl.multiple_of(j * tk, tk)
        scores = []
        for h in range(n_heads):
            kb = kaug_ref[0, pl.ds(k0, tk), (h // 2) * AUG:(h // 2 + 1) * AUG]
            scores.append(jnp.dot(kb, qt_ref[h], preferred_element_type=F32))
        for h in range(n_heads):
            s = scores[h]
            if masked:
                key = lax.broadcasted_iota(jnp.int32, (tk, tq), 0)
                qry = lax.broadcasted_iota(jnp.int32, (tk, tq), 1)
                s = jnp.where(key <= qry, s, NEG_INF)
            m_prev = m_ref[h]
            m_new = jnp.maximum(m_prev, jnp.max(s, axis=0, keepdims=True))
            pt = jnp.exp(s - m_new).astype(BF16)
            alpha = jnp.exp(m_prev - m_new)
            vtb = vt_ref[0, pl.ds(h * HEAD_DIM, HEAD_DIM), pl.ds(k0, tk)]
            lhs = jnp.concatenate([vtb, ones], axis=0)
            acc_ref[h] = alpha * acc_ref[h] + jnp.dot(lhs, pt, preferred_element_type=F32)
            m_ref[h] = m_new

    def body(j, c):
        block(j, False)
        return c

    lax.fori_loop(0, qi, body, 0)
    block(qi, True)
    for p in range(n_pairs):
        outs = []
        for hh in range(2):
            a = acc_ref[2 * p + hh]
            outs.append(a[:HEAD_DIM, :] / a[HEAD_DIM:HEAD_DIM + 1, :])
        o_ref[0, :, p * LANES:(p + 1) * LANES] = jnp.concatenate(outs, axis=0).T.astype(BF16)


def _attention(q, kaug, vt, *, tq):
    B, S, W = q.shape
    n_pairs = W // LANES
    return pl.pallas_call(
        functools.partial(_attn_kernel, tq=tq, n_pairs=n_pairs),
        grid=(B, S // tq),
        in_specs=[pl.BlockSpec((1, tq, W), lambda b, i: (b, i, 0)),
                  pl.BlockSpec((1, S, n_pairs * AUG), lambda b, i: (b, 0, 0)),
                  pl.BlockSpec((1, W, S), lambda b, i: (b, 0, 0))],
        out_specs=pl.BlockSpec((1, tq, W), lambda b, i: (b, i, 0)),
        out_shape=jax.ShapeDtypeStruct((B, S, W), BF16),
        scratch_shapes=[pltpu.VMEM((2 * n_pairs, AUG, tq), BF16),
                        pltpu.VMEM((2 * n_pairs, 1, tq), F32),
                        pltpu.VMEM((2 * n_pairs, HEAD_DIM + DENOM_ROWS, tq), F32)],
        compiler_params=pltpu.CompilerParams(
            dimension_semantics=("arbitrary", "arbitrary"), vmem_limit_bytes=VMEM_LIMIT),
        name="fox_attention",
    )(q, kaug, vt)


def _postmix_kernel(attn_ref, diff_ref, x_ref, mod_ref, wpool_ref, pscale_ref, wout_ref, g_ref,
                    wr_hi_ref, wr_lo_ref, br_ref,
                    x1_ref, hslab_ref, ids_ref, wts_ref, *, tm):
    pooled = []
    for g in range(len(POOL_WINDOWS)):
        c0 = g * POOL_GROUP_DIM
        pooled.append(jnp.dot(diff_ref[0, :, c0:c0 + POOL_GROUP_DIM], wpool_ref[g], preferred_element_type=F32))
    pool_out = (jnp.concatenate(pooled, axis=1) * pscale_ref[...]).astype(BF16)
    cat = jnp.concatenate([attn_ref[0], pool_out], axis=1)
    mix = jnp.dot(cat, wout_ref[...], preferred_element_type=F32)
    x1 = x_ref[0] + mod_ref[0, 2:3, :] * mix
    x1_ref[0] = x1

    h = _rms_modulate(x1, g_ref[...], mod_ref[0, 3:4, :], mod_ref[0, 4:5, :])
    for j in range(h.shape[1] // LANES):
        hslab_ref[pl.ds(j, tm, stride=SUBLANES), :] = h[:, j * LANES:(j + 1) * LANES]

    h_hi = h.astype(BF16)
    h_lo = (h - h_hi.astype(F32)).astype(BF16)
    logits = (_nt_dot(wr_hi_ref[...], h_hi) + _nt_dot(wr_lo_ref[...], h_hi) + _nt_dot(wr_hi_ref[...], h_lo)
              + br_ref[...])
    sub = lax.broadcasted_iota(jnp.int32, (SUBLANES, tm), 0)
    lg = jnp.where(sub < N_EXPERT_GROUPS, logits[0:SUBLANES, :], NEG_INF)
    g_max = jnp.max(lg, axis=0, keepdims=True)
    top_p = 1.0 / jnp.sum(jnp.exp(lg - g_max), axis=0, keepdims=True)
    top_g = jnp.min(jnp.where(lg == g_max, sub, SUBLANES), axis=0, keepdims=True)
    le = logits[SUBLANES:2 * SUBLANES, :]
    for g in range(1, N_EXPERT_GROUPS):
        le = jnp.where(top_g == g, logits[(g + 1) * SUBLANES:(g + 2) * SUBLANES, :], le)
    v1 = jnp.max(le, axis=0, keepdims=True)
    i1 = jnp.min(jnp.where(le == v1, sub, SUBLANES), axis=0, keepdims=True)
    le2 = jnp.where(sub == i1, NEG_INF, le)
    v2 = jnp.max(le2, axis=0, keepdims=True)
    i2 = jnp.min(jnp.where(le2 == v2, sub, SUBLANES), axis=0, keepdims=True)
    e2 = jnp.exp(v2 - v1)
    w1 = top_p / (1.0 + e2)
    ids_ref[...] = jnp.concatenate([top_g * EXPERTS_PER_GROUP + i1, top_g * EXPERTS_PER_GROUP + i2], axis=0)
    wts_ref[...] = jnp.concatenate([w1, w1 * e2], axis=0)


def _postmix(attn, diff, x, mod, wpool, pscale, wout, g, wr_hi, wr_lo, br, *, tm):
    B, S, D = x.shape
    T = B * S
    nst = S // tm
    row_spec = lambda w: pl.BlockSpec((1, tm, w), lambda b, s: (b, s, 0))
    const = lambda a: pl.BlockSpec(a.shape, lambda b, s: (0,) * a.ndim)
    tok_spec = pl.BlockSpec((2, tm), lambda b, s: (0, b * nst + s))
    return pl.pallas_call(
        functools.partial(_postmix_kernel, tm=tm),
        grid=(B, nst),
        in_specs=[row_spec(ATTN_WIDTH), row_spec(POOL_WIDTH), row_spec(D),
                  pl.BlockSpec((1, N_MOD, D), lambda b, s: (b, 0, 0)),
                  const(wpool), const(pscale), const(wout), const(g), const(wr_hi), const(wr_lo), const(br)],
        out_specs=[row_spec(D),
                   pl.BlockSpec((tm * SUBLANES, LANES), lambda b, s: (b * nst + s, 0)),
                   tok_spec, tok_spec],
        out_shape=[
            jax.ShapeDtypeStruct((B, S, D), F32),
            jax.ShapeDtypeStruct((T * SUBLANES, LANES), F32),
            jax.ShapeDtypeStruct((2, T), jnp.int32),
            jax.ShapeDtypeStruct((2, T), F32),
        ],
        compiler_params=pltpu.CompilerParams(
            dimension_semantics=("arbitrary", "arbitrary"), vmem_limit_bytes=VMEM_LIMIT),
        name="postmix_router",
    )(attn, diff, x, mod, wpool, pscale, wout, g, wr_hi, wr_lo, br)


def _row_gather_start(idx_of, n_rows, src_hbm, dst, sem):
    def body(r, c):
        src0 = pl.multiple_of(idx_of(r) * SUBLANES, SUBLANES)
        dst0 = pl.multiple_of(r * SUBLANES, SUBLANES)
        pltpu.make_async_copy(src_hbm.at[pl.ds(src0, SUBLANES), :], dst.at[pl.ds(dst0, SUBLANES), :], sem).start()
        return c

    lax.fori_loop(0, n_rows, body, 0, unroll=8)


def _row_gather_wait(src_hbm, dst, sem):
    pltpu.make_async_copy(src_hbm.at[pl.ds(0, dst.shape[0]), :], dst, sem).wait()


def _slab_rows(ref, tm, width):
    return jnp.concatenate([ref[pl.ds(j, tm, stride=SUBLANES), :] for j in range(width // LANES)], axis=1)


def _expert_kernel(te_ref, tv_ref, idx_ref, h_hbm, wg_ref, wu_ref, wd_ref, o_ref, buf, sem, *, tm, nt):
    i = pl.program_id(0)

    @pl.when(jnp.logical_and(i < nt, tv_ref[jnp.minimum(i, nt - 1)] == 1))
    def _():
        slot = i % 2
        _row_gather_start(lambda r: idx_ref[0, 0, r], tm, h_hbm, buf.at[slot], sem.at[slot])

    @pl.when(i > 0)
    def _():
        t = i - 1
        slot = t % 2

        @pl.when(tv_ref[t] == 1)
        def _():
            _row_gather_wait(h_hbm, buf.at[slot], sem.at[slot])
            d = wg_ref.shape[1]
            x = _slab_rows(buf.at[slot], tm, d).astype(BF16)
            a = jnp.dot(x, wg_ref[0].astype(BF16), preferred_element_type=F32)
            b = jnp.dot(x, wu_ref[0].astype(BF16), preferred_element_type=F32)
            act = (_silu(a) * b).astype(BF16)
            o = jnp.dot(act, wd_ref[0].astype(BF16), preferred_element_type=F32)
            for j in range(d // LANES):
                o_ref[pl.ds(j, tm, stride=SUBLANES), :] = o[:, j * LANES:(j + 1) * LANES]

        @pl.when(tv_ref[t] == 0)
        def _():
            o_ref[...] = jnp.zeros(o_ref.shape, F32)


def _experts(tile_expert, tile_valid, tok_sorted, hslab, wg, wu, wd, *, tm):
    nt = tile_expert.shape[0]
    E, D, Fe = wg.shape
    prev = lambda i: jnp.maximum(i - 1, 0)
    grid_spec = pltpu.PrefetchScalarGridSpec(
        num_scalar_prefetch=2,
        grid=(nt + 1,),
        in_specs=[
            pl.BlockSpec((1, 1, tm), lambda i, te, tv: (jnp.minimum(i, nt - 1), 0, 0), memory_space=pltpu.SMEM),
            pl.BlockSpec(memory_space=pl.ANY),
            pl.BlockSpec((1, D, Fe), lambda i, te, tv: (te[prev(i)], 0, 0)),
            pl.BlockSpec((1, D, Fe), lambda i, te, tv: (te[prev(i)], 0, 0)),
            pl.BlockSpec((1, Fe, D), lambda i, te, tv: (te[prev(i)], 0, 0)),
        ],
        out_specs=pl.BlockSpec((tm * SUBLANES, LANES), lambda i, te, tv: (prev(i), 0)),
        scratch_shapes=[pltpu.VMEM((2, tm * SUBLANES, LANES), F32), pltpu.SemaphoreType.DMA((2,))],
    )
    return pl.pallas_call(
        functools.partial(_expert_kernel, tm=tm, nt=nt),
        grid_spec=grid_spec,
        out_shape=jax.ShapeDtypeStruct((nt * tm * SUBLANES, LANES), F32),
        compiler_params=pltpu.CompilerParams(dimension_semantics=("arbitrary",), vmem_limit_bytes=VMEM_LIMIT),
        name="moe_experts",
    )(tile_expert, tile_valid, tok_sorted.reshape(nt, 1, tm), hslab, wg, wu, wd)


def _combine_kernel(pos_ref, o_hbm, x1_ref, wts_ref, mod_ref, *rest, tm, nt, final):
    if final:
        gf_ref, out_ref, buf, sem = rest
    else:
        out_ref, buf, sem = rest
    i = pl.program_id(0)

    @pl.when(i < nt)
    def _():
        slot = i % 2
        for k in range(2):
            _row_gather_start(lambda r: pos_ref[0, k, r], tm, o_hbm, buf.at[slot, k], sem.at[slot, k])

    @pl.when(i > 0)
    def _():
        slot = (i - 1) % 2
        d = x1_ref.shape[2]
        y = None
        for k in range(2):
            _row_gather_wait(o_hbm, buf.at[slot, k], sem.at[slot, k])
            wk = jnp.broadcast_to(wts_ref[k:k + 1, :], (LANES, tm)).T
            ok = _slab_rows(buf.at[slot, k], tm, d) * jnp.concatenate([wk] * (d // LANES), axis=1)
            y = ok if y is None else y + ok
        x2 = x1_ref[0] + mod_ref[0, 5:6, :] * y
        if final:
            ms = jnp.mean(x2 * x2, axis=-1, keepdims=True)
            x2 = x2 * lax.rsqrt(ms + EPS) * gf_ref[...]
        out_ref[0] = x2


def _combine(pos, oslab, x1, wts, mod, g_final, *, tm):
    B, S, D = x1.shape
    nst = S // tm
    nt = B * nst
    prev = lambda i: jnp.maximum(i - 1, 0)
    final = g_final is not None
    in_specs = [
        pl.BlockSpec((1, 2, tm), lambda i: (jnp.minimum(i, nt - 1), 0, 0), memory_space=pltpu.SMEM),
        pl.BlockSpec(memory_space=pl.ANY),
        pl.BlockSpec((1, tm, D), lambda i: (prev(i) // nst, prev(i) % nst, 0)),
        pl.BlockSpec((2, tm), lambda i: (0, prev(i))),
        pl.BlockSpec((1, N_MOD, D), lambda i: (prev(i) // nst, 0, 0)),
    ]
    args = [pos.reshape(2, nt, tm).transpose(1, 0, 2), oslab, x1, wts, mod]
    if final:
        in_specs.append(pl.BlockSpec((1, D), lambda i: (0, 0)))
        args.append(g_final)
    return pl.pallas_call(
        functools.partial(_combine_kernel, tm=tm, nt=nt, final=final),
        grid=(nt + 1,),
        in_specs=in_specs,
        out_specs=pl.BlockSpec((1, tm, D), lambda i: (prev(i) // nst, prev(i) % nst, 0)),
        out_shape=jax.ShapeDtypeStruct((B, S, D), F32),
        scratch_shapes=[pltpu.VMEM((2, 2, tm * SUBLANES, LANES), F32), pltpu.SemaphoreType.DMA((2, 2))],
        compiler_params=pltpu.CompilerParams(dimension_semantics=("arbitrary",), vmem_limit_bytes=VMEM_LIMIT),
        name="moe_combine",
    )(*args)


def _dispatch_plan(ids, *, tm, nt):
    T = ids.shape[1]
    e = ids.reshape(-1)
    onehot = (e[:, None] == jnp.arange(N_EXPERTS, dtype=jnp.int32)[None, :]).astype(jnp.int32)
    csum = jnp.cumsum(onehot, axis=0)
    rank = jnp.sum(csum * onehot, axis=1) - 1
    counts = csum[-1]
    padded = ((counts + tm - 1) // tm) * tm
    ends = jnp.cumsum(padded)
    pos = (ends - padded)[e] + rank
    tok = jnp.tile(jnp.arange(T, dtype=jnp.int32), 2)
    tok_sorted = jnp.zeros((nt * tm,), jnp.int32).at[pos].set(tok)
    starts = jnp.arange(nt, dtype=jnp.int32) * tm
    tile_expert = jnp.minimum(jnp.sum((starts[:, None] >= ends[None, :]).astype(jnp.int32), axis=1), N_EXPERTS - 1)
    tile_valid = (starts < ends[-1]).astype(jnp.int32)
    return pos.astype(jnp.int32), tok_sorted, tile_expert, tile_valid


def kernel(x, c, norm_mix_g, norm_ffn_g, norm_final_g, w_ada, b_ada, w_in, b_fgate, w_pool, pool_scale, w_out,
           w_router_group, b_router_group, w_router_expert, b_router_expert, w_expert_gate, w_expert_up,
           w_expert_down):
    B, S, D = x.shape
    L = w_ada.shape[0]
    T = B * S
    tm_mix = min(512, S)
    tq = min(256, S)
    tm_e = 256
    tm_c = min(256, S)
    nt_e = (2 * T) // tm_e + N_EXPERTS

    mod_all = _ada_modulation(c, w_ada, b_ada).reshape(L, B, N_MOD, D)
    Fe = w_expert_gate.shape[-1]
    sel = _forget_routing()
    for l in range(L):
        mod = mod_all[l]
        w_in_l = w_in[l]
        wqkv = w_in_l[:, :3 * ATTN_WIDTH].astype(BF16)
        wf = jnp.pad(w_in_l[:, 3 * ATTN_WIDTH:3 * ATTN_WIDTH + ATTN_HEADS], ((0, 0), (0, LANES - ATTN_HEADS))).astype(BF16)
        bf = jnp.pad(b_fgate[l].astype(F32), (0, LANES - ATTN_HEADS)).reshape(1, LANES)
        wu = w_in_l[:, 3 * ATTN_WIDTH + ATTN_HEADS:].astype(BF16)
        q, kaug, vt, diff = _premix(x, mod, norm_mix_g[l].reshape(1, D), wqkv, wf, bf, wu, sel, tm=tm_mix)
        attn = _attention(q, kaug, vt, tq=tq)

        wr = jnp.concatenate([
            jnp.pad(w_router_group[l].T, ((0, SUBLANES - N_EXPERT_GROUPS), (0, 0))),
            w_router_expert[l].transpose(0, 2, 1).reshape(N_EXPERTS, D)], axis=0)
        wr_hi = wr.astype(BF16)
        wr_lo = (wr - wr_hi.astype(F32)).astype(BF16)
        br = jnp.concatenate([jnp.pad(b_router_group[l], (0, SUBLANES - N_EXPERT_GROUPS)),
                              b_router_expert[l].reshape(N_EXPERTS)]).reshape(ROUTER_ROWS, 1).astype(F32)
        x1, hslab, ids, wts = _postmix(attn, diff, x, mod, w_pool[l].astype(BF16), pool_scale[l].reshape(1, POOL_WIDTH),
                                       w_out[l].astype(BF16), norm_ffn_g[l].reshape(1, D), wr_hi, wr_lo, br, tm=tm_mix)

        pos, tok_sorted, tile_expert, tile_valid = _dispatch_plan(ids, tm=tm_e, nt=nt_e)
        oslab = _experts(tile_expert, tile_valid, tok_sorted, hslab,
                         w_expert_gate[l].reshape(N_EXPERTS, D, Fe), w_expert_up[l].reshape(N_EXPERTS, D, Fe),
                         w_expert_down[l].reshape(N_EXPERTS, Fe, D), tm=tm_e)
        g_final = norm_final_g.reshape(1, D) if l == L - 1 else None
        x = _combine(pos, oslab, x1, wts, mod, g_final, tm=tm_c)
    return x
```

```python
import functools

import jax
import jax.numpy as jnp
import numpy as np
from jax import lax
from jax.experimental import pallas as pl
from jax.experimental.pallas import tpu as pltpu

ATTN_HEADS = 8
HEAD_DIM = 64
ATTN_WIDTH = ATTN_HEADS * HEAD_DIM
POOL_WINDOWS = (2, 4, 8, 16)
POOL_GROUP_DIM = 128
POOL_WIDTH = POOL_GROUP_DIM * len(POOL_WINDOWS)
POOL_HALO = 16
N_EXPERT_GROUPS = 4
EXPERTS_PER_GROUP = 8
N_EXPERTS = N_EXPERT_GROUPS * EXPERTS_PER_GROUP
N_MOD = 6
EPS = 1e-6
NEG_INF = -1e30

LANES = 128
SUBLANES = 8
AUG = 2 * LANES
DENOM_ROWS = 16
ROUTER_ROWS = 40
CHUNK = 16
TOP_K = 2
VMEM_LIMIT = 48 * 1024 * 1024


def _sorted_rows(tm):
    worst = TOP_K * tm + N_EXPERTS * (CHUNK - 1)
    return (worst // LANES + 1) * LANES

F32 = jnp.float32
BF16 = jnp.bfloat16


def _silu(a):
    return a * jax.nn.sigmoid(a)


def _nt_dot(a, b):
    return lax.dot_general(a, b, (((1,), (1,)), ((), ())), preferred_element_type=F32)


def _split3(a):
    t0 = a.astype(BF16)
    r1 = a - t0.astype(F32)
    t1 = r1.astype(BF16)
    t2 = (r1 - t1.astype(F32)).astype(BF16)
    return t0, t1, t2


def _rms_modulate(x, g, shift, scale):
    ms = jnp.mean(x * x, axis=-1, keepdims=True)
    y = x * lax.rsqrt(ms + EPS) * g
    return y * (1.0 + scale) + shift


def _ada_kernel(c_ref, w_ref, b_ref, o_ref):
    ca = _silu(c_ref[...])
    o_ref[0] = jnp.dot(ca, w_ref[0], precision=lax.Precision.HIGHEST, preferred_element_type=F32) + b_ref[0]


def _ada_modulation(c, w_ada, b_ada):
    L, D, W = w_ada.shape
    B = c.shape[0]
    tn = W // 4
    return pl.pallas_call(
        _ada_kernel,
        grid=(L, W // tn),
        in_specs=[
            pl.BlockSpec((B, D), lambda l, n: (0, 0)),
            pl.BlockSpec((1, D, tn), lambda l, n: (l, 0, n)),
            pl.BlockSpec((1, 1, tn), lambda l, n: (l, 0, n)),
        ],
        out_specs=pl.BlockSpec((1, B, tn), lambda l, n: (l, 0, n)),
        out_shape=jax.ShapeDtypeStruct((L, B, W), F32),
        compiler_params=pltpu.CompilerParams(vmem_limit_bytes=VMEM_LIMIT),
        name="ada_modulation",
    )(c, w_ada, b_ada.reshape(L, 1, W))


def _premix_kernel(x_ref, mod_ref, g_ref, wqkv_ref, wf_ref, bf_ref, wu_ref, sel_ref,
                   q_ref, kaug_ref, vt_ref, diff_ref, carry_ref, ubuf_ref, *, tm):
    si = pl.program_id(1)

    @pl.when(si == 0)
    def _():
        carry_ref[...] = jnp.zeros_like(carry_ref)
        ubuf_ref[0:POOL_HALO, :] = jnp.zeros((POOL_HALO, POOL_WIDTH), F32)

    h = _rms_modulate(x_ref[0], g_ref[...], mod_ref[0, 0:1, :], mod_ref[0, 1:2, :]).astype(BF16)
    qkv = jnp.dot(h, wqkv_ref[...], preferred_element_type=F32)
    q_ref[0] = (qkv[:, :ATTN_WIDTH] * (HEAD_DIM ** -0.5)).astype(BF16)
    vt_ref[0] = qkv[:, 2 * ATTN_WIDTH:].T.astype(BF16)

    fl = jnp.dot(h, wf_ref[...], preferred_element_type=F32) + bf_ref[...]
    lf = jnp.minimum(fl, 0.0) - jnp.log1p(jnp.exp(-jnp.abs(fl)))
    row = lax.broadcasted_iota(jnp.int32, (tm, tm), 0)
    col = lax.broadcasted_iota(jnp.int32, (tm, tm), 1)
    tri = (row >= col).astype(BF16)
    cs = None
    for term in _split3(lf):
        d = jnp.dot(tri, term, preferred_element_type=F32)
        cs = d if cs is None else cs + d
    f_cum = cs + carry_ref[...]
    carry_ref[...] = f_cum[tm - 1:tm, :]

    aug = None
    for i, term in enumerate(_split3(-f_cum)):
        d = jnp.dot(term, sel_ref[i], preferred_element_type=F32)
        aug = d if aug is None else aug + d
    parts = []
    for p in range(ATTN_WIDTH // LANES):
        parts.append(qkv[:, ATTN_WIDTH + p * LANES:ATTN_WIDTH + (p + 1) * LANES].astype(BF16))
        parts.append(aug[:, p * LANES:(p + 1) * LANES].astype(BF16))
    kaug_ref[0] = jnp.concatenate(parts, axis=1)

    u = jnp.dot(h, wu_ref[...], preferred_element_type=F32)
    ubuf_ref[POOL_HALO:POOL_HALO + tm, :] = u
    pos = si * tm + lax.broadcasted_iota(jnp.int32, (tm, POOL_GROUP_DIM), 0)
    diffs = []
    for g, w in enumerate(POOL_WINDOWS):
        c0 = g * POOL_GROUP_DIM
        ug = u[:, c0:c0 + POOL_GROUP_DIM]
        acc = ug
        for j in range(1, w):
            acc = acc + ubuf_ref[POOL_HALO - j:POOL_HALO - j + tm, c0:c0 + POOL_GROUP_DIM]
        cnt = jnp.minimum(pos + 1, w).astype(F32)
        diffs.append((acc / cnt - ug).astype(BF16))
    diff_ref[0] = jnp.concatenate(diffs, axis=1)
    ubuf_ref[0:POOL_HALO, :] = u[tm - POOL_HALO:, :]


def _premix(x, mod, g, wqkv, wf, bf, wu, sel, *, tm):
    B, S, D = x.shape
    row_spec = lambda w: pl.BlockSpec((1, tm, w), lambda b, s: (b, s, 0))
    const = lambda a: pl.BlockSpec(a.shape, lambda b, s: (0,) * a.ndim)
    kaug_w = (ATTN_WIDTH // LANES) * AUG
    return pl.pallas_call(
        functools.partial(_premix_kernel, tm=tm),
        grid=(B, S // tm),
        in_specs=[
            row_spec(D),
            pl.BlockSpec((1, N_MOD, D), lambda b, s: (b, 0, 0)),
            const(g), const(wqkv), const(wf), const(bf), const(wu), const(sel),
        ],
        out_specs=[row_spec(ATTN_WIDTH), row_spec(kaug_w),
                   pl.BlockSpec((1, ATTN_WIDTH, tm), lambda b, s: (b, 0, s)), row_spec(POOL_WIDTH)],
        out_shape=[
            jax.ShapeDtypeStruct((B, S, ATTN_WIDTH), BF16),
            jax.ShapeDtypeStruct((B, S, kaug_w), BF16),
            jax.ShapeDtypeStruct((B, ATTN_WIDTH, S), BF16),
            jax.ShapeDtypeStruct((B, S, POOL_WIDTH), BF16),
        ],
        scratch_shapes=[pltpu.VMEM((1, LANES), F32), pltpu.VMEM((POOL_HALO + tm, POOL_WIDTH), F32)],
        compiler_params=pltpu.CompilerParams(
            dimension_semantics=("arbitrary", "arbitrary"), vmem_limit_bytes=VMEM_LIMIT),
        name="premix",
    )(x, mod, g, wqkv, wf, bf, wu, sel)


def _forget_routing():
    sel = np.zeros((3, LANES, (ATTN_WIDTH // LANES) * LANES), np.float32)
    for i in range(3):
        for h in range(ATTN_HEADS):
            sel[i, h, (h // 2) * LANES + 3 * (h % 2) + i] = 1.0
    return jnp.asarray(sel, BF16)


def _attn_kernel(q_ref, kaug_ref, vt_ref, o_ref, qt_ref, m_ref, acc_ref, *, tq, n_pairs):
    tk = tq
    qi = pl.program_id(1)
    n_heads = 2 * n_pairs
    r128 = lax.broadcasted_iota(jnp.int32, (LANES, tq), 0)
    for p in range(n_pairs):
        qt = q_ref[0, :, p * LANES:(p + 1) * LANES].astype(F32).T.astype(BF16)
        for hh in range(2):
            top = jnp.where((r128 >= HEAD_DIM * hh) & (r128 < HEAD_DIM * (hh + 1)), qt, jnp.zeros_like(qt))
            bot = ((r128 >= 3 * hh) & (r128 < 3 * hh + 3)).astype(BF16)
            qt_ref[2 * p + hh] = jnp.concatenate([top, bot], axis=0)
    m_ref[...] = jnp.full(m_ref.shape, NEG_INF, F32)
    acc_ref[...] = jnp.zeros(acc_ref.shape, F32)
    ones = jnp.ones((DENOM_ROWS, tk), BF16)

    def block(j, masked):
        k0 = pl.multiple_of(j * tk, tk)
        scores = []
        for h in range(n_heads):
            kb = kaug_ref[0, pl.ds(k0, tk), (h // 2) * AUG:(h // 2 + 1) * AUG]
            scores.append(jnp.dot(kb, qt_ref[h], preferred_element_type=F32))
        for h in range(n_heads):
            s = scores[h]
            if masked:
                key = lax.broadcasted_iota(jnp.int32, (tk, tq), 0)
                qry = lax.broadcasted_iota(jnp.int32, (tk, tq), 1)
                s = jnp.where(key <= qry, s, NEG_INF)
            m_prev = m_ref[h]
            m_new = jnp.maximum(m_prev, jnp.max(s, axis=0, keepdims=True))
            pt = jnp.exp(s - m_new).astype(BF16)
            alpha = jnp.exp(m_prev - m_new)
            vtb = vt_ref[0, pl.ds(h * HEAD_DIM, HEAD_DIM), pl.ds(k0, tk)]
            lhs = jnp.concatenate([vtb, ones], axis=0)
            acc_ref[h] = alpha * acc_ref[h] + jnp.dot(lhs, pt, preferred_element_type=F32)
            m_ref[h] = m_new

    def body(j, c):
        block(j, False)
        return c

    lax.fori_loop(0, qi, body, 0)
    block(qi, True)
    for p in range(n_pairs):
        outs = []
        for hh in range(2):
            a = acc_ref[2 * p + hh]
            outs.append(a[:HEAD_DIM, :] / a[HEAD_DIM:HEAD_DIM + 1, :])
        o_ref[0, :, p * LANES:(p + 1) * LANES] = jnp.concatenate(outs, axis=0).T.astype(BF16)


def _attention(q, kaug, vt, *, tq):
    B, S, W = q.shape
    n_pairs = W // LANES
    return pl.pallas_call(
        functools.partial(_attn_kernel, tq=tq, n_pairs=n_pairs),
        grid=(B, S // tq),
        in_specs=[pl.BlockSpec((1, tq, W), lambda b, i: (b, i, 0)),
                  pl.BlockSpec((1, S, n_pairs * AUG), lambda b, i: (b, 0, 0)),
                  pl.BlockSpec((1, W, S), lambda b, i: (b, 0, 0))],
        out_specs=pl.BlockSpec((1, tq, W), lambda b, i: (b, i, 0)),
        out_shape=jax.ShapeDtypeStruct((B, S, W), BF16),
        scratch_shapes=[pltpu.VMEM((2 * n_pairs, AUG, tq), BF16),
                        pltpu.VMEM((2 * n_pairs, 1, tq), F32),
                        pltpu.VMEM((2 * n_pairs, HEAD_DIM + DENOM_ROWS, tq), F32)],
        compiler_params=pltpu.CompilerParams(
            dimension_semantics=("arbitrary", "arbitrary"), vmem_limit_bytes=VMEM_LIMIT),
        name="fox_attention",
    )(q, kaug, vt)


def _postmix_kernel(attn_ref, diff_ref, x_ref, mod_ref, wpool_ref, pscale_ref, wout_ref, g_ref,
                    wr_hi_ref, wr_lo_ref, br_ref, before_ref,
                    x1_ref, xs_ref, ids_ref, wts_ref, pos_ref, *, tm):
    pooled = []
    for g in range(len(POOL_WINDOWS)):
        c0 = g * POOL_GROUP_DIM
        pooled.append(jnp.dot(diff_ref[0, :, c0:c0 + POOL_GROUP_DIM], wpool_ref[g], preferred_element_type=F32))
    pool_out = (jnp.concatenate(pooled, axis=1) * pscale_ref[...]).astype(BF16)
    cat = jnp.concatenate([attn_ref[0], pool_out], axis=1)
    mix = jnp.dot(cat, wout_ref[...], preferred_element_type=F32)
    x1 = x_ref[0] + mod_ref[0, 2:3, :] * mix
    x1_ref[0] = x1

    h = _rms_modulate(x1, g_ref[...], mod_ref[0, 3:4, :], mod_ref[0, 4:5, :])

    h_hi = h.astype(BF16)
    h_lo = (h - h_hi.astype(F32)).astype(BF16)
    logits = (_nt_dot(wr_hi_ref[...], h_hi) + _nt_dot(wr_lo_ref[...], h_hi) + _nt_dot(wr_hi_ref[...], h_lo)
              + br_ref[...])
    sub = lax.broadcasted_iota(jnp.int32, (SUBLANES, tm), 0)
    lg = jnp.where(sub < N_EXPERT_GROUPS, logits[0:SUBLANES, :], NEG_INF)
    g_max = jnp.max(lg, axis=0, keepdims=True)
    top_p = 1.0 / jnp.sum(jnp.exp(lg - g_max), axis=0, keepdims=True)
    top_g = jnp.min(jnp.where(lg == g_max, sub, SUBLANES), axis=0, keepdims=True)
    le = logits[SUBLANES:2 * SUBLANES, :]
    for g in range(1, N_EXPERT_GROUPS):
        le = jnp.where(top_g == g, logits[(g + 1) * SUBLANES:(g + 2) * SUBLANES, :], le)
    v1 = jnp.max(le, axis=0, keepdims=True)
    i1 = jnp.min(jnp.where(le == v1, sub, SUBLANES), axis=0, keepdims=True)
    le2 = jnp.where(sub == i1, NEG_INF, le)
    v2 = jnp.max(le2, axis=0, keepdims=True)
    i2 = jnp.min(jnp.where(le2 == v2, sub, SUBLANES), axis=0, keepdims=True)
    e2 = jnp.exp(v2 - v1)
    w1 = top_p / (1.0 + e2)
    id0 = top_g * EXPERTS_PER_GROUP + i1
    id1 = top_g * EXPERTS_PER_GROUP + i2
    ids_ref[...] = jnp.concatenate([id0, id1], axis=0)
    wts_ref[...] = jnp.concatenate([w1, w1 * e2], axis=0)

    sub_e = lax.broadcasted_iota(jnp.int32, (N_EXPERTS, tm), 0)
    onehot = jnp.concatenate([sub_e == id0, sub_e == id1], axis=1)
    oh_f = onehot.astype(F32)
    rank = jnp.dot(onehot.astype(BF16), before_ref[...], preferred_element_type=F32)
    chunks = jnp.floor((jnp.sum(oh_f, axis=1, keepdims=True) + (CHUNK - 1.0)) * (1.0 / CHUNK))
    er = lax.broadcasted_iota(jnp.int32, (N_EXPERTS, N_EXPERTS), 0)
    ec = lax.broadcasted_iota(jnp.int32, (N_EXPERTS, N_EXPERTS), 1)
    first_chunk = jnp.dot((er > ec).astype(BF16), jnp.broadcast_to(chunks, (N_EXPERTS, LANES)).astype(BF16),
                          preferred_element_type=F32)[:, 0:1]
    pos = jnp.sum(oh_f * (rank + CHUNK * first_chunk), axis=0, keepdims=True)
    pos0 = pos[:, :tm]
    pos1 = pos[:, tm:]
    pos_ref[...] = jnp.concatenate([pos0, pos1], axis=0)
    r_iota = lax.broadcasted_iota(jnp.int32, (xs_ref.shape[0], tm), 0)
    perm = ((r_iota == pos0.astype(jnp.int32)) | (r_iota == pos1.astype(jnp.int32))).astype(BF16)
    xs_ref[...] = jnp.dot(perm, h_hi, preferred_element_type=F32).astype(BF16)


def _postmix(attn, diff, x, mod, wpool, pscale, wout, g, wr_hi, wr_lo, br, *, tm):
    B, S, D = x.shape
    T = B * S
    nst = S // tm
    rows = _sorted_rows(tm)
    before = jnp.asarray(np.triu(np.ones((TOP_K * tm, TOP_K * tm), np.float32), k=1), BF16)
    row_spec = lambda w: pl.BlockSpec((1, tm, w), lambda b, s: (b, s, 0))
    const = lambda a: pl.BlockSpec(a.shape, lambda b, s: (0,) * a.ndim)
    tok_spec = pl.BlockSpec((TOP_K, tm), lambda b, s: (0, b * nst + s))
    return pl.pallas_call(
        functools.partial(_postmix_kernel, tm=tm),
        grid=(B, nst),
        in_specs=[row_spec(ATTN_WIDTH), row_spec(POOL_WIDTH), row_spec(D),
                  pl.BlockSpec((1, N_MOD, D), lambda b, s: (b, 0, 0)),
                  const(wpool), const(pscale), const(wout), const(g), const(wr_hi), const(wr_lo), const(br),
                  const(before)],
        out_specs=[row_spec(D),
                   pl.BlockSpec((rows, D), lambda b, s: (b * nst + s, 0)),
                   tok_spec, tok_spec, tok_spec],
        out_shape=[
            jax.ShapeDtypeStruct((B, S, D), F32),
            jax.ShapeDtypeStruct((B * nst * rows, D), BF16),
            jax.ShapeDtypeStruct((TOP_K, T), jnp.int32),
            jax.ShapeDtypeStruct((TOP_K, T), F32),
            jax.ShapeDtypeStruct((TOP_K, T), F32),
        ],
        compiler_params=pltpu.CompilerParams(
            dimension_semantics=("arbitrary", "arbitrary"), vmem_limit_bytes=VMEM_LIMIT),
        name="postmix_router",
    )(attn, diff, x, mod, wpool, pscale, wout, g, wr_hi, wr_lo, br, before)


def _chunk_copy(src_hbm, src_row, dst, dst_row, sem):
    return pltpu.make_async_copy(src_hbm.at[pl.ds(pl.multiple_of(src_row, CHUNK), CHUNK), :],
                                 dst.at[pl.ds(pl.multiple_of(dst_row, CHUNK), CHUNK), :], sem)


def _expert_kernel(te_ref, tv_ref, src_ref, xs_hbm, wg_ref, wu_ref, wd_ref, o_ref, buf, sem, *, tm, nt):
    i = pl.program_id(0)
    n_chunks = tm // CHUNK

    @pl.when(jnp.logical_and(i < nt, tv_ref[jnp.minimum(i, nt - 1)] == 1))
    def _():
        slot = i % 2
        for c in range(n_chunks):
            _chunk_copy(xs_hbm, src_ref[i * n_chunks + c], buf.at[slot], c * CHUNK, sem.at[slot]).start()

    @pl.when(i > 0)
    def _():
        t = i - 1
        slot = t % 2

        @pl.when(tv_ref[t] == 1)
        def _():
            for c in range(n_chunks):
                _chunk_copy(xs_hbm, 0, buf.at[slot], c * CHUNK, sem.at[slot]).wait()
            x = buf[slot]
            a = jnp.dot(x, wg_ref[0].astype(BF16), preferred_element_type=F32)
            b = jnp.dot(x, wu_ref[0].astype(BF16), preferred_element_type=F32)
            act = (_silu(a) * b).astype(BF16)
            o_ref[...] = jnp.dot(act, wd_ref[0].astype(BF16), preferred_element_type=F32).astype(BF16)

        @pl.when(tv_ref[t] == 0)
        def _():
            o_ref[...] = jnp.zeros(o_ref.shape, BF16)


def _experts(tile_expert, tile_valid, chunk_src, xs, wg, wu, wd, *, tm):
    nt = tile_expert.shape[0]
    E, D, Fe = wg.shape
    prev = lambda i: jnp.maximum(i - 1, 0)
    grid_spec = pltpu.PrefetchScalarGridSpec(
        num_scalar_prefetch=3,
        grid=(nt + 1,),
        in_specs=[
            pl.BlockSpec(memory_space=pl.ANY),
            pl.BlockSpec((1, D, Fe), lambda i, te, tv, cs: (te[prev(i)], 0, 0)),
            pl.BlockSpec((1, D, Fe), lambda i, te, tv, cs: (te[prev(i)], 0, 0)),
            pl.BlockSpec((1, Fe, D), lambda i, te, tv, cs: (te[prev(i)], 0, 0)),
        ],
        out_specs=pl.BlockSpec((tm, D), lambda i, te, tv, cs: (prev(i), 0)),
        scratch_shapes=[pltpu.VMEM((2, tm, D), BF16), pltpu.SemaphoreType.DMA((2,))],
    )
    return pl.pallas_call(
        functools.partial(_expert_kernel, tm=tm, nt=nt),
        grid_spec=grid_spec,
        out_shape=jax.ShapeDtypeStruct((nt * tm, D), BF16),
        compiler_params=pltpu.CompilerParams(dimension_semantics=("arbitrary",), vmem_limit_bytes=VMEM_LIMIT),
        name="moe_experts",
    )(tile_expert, tile_valid, chunk_src, xs, wg, wu, wd)


def _combine_kernel(nch_ref, dst_ref, o_hbm, x1_ref, pos_ref, wts_ref, mod_ref, *rest, tm, nt, max_chunks, final):
    if final:
        gf_ref, out_ref, buf, sem = rest
    else:
        out_ref, buf, sem = rest
    i = pl.program_id(0)

    @pl.when(i == 0)
    def _():
        buf[...] = jnp.zeros(buf.shape, BF16)

    @pl.when(i < nt)
    def _():
        slot = i % 2

        def body(c, carry):
            _chunk_copy(o_hbm, dst_ref[i * max_chunks + c], buf.at[slot], c * CHUNK, sem.at[slot]).start()
            return carry

        lax.fori_loop(0, nch_ref[jnp.minimum(i, nt - 1)], body, 0)

    @pl.when(i > 0)
    def _():
        t = i - 1
        slot = t % 2

        def body(c, carry):
            _chunk_copy(o_hbm, 0, buf.at[slot], c * CHUNK, sem.at[slot]).wait()
            return carry

        lax.fori_loop(0, nch_ref[t], body, 0)
        rows = buf.shape[1]
        rep = lambda r: jnp.broadcast_to(r, (LANES, tm)).T
        p0, p1 = rep(pos_ref[0:1, :]), rep(pos_ref[1:2, :])
        w0, w1 = rep(wts_ref[0:1, :]), rep(wts_ref[1:2, :])
        lane = lax.broadcasted_iota(jnp.int32, (tm, LANES), 1).astype(F32)
        cols = []
        for c in range(rows // LANES):
            r = lane + float(c * LANES)
            cols.append((jnp.where(p0 == r, w0, 0.0) + jnp.where(p1 == r, w1, 0.0)).astype(BF16))
        comb = jnp.concatenate(cols, axis=1)
        y = jnp.dot(comb, buf[slot], preferred_element_type=F32)
        x2 = x1_ref[0] + mod_ref[0, 5:6, :] * y
        if final:
            ms = jnp.mean(x2 * x2, axis=-1, keepdims=True)
            x2 = x2 * lax.rsqrt(ms + EPS) * gf_ref[...]
        out_ref[0] = x2


def _combine(n_chunks, chunk_dst, o_sorted, x1, pos, wts, mod, g_final, *, tm):
    B, S, D = x1.shape
    nst = S // tm
    nt = B * nst
    rows = _sorted_rows(tm)
    max_chunks = chunk_dst.shape[0] // nt
    prev = lambda i: jnp.maximum(i - 1, 0)
    final = g_final is not None
    tok_spec = pl.BlockSpec((TOP_K, tm), lambda i, nc, cd: (0, prev(i)))
    in_specs = [
        pl.BlockSpec(memory_space=pl.ANY),
        pl.BlockSpec((1, tm, D), lambda i, nc, cd: (prev(i) // nst, prev(i) % nst, 0)),
        tok_spec, tok_spec,
        pl.BlockSpec((1, N_MOD, D), lambda i, nc, cd: (prev(i) // nst, 0, 0)),
    ]
    args = [o_sorted, x1, pos, wts, mod]
    if final:
        in_specs.append(pl.BlockSpec((1, D), lambda i, nc, cd: (0, 0)))
        args.append(g_final)
    grid_spec = pltpu.PrefetchScalarGridSpec(
        num_scalar_prefetch=2,
        grid=(nt + 1,),
        in_specs=in_specs,
        out_specs=pl.BlockSpec((1, tm, D), lambda i, nc, cd: (prev(i) // nst, prev(i) % nst, 0)),
        scratch_shapes=[pltpu.VMEM((2, rows, D), BF16), pltpu.SemaphoreType.DMA((2,))],
    )
    return pl.pallas_call(
        functools.partial(_combine_kernel, tm=tm, nt=nt, max_chunks=max_chunks, final=final),
        grid_spec=grid_spec,
        out_shape=jax.ShapeDtypeStruct((B, S, D), F32),
        compiler_params=pltpu.CompilerParams(dimension_semantics=("arbitrary",), vmem_limit_bytes=VMEM_LIMIT),
        name="moe_combine",
    )(n_chunks, chunk_dst, *args)


def _dispatch_tables(ids, *, tm, tm_e, nt_e):
    T = ids.shape[1]
    nts = T // tm
    rows = _sorted_rows(tm)
    max_chunks = rows // CHUNK
    cpt = tm_e // CHUNK
    experts = jnp.arange(N_EXPERTS, dtype=jnp.int32)
    onehot = (ids.reshape(TOP_K, nts, tm)[..., None] == experts).astype(jnp.int32)
    seg_chunks = (jnp.sum(onehot, axis=(0, 2)) + CHUNK - 1) // CHUNK
    local_first = jnp.cumsum(seg_chunks, axis=1) - seg_chunks
    n_chunks = jnp.sum(seg_chunks, axis=1)
    expert_chunks = jnp.sum(seg_chunks, axis=0)
    region = ((expert_chunks + cpt - 1) // cpt) * cpt
    region_end = jnp.cumsum(region)
    seg_first = (region_end - region)[None, :] + jnp.cumsum(seg_chunks, axis=0) - seg_chunks
    ci = jnp.arange(max_chunks, dtype=jnp.int32)
    in_seg = (ci[None, :, None] >= local_first[:, None, :]) & (ci[None, :, None] < (local_first + seg_chunks)[:, None, :])
    gchunk = jnp.sum(in_seg * (seg_first - local_first)[:, None, :], axis=2) + ci[None, :]
    used = ci[None, :] < n_chunks[:, None]
    chunk_dst = jnp.where(used, gchunk * CHUNK, 0).reshape(-1).astype(jnp.int32)
    n_global = nt_e * cpt
    local_row = jnp.arange(nts, dtype=jnp.int32)[:, None] * rows + ci[None, :] * CHUNK
    zero_chunk_row = rows - CHUNK
    chunk_src = jnp.full((n_global,), zero_chunk_row, jnp.int32).at[
        jnp.where(used, gchunk, n_global).reshape(-1)].set(local_row.reshape(-1), mode="drop")
    tile_start = jnp.arange(nt_e, dtype=jnp.int32) * cpt
    tile_expert = jnp.minimum(jnp.sum((tile_start[:, None] >= region_end[None, :]).astype(jnp.int32), axis=1),
                              N_EXPERTS - 1)
    tile_valid = (tile_start < region_end[-1]).astype(jnp.int32)
    return n_chunks.astype(jnp.int32), chunk_dst, chunk_src, tile_expert, tile_valid


def kernel(x, c, norm_mix_g, norm_ffn_g, norm_final_g, w_ada, b_ada, w_in, b_fgate, w_pool, pool_scale, w_out,
           w_router_group, b_router_group, w_router_expert, b_router_expert, w_expert_gate, w_expert_up,
           w_expert_down):
    B, S, D = x.shape
    L = w_ada.shape[0]
    T = B * S
    tm_mix = min(512, S)
    tq = min(256, S)
    tm_e = 256
    chunks_per_tile = tm_e // CHUNK
    max_used = (T // tm_mix) * (_sorted_rows(tm_mix) // CHUNK - 1) + N_EXPERTS * (chunks_per_tile - 1)
    nt_e = -(-max_used // chunks_per_tile)

    mod_all = _ada_modulation(c, w_ada, b_ada).reshape(L, B, N_MOD, D)
    Fe = w_expert_gate.shape[-1]
    sel = _forget_routing()
    for l in range(L):
        mod = mod_all[l]
        w_in_l = w_in[l]
        wqkv = w_in_l[:, :3 * ATTN_WIDTH].astype(BF16)
        wf = jnp.pad(w_in_l[:, 3 * ATTN_WIDTH:3 * ATTN_WIDTH + ATTN_HEADS], ((0, 0), (0, LANES - ATTN_HEADS))).astype(BF16)
        bf = jnp.pad(b_fgate[l].astype(F32), (0, LANES - ATTN_HEADS)).reshape(1, LANES)
        wu = w_in_l[:, 3 * ATTN_WIDTH + ATTN_HEADS:].astype(BF16)
        q, kaug, vt, diff = _premix(x, mod, norm_mix_g[l].reshape(1, D), wqkv, wf, bf, wu, sel, tm=tm_mix)
        attn = _attention(q, kaug, vt, tq=tq)

        wr = jnp.concatenate([
            jnp.pad(w_router_group[l].T, ((0, SUBLANES - N_EXPERT_GROUPS), (0, 0))),
            w_router_expert[l].transpose(0, 2, 1).reshape(N_EXPERTS, D)], axis=0)
        wr_hi = wr.astype(BF16)
        wr_lo = (wr - wr_hi.astype(F32)).astype(BF16)
        br = jnp.concatenate([jnp.pad(b_router_group[l], (0, SUBLANES - N_EXPERT_GROUPS)),
                              b_router_expert[l].reshape(N_EXPERTS)]).reshape(ROUTER_ROWS, 1).astype(F32)
        x1, xs, ids, wts, pos = _postmix(attn, diff, x, mod, w_pool[l].astype(BF16),
                                         pool_scale[l].reshape(1, POOL_WIDTH), w_out[l].astype(BF16),
                                         norm_ffn_g[l].reshape(1, D), wr_hi, wr_lo, br, tm=tm_mix)

        n_chunks, chunk_dst, chunk_src, tile_expert, tile_valid = _dispatch_tables(ids, tm=tm_mix, tm_e=tm_e, nt_e=nt_e)
        o_sorted = _experts(tile_expert, tile_valid, chunk_src, xs,
                            w_expert_gate[l].reshape(N_EXPERTS, D, Fe), w_expert_up[l].reshape(N_EXPERTS, D, Fe),
                            w_expert_down[l].reshape(N_EXPERTS, Fe, D), tm=tm_e)
        g_final = norm_final_g.reshape(1, D) if l == L - 1 else None
        x = _combine(n_chunks, chunk_dst, o_sorted, x1, pos, wts, mod, g_final, tm=tm_mix)
    return x
```

```python
import functools

import jax
import jax.numpy as jnp
import numpy as np
from jax import lax
from jax.experimental import pallas as pl
from jax.experimental.pallas import tpu as pltpu

ATTN_HEADS = 8
HEAD_DIM = 64
ATTN_WIDTH = ATTN_HEADS * HEAD_DIM
POOL_WINDOWS = (2, 4, 8, 16)
POOL_GROUP_DIM = 128
POOL_WIDTH = POOL_GROUP_DIM * len(POOL_WINDOWS)
POOL_HALO = 16
N_EXPERT_GROUPS = 4
EXPERTS_PER_GROUP = 8
N_EXPERTS = N_EXPERT_GROUPS * EXPERTS_PER_GROUP
N_MOD = 6
EPS = 1e-6
NEG_INF = -1e30
LOG2E = 1.4426950408889634

LANES = 128
SUBLANES = 8
AUG = 2 * LANES
DENOM_ROWS = 16
ROUTER_ROWS = 40
CHUNK = 16
TOP_K = 2
VMEM_LIMIT = 48 * 1024 * 1024


def _sorted_rows(tm):
    worst = TOP_K * tm + N_EXPERTS * (CHUNK - 1)
    return (worst // LANES + 1) * LANES

F32 = jnp.float32
BF16 = jnp.bfloat16


def _silu(a):
    return a * jax.nn.sigmoid(a)


def _nt_dot(a, b):
    return lax.dot_general(a, b, (((1,), (1,)), ((), ())), preferred_element_type=F32)


def _split3(a):
    t0 = a.astype(BF16)
    r1 = a - t0.astype(F32)
    t1 = r1.astype(BF16)
    t2 = (r1 - t1.astype(F32)).astype(BF16)
    return t0, t1, t2


def _rms_modulate(x, g, shift, scale):
    ms = jnp.mean(x * x, axis=-1, keepdims=True)
    y = x * lax.rsqrt(ms + EPS) * g
    return y * (1.0 + scale) + shift


def _ada_kernel(c_ref, w_ref, b_ref, o_ref):
    ca = _silu(c_ref[...])
    o_ref[0] = jnp.dot(ca, w_ref[0], precision=lax.Precision.HIGHEST, preferred_element_type=F32) + b_ref[0]


def _ada_modulation(c, w_ada, b_ada):
    L, D, W = w_ada.shape
    B = c.shape[0]
    tn = W // 4
    return pl.pallas_call(
        _ada_kernel,
        grid=(L, W // tn),
        in_specs=[
            pl.BlockSpec((B, D), lambda l, n: (0, 0)),
            pl.BlockSpec((1, D, tn), lambda l, n: (l, 0, n)),
            pl.BlockSpec((1, 1, tn), lambda l, n: (l, 0, n)),
        ],
        out_specs=pl.BlockSpec((1, B, tn), lambda l, n: (l, 0, n)),
        out_shape=jax.ShapeDtypeStruct((L, B, W), F32),
        compiler_params=pltpu.CompilerParams(vmem_limit_bytes=VMEM_LIMIT),
        name="ada_modulation",
    )(c, w_ada, b_ada.reshape(L, 1, W))


def _premix_kernel(x_ref, mod_ref, g_ref, wqkv_ref, wf_ref, bf_ref, wu_ref, sel_ref,
                   q_ref, kaug_ref, vt_ref, diff_ref, carry_ref, ubuf_ref, *, tm):
    si = pl.program_id(1)

    @pl.when(si == 0)
    def _():
        carry_ref[...] = jnp.zeros_like(carry_ref)
        ubuf_ref[0:POOL_HALO, :] = jnp.zeros((POOL_HALO, POOL_WIDTH), F32)

    h = _rms_modulate(x_ref[0], g_ref[...], mod_ref[0, 0:1, :], mod_ref[0, 1:2, :]).astype(BF16)
    qkv = jnp.dot(h, wqkv_ref[...], preferred_element_type=F32)
    q_ref[0] = (qkv[:, :ATTN_WIDTH] * (LOG2E * HEAD_DIM ** -0.5)).astype(BF16)
    vt_ref[0] = qkv[:, 2 * ATTN_WIDTH:].T.astype(BF16)

    fl = jnp.dot(h, wf_ref[...], preferred_element_type=F32) + bf_ref[...]
    lf = jnp.minimum(fl, 0.0) - jnp.log1p(jnp.exp(-jnp.abs(fl)))
    row = lax.broadcasted_iota(jnp.int32, (tm, tm), 0)
    col = lax.broadcasted_iota(jnp.int32, (tm, tm), 1)
    tri = (row >= col).astype(BF16)
    cs = None
    for term in _split3(lf):
        d = jnp.dot(tri, term, preferred_element_type=F32)
        cs = d if cs is None else cs + d
    f_cum = cs + carry_ref[...]
    carry_ref[...] = f_cum[tm - 1:tm, :]

    aug = None
    for i, term in enumerate(_split3(-LOG2E * f_cum)):
        d = jnp.dot(term, sel_ref[i], preferred_element_type=F32)
        aug = d if aug is None else aug + d
    kaug_ref[0] = jnp.concatenate([qkv[:, ATTN_WIDTH:2 * ATTN_WIDTH], aug], axis=1).astype(BF16)

    u = jnp.dot(h, wu_ref[...], preferred_element_type=F32)
    ubuf_ref[POOL_HALO:POOL_HALO + tm, :] = u
    pos = si * tm + lax.broadcasted_iota(jnp.int32, (tm, POOL_GROUP_DIM), 0)
    diffs = []
    for g, w in enumerate(POOL_WINDOWS):
        c0 = g * POOL_GROUP_DIM
        ug = u[:, c0:c0 + POOL_GROUP_DIM]
        acc = ug
        for j in range(1, w):
            acc = acc + ubuf_ref[POOL_HALO - j:POOL_HALO - j + tm, c0:c0 + POOL_GROUP_DIM]
        cnt = jnp.minimum(pos + 1, w).astype(F32)
        diffs.append((acc / cnt - ug).astype(BF16))
    diff_ref[0] = jnp.concatenate(diffs, axis=1)
    ubuf_ref[0:POOL_HALO, :] = u[tm - POOL_HALO:, :]


def _premix(x, mod, g, wqkv, wf, bf, wu, sel, *, tm):
    B, S, D = x.shape
    row_spec = lambda w: pl.BlockSpec((1, tm, w), lambda b, s: (b, s, 0))
    const = lambda a: pl.BlockSpec(a.shape, lambda b, s: (0,) * a.ndim)
    kaug_w = ATTN_WIDTH + LANES
    return pl.pallas_call(
        functools.partial(_premix_kernel, tm=tm),
        grid=(B, S // tm),
        in_specs=[
            row_spec(D),
            pl.BlockSpec((1, N_MOD, D), lambda b, s: (b, 0, 0)),
            const(g), const(wqkv), const(wf), const(bf), const(wu), const(sel),
        ],
        out_specs=[row_spec(ATTN_WIDTH), row_spec(kaug_w),
                   pl.BlockSpec((1, ATTN_WIDTH, tm), lambda b, s: (b, 0, s)), row_spec(POOL_WIDTH)],
        out_shape=[
            jax.ShapeDtypeStruct((B, S, ATTN_WIDTH), BF16),
            jax.ShapeDtypeStruct((B, S, kaug_w), BF16),
            jax.ShapeDtypeStruct((B, ATTN_WIDTH, S), BF16),
            jax.ShapeDtypeStruct((B, S, POOL_WIDTH), BF16),
        ],
        scratch_shapes=[pltpu.VMEM((1, LANES), F32), pltpu.VMEM((POOL_HALO + tm, POOL_WIDTH), F32)],
        compiler_params=pltpu.CompilerParams(
            dimension_semantics=("arbitrary", "arbitrary"), vmem_limit_bytes=VMEM_LIMIT),
        name="premix",
    )(x, mod, g, wqkv, wf, bf, wu, sel)


def _forget_routing():
    sel = np.zeros((3, LANES, LANES), np.float32)
    for i in range(3):
        for h in range(ATTN_HEADS):
            sel[i, h, 3 * h + i] = 1.0
    return jnp.asarray(sel, BF16)


def _attn_kernel(q_ref, kaug_ref, vt_ref, o_ref, qt_ref, m_ref, acc_ref, s_ref, mb_ref, *, tq, n_pairs):
    tk = tq
    qi = pl.program_id(1)
    n_heads = 2 * n_pairs
    r128 = lax.broadcasted_iota(jnp.int32, (LANES, tq), 0)
    for p in range(n_pairs):
        qt = q_ref[0, :, p * LANES:(p + 1) * LANES].astype(F32).T.astype(BF16)
        for hh in range(2):
            top = jnp.where((r128 >= HEAD_DIM * hh) & (r128 < HEAD_DIM * (hh + 1)), qt, jnp.zeros_like(qt))
            h = 2 * p + hh
            bot = ((r128 >= 3 * h) & (r128 < 3 * h + 3)).astype(BF16)
            qt_ref[h] = jnp.concatenate([top, bot], axis=0)
    m_ref[...] = jnp.full(m_ref.shape, NEG_INF, F32)
    acc_ref[...] = jnp.zeros(acc_ref.shape, F32)
    ones = jnp.ones((DENOM_ROWS, tk), BF16)

    def scores(j, slot, masked):
        k0 = pl.multiple_of(j * tk, tk)
        f_terms = kaug_ref[0, pl.ds(k0, tk), n_pairs * LANES:(n_pairs + 1) * LANES]
        for h in range(n_heads):
            kb = jnp.concatenate([kaug_ref[0, pl.ds(k0, tk), (h // 2) * LANES:(h // 2 + 1) * LANES], f_terms], axis=1)
            s = jnp.dot(kb, qt_ref[h], preferred_element_type=F32)
            if masked:
                key = lax.broadcasted_iota(jnp.int32, (tk, tq), 0)
                qry = lax.broadcasted_iota(jnp.int32, (tk, tq), 1)
                s = jnp.where(key <= qry, s, NEG_INF)
            s_ref[slot, h] = s
            mb_ref[slot, h] = jnp.max(s, axis=0, keepdims=True)

    def consume(j, slot):
        k0 = pl.multiple_of(j * tk, tk)
        for h in range(n_heads):
            m_prev = m_ref[h]
            m_new = jnp.maximum(m_prev, mb_ref[slot, h])
            pt = jnp.exp2(s_ref[slot, h] - m_new).astype(BF16)
            alpha = jnp.exp2(m_prev - m_new)
            vtb = vt_ref[0, pl.ds(h * HEAD_DIM, HEAD_DIM), pl.ds(k0, tk)]
            lhs = jnp.concatenate([vtb, ones], axis=0)
            acc_ref[h] = alpha * acc_ref[h] + jnp.dot(lhs, pt, preferred_element_type=F32)
            m_ref[h] = m_new

    @pl.when(qi == 0)
    def _():
        scores(0, 0, True)

    @pl.when(qi > 0)
    def _():
        scores(0, 0, False)

    def body(jj, c):
        j = 2 * jj
        scores(j + 1, 1, False)
        consume(j, 0)
        scores(j + 2, 0, False)
        consume(j + 1, 1)
        return c

    n_double = jnp.maximum(qi - 1, 0) // 2
    lax.fori_loop(0, n_double, body, 0)
    j0 = 2 * n_double
    rem = qi - j0

    @pl.when(rem == 0)
    def _():
        consume(0, 0)

    @pl.when(rem == 1)
    def _():
        scores(qi, 1, True)
        consume(j0, 0)
        consume(qi, 1)

    @pl.when(rem == 2)
    def _():
        scores(j0 + 1, 1, False)
        consume(j0, 0)
        scores(qi, 0, True)
        consume(j0 + 1, 1)
        consume(qi, 0)

    for p in range(n_pairs):
        outs = []
        for hh in range(2):
            a = acc_ref[2 * p + hh]
            outs.append(a[:HEAD_DIM, :] / a[HEAD_DIM:HEAD_DIM + 1, :])
        o_ref[0, :, p * LANES:(p + 1) * LANES] = jnp.concatenate(outs, axis=0).T.astype(BF16)


def _attention(q, kaug, vt, *, tq):
    B, S, W = q.shape
    n_pairs = W // LANES
    return pl.pallas_call(
        functools.partial(_attn_kernel, tq=tq, n_pairs=n_pairs),
        grid=(B, S // tq),
        in_specs=[pl.BlockSpec((1, tq, W), lambda b, i: (b, i, 0)),
                  pl.BlockSpec((1, S, (n_pairs + 1) * LANES), lambda b, i: (b, 0, 0)),
                  pl.BlockSpec((1, W, S), lambda b, i: (b, 0, 0))],
        out_specs=pl.BlockSpec((1, tq, W), lambda b, i: (b, i, 0)),
        out_shape=jax.ShapeDtypeStruct((B, S, W), BF16),
        scratch_shapes=[pltpu.VMEM((2 * n_pairs, AUG, tq), BF16),
                        pltpu.VMEM((2 * n_pairs, 1, tq), F32),
                        pltpu.VMEM((2 * n_pairs, HEAD_DIM + DENOM_ROWS, tq), F32),
                        pltpu.VMEM((2, 2 * n_pairs, tq, tq), F32),
                        pltpu.VMEM((2, 2 * n_pairs, 1, tq), F32)],
        compiler_params=pltpu.CompilerParams(
            dimension_semantics=("arbitrary", "arbitrary"), vmem_limit_bytes=VMEM_LIMIT),
        name="fox_attention",
    )(q, kaug, vt)


def _postmix_kernel(attn_ref, diff_ref, x_ref, mod_ref, wpool_ref, pscale_ref, wout_ref, g_ref,
                    wr_hi_ref, wr_lo_ref, br_ref, before_ref,
                    x1_ref, xs_ref, ids_ref, wts_ref, pos_ref, *, tm):
    pooled = []
    for g in range(len(POOL_WINDOWS)):
        c0 = g * POOL_GROUP_DIM
        pooled.append(jnp.dot(diff_ref[0, :, c0:c0 + POOL_GROUP_DIM], wpool_ref[g], preferred_element_type=F32))
    pool_out = (jnp.concatenate(pooled, axis=1) * pscale_ref[...]).astype(BF16)
    cat = jnp.concatenate([attn_ref[0], pool_out], axis=1)
    mix = jnp.dot(cat, wout_ref[...], preferred_element_type=F32)
    x1 = x_ref[0] + mod_ref[0, 2:3, :] * mix
    x1_ref[0] = x1

    h = _rms_modulate(x1, g_ref[...], mod_ref[0, 3:4, :], mod_ref[0, 4:5, :])

    h_hi = h.astype(BF16)
    h_lo = (h - h_hi.astype(F32)).astype(BF16)
    logits = (_nt_dot(wr_hi_ref[...], h_hi) + _nt_dot(wr_lo_ref[...], h_hi) + _nt_dot(wr_hi_ref[...], h_lo)
              + br_ref[...])
    sub = lax.broadcasted_iota(jnp.int32, (SUBLANES, tm), 0)
    lg = jnp.where(sub < N_EXPERT_GROUPS, logits[0:SUBLANES, :], NEG_INF)
    g_max = jnp.max(lg, axis=0, keepdims=True)
    top_p = 1.0 / jnp.sum(jnp.exp(lg - g_max), axis=0, keepdims=True)
    top_g = jnp.min(jnp.where(lg == g_max, sub, SUBLANES), axis=0, keepdims=True)
    le = logits[SUBLANES:2 * SUBLANES, :]
    for g in range(1, N_EXPERT_GROUPS):
        le = jnp.where(top_g == g, logits[(g + 1) * SUBLANES:(g + 2) * SUBLANES, :], le)
    v1 = jnp.max(le, axis=0, keepdims=True)
    i1 = jnp.min(jnp.where(le == v1, sub, SUBLANES), axis=0, keepdims=True)
    le2 = jnp.where(sub == i1, NEG_INF, le)
    v2 = jnp.max(le2, axis=0, keepdims=True)
    i2 = jnp.min(jnp.where(le2 == v2, sub, SUBLANES), axis=0, keepdims=True)
    e2 = jnp.exp(v2 - v1)
    w1 = top_p / (1.0 + e2)
    id0 = top_g * EXPERTS_PER_GROUP + i1
    id1 = top_g * EXPERTS_PER_GROUP + i2
    ids_ref[...] = jnp.concatenate([id0, id1], axis=0)
    wts_ref[...] = jnp.concatenate([w1, w1 * e2], axis=0)

    sub_e = lax.broadcasted_iota(jnp.int32, (N_EXPERTS, tm), 0)
    onehot = jnp.concatenate([sub_e == id0, sub_e == id1], axis=1)
    oh_f = onehot.astype(F32)
    rank = jnp.dot(onehot.astype(BF16), before_ref[...], preferred_element_type=F32)
    chunks = jnp.floor((jnp.sum(oh_f, axis=1, keepdims=True) + (CHUNK - 1.0)) * (1.0 / CHUNK))
    er = lax.broadcasted_iota(jnp.int32, (N_EXPERTS, N_EXPERTS), 0)
    ec = lax.broadcasted_iota(jnp.int32, (N_EXPERTS, N_EXPERTS), 1)
    first_chunk = jnp.dot((er > ec).astype(BF16), jnp.broadcast_to(chunks, (N_EXPERTS, LANES)).astype(BF16),
                          preferred_element_type=F32)[:, 0:1]
    pos = jnp.sum(oh_f * (rank + CHUNK * first_chunk), axis=0, keepdims=True)
    pos0 = pos[:, :tm]
    pos1 = pos[:, tm:]
    pos_ref[...] = jnp.concatenate([pos0, pos1], axis=0)
    r_iota = lax.broadcasted_iota(jnp.int32, (xs_ref.shape[0], tm), 0)
    perm = ((r_iota == pos0.astype(jnp.int32)) | (r_iota == pos1.astype(jnp.int32))).astype(BF16)
    xs_ref[...] = jnp.dot(perm, h_hi, preferred_element_type=F32).astype(BF16)


def _postmix(attn, diff, x, mod, wpool, pscale, wout, g, wr_hi, wr_lo, br, *, tm):
    B, S, D = x.shape
    T = B * S
    nst = S // tm
    rows = _sorted_rows(tm)
    before = jnp.asarray(np.triu(np.ones((TOP_K * tm, TOP_K * tm), np.float32), k=1), BF16)
    row_spec = lambda w: pl.BlockSpec((1, tm, w), lambda b, s: (b, s, 0))
    const = lambda a: pl.BlockSpec(a.shape, lambda b, s: (0,) * a.ndim)
    tok_spec = pl.BlockSpec((TOP_K, tm), lambda b, s: (0, b * nst + s))
    return pl.pallas_call(
        functools.partial(_postmix_kernel, tm=tm),
        grid=(B, nst),
        in_specs=[row_spec(ATTN_WIDTH), row_spec(POOL_WIDTH), row_spec(D),
                  pl.BlockSpec((1, N_MOD, D), lambda b, s: (b, 0, 0)),
                  const(wpool), const(pscale), const(wout), const(g), const(wr_hi), const(wr_lo), const(br),
                  const(before)],
        out_specs=[row_spec(D),
                   pl.BlockSpec((rows, D), lambda b, s: (b * nst + s, 0)),
                   tok_spec, tok_spec, tok_spec],
        out_shape=[
            jax.ShapeDtypeStruct((B, S, D), F32),
            jax.ShapeDtypeStruct((B * nst * rows, D), BF16),
            jax.ShapeDtypeStruct((TOP_K, T), jnp.int32),
            jax.ShapeDtypeStruct((TOP_K, T), F32),
            jax.ShapeDtypeStruct((TOP_K, T), F32),
        ],
        compiler_params=pltpu.CompilerParams(
            dimension_semantics=("arbitrary", "arbitrary"), vmem_limit_bytes=VMEM_LIMIT),
        name="postmix_router",
    )(attn, diff, x, mod, wpool, pscale, wout, g, wr_hi, wr_lo, br, before)


def _chunk_copy(src_hbm, src_row, dst, dst_row, sem):
    return pltpu.make_async_copy(src_hbm.at[pl.ds(pl.multiple_of(src_row, CHUNK), CHUNK), :],
                                 dst.at[pl.ds(pl.multiple_of(dst_row, CHUNK), CHUNK), :], sem)


def _expert_kernel(te_ref, tv_ref, src_ref, xs_hbm, wg_ref, wu_ref, wd_ref, o_ref, buf, sem, wg_b, wu_b, wd_b,
                   *, tm, nt):
    i = pl.program_id(0)
    n_chunks = tm // CHUNK

    @pl.when(jnp.logical_or(i == 1, te_ref[jnp.maximum(i - 1, 0)] != te_ref[jnp.maximum(i - 2, 0)]))
    def _():
        wg_b[...] = wg_ref[0].astype(BF16)
        wu_b[...] = wu_ref[0].astype(BF16)
        wd_b[...] = wd_ref[0].astype(BF16)

    @pl.when(jnp.logical_and(i < nt, tv_ref[jnp.minimum(i, nt - 1)] == 1))
    def _():
        slot = i % 2
        for c in range(n_chunks):
            _chunk_copy(xs_hbm, src_ref[i * n_chunks + c], buf.at[slot], c * CHUNK, sem.at[slot]).start()

    @pl.when(i > 0)
    def _():
        t = i - 1
        slot = t % 2

        @pl.when(tv_ref[t] == 1)
        def _():
            for c in range(n_chunks):
                _chunk_copy(xs_hbm, 0, buf.at[slot], c * CHUNK, sem.at[slot]).wait()
            half = tm // 2
            gate_up = []
            for r0 in (0, half):
                x = buf[slot, r0:r0 + half, :]
                gate_up.append((jnp.dot(x, wg_b[...], preferred_element_type=F32),
                                jnp.dot(x, wu_b[...], preferred_element_type=F32)))
            for r0, (a, b) in zip((0, half), gate_up):
                act = (_silu(a) * b).astype(BF16)
                o_ref[r0:r0 + half, :] = jnp.dot(act, wd_b[...], preferred_element_type=F32).astype(BF16)

        @pl.when(tv_ref[t] == 0)
        def _():
            o_ref[...] = jnp.zeros(o_ref.shape, BF16)


def _experts(tile_expert, tile_valid, chunk_src, xs, wg, wu, wd, *, tm, layer):
    nt = tile_expert.shape[0]
    _, D, Fe = wg.shape
    prev = lambda i: jnp.maximum(i - 1, 0)
    expert = lambda i, te: layer * N_EXPERTS + te[prev(i)]
    grid_spec = pltpu.PrefetchScalarGridSpec(
        num_scalar_prefetch=3,
        grid=(nt + 1,),
        in_specs=[
            pl.BlockSpec(memory_space=pl.ANY),
            pl.BlockSpec((1, D, Fe), lambda i, te, tv, cs: (expert(i, te), 0, 0)),
            pl.BlockSpec((1, D, Fe), lambda i, te, tv, cs: (expert(i, te), 0, 0)),
            pl.BlockSpec((1, Fe, D), lambda i, te, tv, cs: (expert(i, te), 0, 0)),
        ],
        out_specs=pl.BlockSpec((tm, D), lambda i, te, tv, cs: (prev(i), 0)),
        scratch_shapes=[pltpu.VMEM((2, tm, D), BF16), pltpu.SemaphoreType.DMA((2,)),
                        pltpu.VMEM((D, Fe), BF16), pltpu.VMEM((D, Fe), BF16), pltpu.VMEM((Fe, D), BF16)],
    )
    return pl.pallas_call(
        functools.partial(_expert_kernel, tm=tm, nt=nt),
        grid_spec=grid_spec,
        out_shape=jax.ShapeDtypeStruct((nt * tm, D), BF16),
        compiler_params=pltpu.CompilerParams(dimension_semantics=("arbitrary",), vmem_limit_bytes=VMEM_LIMIT),
        name="moe_experts",
    )(tile_expert, tile_valid, chunk_src, xs, wg, wu, wd)


def _combine_kernel(nch_ref, dst_ref, o_hbm, x1_ref, pos_ref, wts_ref, mod_ref, *rest, tm, nt, max_chunks, final):
    if final:
        gf_ref, out_ref, buf, sem = rest
    else:
        out_ref, buf, sem = rest
    i = pl.program_id(0)

    @pl.when(i == 0)
    def _():
        buf[...] = jnp.zeros(buf.shape, BF16)

    @pl.when(i < nt)
    def _():
        slot = i % 2

        def body(c, carry):
            _chunk_copy(o_hbm, dst_ref[i * max_chunks + c], buf.at[slot], c * CHUNK, sem.at[slot]).start()
            return carry

        lax.fori_loop(0, nch_ref[jnp.minimum(i, nt - 1)], body, 0)

    @pl.when(i > 0)
    def _():
        t = i - 1
        slot = t % 2

        def body(c, carry):
            _chunk_copy(o_hbm, 0, buf.at[slot], c * CHUNK, sem.at[slot]).wait()
            return carry

        lax.fori_loop(0, nch_ref[t], body, 0)
        rows = buf.shape[1]
        rep = lambda r: jnp.broadcast_to(r, (LANES, tm)).T
        p0, p1 = rep(pos_ref[0:1, :]), rep(pos_ref[1:2, :])
        w0, w1 = rep(wts_ref[0:1, :]), rep(wts_ref[1:2, :])
        lane = lax.broadcasted_iota(jnp.int32, (tm, LANES), 1).astype(F32)
        cols = []
        for c in range(rows // LANES):
            r = lane + float(c * LANES)
            cols.append((jnp.where(p0 == r, w0, 0.0) + jnp.where(p1 == r, w1, 0.0)).astype(BF16))
        comb = jnp.concatenate(cols, axis=1)
        y = jnp.dot(comb, buf[slot], preferred_element_type=F32)
        x2 = x1_ref[0] + mod_ref[0, 5:6, :] * y
        if final:
            ms = jnp.mean(x2 * x2, axis=-1, keepdims=True)
            x2 = x2 * lax.rsqrt(ms + EPS) * gf_ref[...]
        out_ref[0] = x2


def _combine(n_chunks, chunk_dst, o_sorted, x1, pos, wts, mod, g_final, *, tm):
    B, S, D = x1.shape
    nst = S // tm
    nt = B * nst
    rows = _sorted_rows(tm)
    max_chunks = chunk_dst.shape[0] // nt
    prev = lambda i: jnp.maximum(i - 1, 0)
    final = g_final is not None
    tok_spec = pl.BlockSpec((TOP_K, tm), lambda i, nc, cd: (0, prev(i)))
    in_specs = [
        pl.BlockSpec(memory_space=pl.ANY),
        pl.BlockSpec((1, tm, D), lambda i, nc, cd: (prev(i) // nst, prev(i) % nst, 0)),
        tok_spec, tok_spec,
        pl.BlockSpec((1, N_MOD, D), lambda i, nc, cd: (prev(i) // nst, 0, 0)),
    ]
    args = [o_sorted, x1, pos, wts, mod]
    if final:
        in_specs.append(pl.BlockSpec((1, D), lambda i, nc, cd: (0, 0)))
        args.append(g_final)
    grid_spec = pltpu.PrefetchScalarGridSpec(
        num_scalar_prefetch=2,
        grid=(nt + 1,),
        in_specs=in_specs,
        out_specs=pl.BlockSpec((1, tm, D), lambda i, nc, cd: (prev(i) // nst, prev(i) % nst, 0)),
        scratch_shapes=[pltpu.VMEM((2, rows, D), BF16), pltpu.SemaphoreType.DMA((2,))],
    )
    return pl.pallas_call(
        functools.partial(_combine_kernel, tm=tm, nt=nt, max_chunks=max_chunks, final=final),
        grid_spec=grid_spec,
        out_shape=jax.ShapeDtypeStruct((B, S, D), F32),
        compiler_params=pltpu.CompilerParams(dimension_semantics=("arbitrary",), vmem_limit_bytes=VMEM_LIMIT),
        name="moe_combine",
    )(n_chunks, chunk_dst, *args)


def _dispatch_tables(ids, *, tm, tm_e, nt_e):
    T = ids.shape[1]
    nts = T // tm
    rows = _sorted_rows(tm)
    max_chunks = rows // CHUNK
    cpt = tm_e // CHUNK
    experts = jnp.arange(N_EXPERTS, dtype=jnp.int32)
    onehot = (ids.reshape(TOP_K, nts, tm)[..., None] == experts).astype(jnp.int32)
    seg_chunks = (jnp.sum(onehot, axis=(0, 2)) + CHUNK - 1) // CHUNK
    local_first = jnp.cumsum(seg_chunks, axis=1) - seg_chunks
    n_chunks = jnp.sum(seg_chunks, axis=1)
    expert_chunks = jnp.sum(seg_chunks, axis=0)
    region = ((expert_chunks + cpt - 1) // cpt) * cpt
    region_end = jnp.cumsum(region)
    seg_first = (region_end - region)[None, :] + jnp.cumsum(seg_chunks, axis=0) - seg_chunks
    ci = jnp.arange(max_chunks, dtype=jnp.int32)
    in_seg = (ci[None, :, None] >= local_first[:, None, :]) & (ci[None, :, None] < (local_first + seg_chunks)[:, None, :])
    gchunk = jnp.sum(in_seg * (seg_first - local_first)[:, None, :], axis=2) + ci[None, :]
    used = ci[None, :] < n_chunks[:, None]
    chunk_dst = jnp.where(used, gchunk * CHUNK, 0).reshape(-1).astype(jnp.int32)
    n_global = nt_e * cpt
    local_row = jnp.arange(nts, dtype=jnp.int32)[:, None] * rows + ci[None, :] * CHUNK
    zero_chunk_row = rows - CHUNK
    chunk_src = jnp.full((n_global,), zero_chunk_row, jnp.int32).at[
        jnp.where(used, gchunk, n_global).reshape(-1)].set(local_row.reshape(-1), mode="drop")
    tile_start = jnp.arange(nt_e, dtype=jnp.int32) * cpt
    tile_expert = jnp.minimum(jnp.sum((tile_start[:, None] >= region_end[None, :]).astype(jnp.int32), axis=1),
                              N_EXPERTS - 1)
    tile_valid = (tile_start < region_end[-1]).astype(jnp.int32)
    return n_chunks.astype(jnp.int32), chunk_dst, chunk_src, tile_expert, tile_valid


def kernel(x, c, norm_mix_g, norm_ffn_g, norm_final_g, w_ada, b_ada, w_in, b_fgate, w_pool, pool_scale, w_out,
           w_router_group, b_router_group, w_router_expert, b_router_expert, w_expert_gate, w_expert_up,
           w_expert_down):
    B, S, D = x.shape
    L = w_ada.shape[0]
    T = B * S
    tm_mix = min(512, S)
    tq = min(256, S)
    tm_e = 512
    chunks_per_tile = tm_e // CHUNK
    max_used = (T // tm_mix) * (_sorted_rows(tm_mix) // CHUNK - 1) + N_EXPERTS * (chunks_per_tile - 1)
    nt_e = -(-max_used // chunks_per_tile)

    mod_all = _ada_modulation(c, w_ada, b_ada).reshape(L, B, N_MOD, D)
    Fe = w_expert_gate.shape[-1]
    wg_all = w_expert_gate.reshape(L * N_EXPERTS, D, Fe)
    wu_all = w_expert_up.reshape(L * N_EXPERTS, D, Fe)
    wd_all = w_expert_down.reshape(L * N_EXPERTS, Fe, D)
    sel = _forget_routing()
    for l in range(L):
        mod = mod_all[l]
        w_in_l = w_in[l]
        wqkv = w_in_l[:, :3 * ATTN_WIDTH].astype(BF16)
        wf = jnp.pad(w_in_l[:, 3 * ATTN_WIDTH:3 * ATTN_WIDTH + ATTN_HEADS], ((0, 0), (0, LANES - ATTN_HEADS))).astype(BF16)
        bf = jnp.pad(b_fgate[l].astype(F32), (0, LANES - ATTN_HEADS)).reshape(1, LANES)
        wu = w_in_l[:, 3 * ATTN_WIDTH + ATTN_HEADS:].astype(BF16)
        q, kaug, vt, diff = _premix(x, mod, norm_mix_g[l].reshape(1, D), wqkv, wf, bf, wu, sel, tm=tm_mix)
        attn = _attention(q, kaug, vt, tq=tq)

        wr = jnp.concatenate([
            jnp.pad(w_router_group[l].T, ((0, SUBLANES - N_EXPERT_GROUPS), (0, 0))),
            w_router_expert[l].transpose(0, 2, 1).reshape(N_EXPERTS, D)], axis=0)
        wr_hi = wr.astype(BF16)
        wr_lo = (wr - wr_hi.astype(F32)).astype(BF16)
        br = jnp.concatenate([jnp.pad(b_router_group[l], (0, SUBLANES - N_EXPERT_GROUPS)),
                              b_router_expert[l].reshape(N_EXPERTS)]).reshape(ROUTER_ROWS, 1).astype(F32)
        x1, xs, ids, wts, pos = _postmix(attn, diff, x, mod, w_pool[l].astype(BF16),
                                         pool_scale[l].reshape(1, POOL_WIDTH), w_out[l].astype(BF16),
                                         norm_ffn_g[l].reshape(1, D), wr_hi, wr_lo, br, tm=tm_mix)

        n_chunks, chunk_dst, chunk_src, tile_expert, tile_valid = _dispatch_tables(ids, tm=tm_mix, tm_e=tm_e, nt_e=nt_e)
        o_sorted = _experts(tile_expert, tile_valid, chunk_src, xs, wg_all, wu_all, wd_all, tm=tm_e, layer=l)
        g_final = norm_final_g.reshape(1, D) if l == L - 1 else None
        x = _combine(n_chunks, chunk_dst, o_sorted, x1, pos, wts, mod, g_final, tm=tm_mix)
    return x
```

```python
import functools

import jax
import jax.numpy as jnp
import numpy as np
from jax import lax
from jax.experimental import pallas as pl
from jax.experimental.pallas import tpu as pltpu

ATTN_HEADS = 8
HEAD_DIM = 64
ATTN_WIDTH = ATTN_HEADS * HEAD_DIM
POOL_WINDOWS = (2, 4, 8, 16)
POOL_GROUP_DIM = 128
POOL_WIDTH = POOL_GROUP_DIM * len(POOL_WINDOWS)
POOL_HALO = 16
N_EXPERT_GROUPS = 4
EXPERTS_PER_GROUP = 8
N_EXPERTS = N_EXPERT_GROUPS * EXPERTS_PER_GROUP
N_MOD = 6
EPS = 1e-6
NEG_INF = -1e30
LOG2E = 1.4426950408889634

LANES = 128
SUBLANES = 8
AUG = 2 * LANES
DENOM_ROWS = 16
ROUTER_ROWS = 40
CHUNK = 16
TOP_K = 2
VMEM_LIMIT = 48 * 1024 * 1024


def _sorted_rows(tm):
    worst = TOP_K * tm + N_EXPERTS * (CHUNK - 1)
    return (worst // LANES + 1) * LANES

F32 = jnp.float32
BF16 = jnp.bfloat16


def _silu(a):
    return a * jax.nn.sigmoid(a)


def _nt_dot(a, b):
    return lax.dot_general(a, b, (((1,), (1,)), ((), ())), preferred_element_type=F32)


def _split3(a):
    t0 = a.astype(BF16)
    r1 = a - t0.astype(F32)
    t1 = r1.astype(BF16)
    t2 = (r1 - t1.astype(F32)).astype(BF16)
    return t0, t1, t2


def _rms_modulate(x, g, shift, scale):
    ms = jnp.mean(x * x, axis=-1, keepdims=True)
    y = x * lax.rsqrt(ms + EPS) * g
    return y * (1.0 + scale) + shift


def _ada_kernel(c_ref, w_ref, b_ref, o_ref):
    ca = _silu(c_ref[...])
    o_ref[0] = jnp.dot(ca, w_ref[0], precision=lax.Precision.HIGHEST, preferred_element_type=F32) + b_ref[0]


def _ada_modulation(c, w_ada, b_ada):
    L, D, W = w_ada.shape
    B = c.shape[0]
    tn = W // 4
    return pl.pallas_call(
        _ada_kernel,
        grid=(L, W // tn),
        in_specs=[
            pl.BlockSpec((B, D), lambda l, n: (0, 0)),
            pl.BlockSpec((1, D, tn), lambda l, n: (l, 0, n)),
            pl.BlockSpec((1, 1, tn), lambda l, n: (l, 0, n)),
        ],
        out_specs=pl.BlockSpec((1, B, tn), lambda l, n: (l, 0, n)),
        out_shape=jax.ShapeDtypeStruct((L, B, W), F32),
        compiler_params=pltpu.CompilerParams(vmem_limit_bytes=VMEM_LIMIT),
        name="ada_modulation",
    )(c, w_ada, b_ada.reshape(L, 1, W))


def _premix_kernel(x_ref, mod_ref, g_ref, wqkv_ref, wf_ref, bf_ref, wu_ref, sel_ref,
                   qt_ref, kaug_ref, vt_ref, diff_ref, carry_ref, ubuf_ref, *, tm):
    si = pl.program_id(1)

    @pl.when(si == 0)
    def _():
        carry_ref[...] = jnp.zeros_like(carry_ref)
        ubuf_ref[0:POOL_HALO, :] = jnp.zeros((POOL_HALO, POOL_WIDTH), F32)

    h = _rms_modulate(x_ref[0], g_ref[...], mod_ref[0, 0:1, :], mod_ref[0, 1:2, :]).astype(BF16)
    u = jnp.dot(h, wu_ref[...], preferred_element_type=F32)
    fl = jnp.dot(h, wf_ref[...], preferred_element_type=F32) + bf_ref[...]
    qkv = jnp.dot(h, wqkv_ref[...], preferred_element_type=F32)

    ubuf_ref[POOL_HALO:POOL_HALO + tm, :] = u
    pos = si * tm + lax.broadcasted_iota(jnp.int32, (tm, POOL_GROUP_DIM), 0)
    diffs = []
    for g, w in enumerate(POOL_WINDOWS):
        c0 = g * POOL_GROUP_DIM
        ug = u[:, c0:c0 + POOL_GROUP_DIM]
        acc = ug
        for j in range(1, w):
            acc = acc + ubuf_ref[POOL_HALO - j:POOL_HALO - j + tm, c0:c0 + POOL_GROUP_DIM]
        cnt = jnp.minimum(pos + 1, w).astype(F32)
        diffs.append((acc / cnt - ug).astype(BF16))
    diff_ref[0] = jnp.concatenate(diffs, axis=1)
    ubuf_ref[0:POOL_HALO, :] = u[tm - POOL_HALO:, :]

    lf = jnp.minimum(fl, 0.0) - jnp.log1p(jnp.exp(-jnp.abs(fl)))
    row = lax.broadcasted_iota(jnp.int32, (tm, tm), 0)
    col = lax.broadcasted_iota(jnp.int32, (tm, tm), 1)
    tri = (row >= col).astype(BF16)
    cs = None
    for term in _split3(lf):
        d = jnp.dot(tri, term, preferred_element_type=F32)
        cs = d if cs is None else cs + d
    f_cum = cs + carry_ref[...]
    carry_ref[...] = f_cum[tm - 1:tm, :]

    aug = None
    for i, term in enumerate(_split3(-LOG2E * f_cum)):
        d = jnp.dot(term, sel_ref[i], preferred_element_type=F32)
        aug = d if aug is None else aug + d
    kaug_ref[0] = jnp.concatenate([qkv[:, ATTN_WIDTH:2 * ATTN_WIDTH], aug], axis=1).astype(BF16)
    qt_ref[0] = (qkv[:, :ATTN_WIDTH] * (LOG2E * HEAD_DIM ** -0.5)).T.astype(BF16)
    vt_ref[0] = qkv[:, 2 * ATTN_WIDTH:].T.astype(BF16)


def _premix(x, mod, g, wqkv, wf, bf, wu, sel, *, tm):
    B, S, D = x.shape
    row_spec = lambda w: pl.BlockSpec((1, tm, w), lambda b, s: (b, s, 0))
    const = lambda a: pl.BlockSpec(a.shape, lambda b, s: (0,) * a.ndim)
    kaug_w = ATTN_WIDTH + LANES
    return pl.pallas_call(
        functools.partial(_premix_kernel, tm=tm),
        grid=(B, S // tm),
        in_specs=[
            row_spec(D),
            pl.BlockSpec((1, N_MOD, D), lambda b, s: (b, 0, 0)),
            const(g), const(wqkv), const(wf), const(bf), const(wu), const(sel),
        ],
        out_specs=[pl.BlockSpec((1, ATTN_WIDTH, tm), lambda b, s: (b, 0, s)), row_spec(kaug_w),
                   pl.BlockSpec((1, ATTN_WIDTH, tm), lambda b, s: (b, 0, s)), row_spec(POOL_WIDTH)],
        out_shape=[
            jax.ShapeDtypeStruct((B, ATTN_WIDTH, S), BF16),
            jax.ShapeDtypeStruct((B, S, kaug_w), BF16),
            jax.ShapeDtypeStruct((B, ATTN_WIDTH, S), BF16),
            jax.ShapeDtypeStruct((B, S, POOL_WIDTH), BF16),
        ],
        scratch_shapes=[pltpu.VMEM((1, LANES), F32), pltpu.VMEM((POOL_HALO + tm, POOL_WIDTH), F32)],
        compiler_params=pltpu.CompilerParams(
            dimension_semantics=("arbitrary", "arbitrary"), vmem_limit_bytes=VMEM_LIMIT),
        name="premix",
    )(x, mod, g, wqkv, wf, bf, wu, sel)


def _forget_routing():
    sel = np.zeros((3, LANES, LANES), np.float32)
    for i in range(3):
        for h in range(ATTN_HEADS):
            sel[i, h, 3 * h + i] = 1.0
    return jnp.asarray(sel, BF16)


def _attn_kernel(qt_in_ref, kaug_ref, vt_ref, o_ref, qt_ref, m_ref, acc_ref, s_ref, mb_ref, *, tq, n_pairs):
    tk = tq
    qi = pl.program_id(1)
    n_heads = 2 * n_pairs
    @pl.when(qi == 0)
    def _():
        r128 = lax.broadcasted_iota(jnp.int32, (LANES, tq), 0)
        for h in range(n_heads):
            qt_ref[h, 0:LANES, :] = jnp.zeros((LANES, tq), BF16)
            qt_ref[h, LANES:AUG, :] = ((r128 >= 3 * h) & (r128 < 3 * h + 3)).astype(BF16)

    for h in range(n_heads):
        r0 = HEAD_DIM * (h % 2)
        qt_ref[h, r0:r0 + HEAD_DIM, :] = qt_in_ref[0, h * HEAD_DIM:(h + 1) * HEAD_DIM, :]
    m_ref[...] = jnp.full(m_ref.shape, NEG_INF, F32)
    acc_ref[...] = jnp.zeros(acc_ref.shape, F32)
    ones = jnp.ones((DENOM_ROWS, tk), BF16)

    def step(new=None, cur=None):
        if new is not None:
            jn, slot_n, masked = new
            k0n = pl.multiple_of(jn * tk, tk)
            f_terms = kaug_ref[0, pl.ds(k0n, tk), n_pairs * LANES:(n_pairs + 1) * LANES]
        if cur is not None:
            jc, slot_c = cur
            k0c = pl.multiple_of(jc * tk, tk)
        for h in range(n_heads):
            if new is not None:
                k_pair = kaug_ref[0, pl.ds(k0n, tk), (h // 2) * LANES:(h // 2 + 1) * LANES]
                s = jnp.dot(jnp.concatenate([k_pair, f_terms], axis=1), qt_ref[h],
                            preferred_element_type=F32)
                if masked:
                    key = lax.broadcasted_iota(jnp.int32, (tk, tq), 0)
                    qry = lax.broadcasted_iota(jnp.int32, (tk, tq), 1)
                    s = jnp.where(key <= qry, s, NEG_INF)
                s_ref[slot_n, h] = s
                mb_ref[slot_n, h] = jnp.max(s, axis=0, keepdims=True)
            if cur is not None:
                m_prev = m_ref[h]
                m_new = jnp.maximum(m_prev, mb_ref[slot_c, h])
                pt = jnp.exp2(s_ref[slot_c, h] - m_new).astype(BF16)
                alpha = jnp.exp2(m_prev - m_new)
                vtb = vt_ref[0, pl.ds(h * HEAD_DIM, HEAD_DIM), pl.ds(k0c, tk)]
                lhs = jnp.concatenate([vtb, ones], axis=0)
                acc_ref[h] = alpha * acc_ref[h] + jnp.dot(lhs, pt, preferred_element_type=F32)
                m_ref[h] = m_new

    @pl.when(qi == 0)
    def _():
        step(new=(0, 0, True))

    @pl.when(qi > 0)
    def _():
        step(new=(0, 0, False))

    def body(jj, c):
        j = 2 * jj
        step(new=(j + 1, 1, False), cur=(j, 0))
        step(new=(j + 2, 0, False), cur=(j + 1, 1))
        return c

    n_double = jnp.maximum(qi - 1, 0) // 2
    lax.fori_loop(0, n_double, body, 0)
    j0 = 2 * n_double
    rem = qi - j0

    @pl.when(rem == 0)
    def _():
        step(cur=(0, 0))

    @pl.when(rem == 1)
    def _():
        step(new=(qi, 1, True), cur=(j0, 0))
        step(cur=(qi, 1))

    @pl.when(rem == 2)
    def _():
        step(new=(j0 + 1, 1, False), cur=(j0, 0))
        step(new=(qi, 0, True), cur=(j0 + 1, 1))
        step(cur=(qi, 0))

    for p in range(n_pairs):
        outs = []
        for hh in range(2):
            a = acc_ref[2 * p + hh]
            outs.append(a[:HEAD_DIM, :] / a[HEAD_DIM:HEAD_DIM + 1, :])
        o_ref[0, :, p * LANES:(p + 1) * LANES] = jnp.concatenate(outs, axis=0).T.astype(BF16)


def _attention(qt, kaug, vt, *, tq):
    B, W, S = qt.shape
    n_pairs = W // LANES
    return pl.pallas_call(
        functools.partial(_attn_kernel, tq=tq, n_pairs=n_pairs),
        grid=(B, S // tq),
        in_specs=[pl.BlockSpec((1, W, tq), lambda b, i: (b, 0, i)),
                  pl.BlockSpec((1, S, (n_pairs + 1) * LANES), lambda b, i: (b, 0, 0)),
                  pl.BlockSpec((1, W, S), lambda b, i: (b, 0, 0))],
        out_specs=pl.BlockSpec((1, tq, W), lambda b, i: (b, i, 0)),
        out_shape=jax.ShapeDtypeStruct((B, S, W), BF16),
        scratch_shapes=[pltpu.VMEM((2 * n_pairs, AUG, tq), BF16),
                        pltpu.VMEM((2 * n_pairs, 1, tq), F32),
                        pltpu.VMEM((2 * n_pairs, HEAD_DIM + DENOM_ROWS, tq), F32),
                        pltpu.VMEM((2, 2 * n_pairs, tq, tq), F32),
                        pltpu.VMEM((2, 2 * n_pairs, 1, tq), F32)],
        compiler_params=pltpu.CompilerParams(
            dimension_semantics=("arbitrary", "arbitrary"), vmem_limit_bytes=VMEM_LIMIT),
        name="fox_attention",
    )(qt, kaug, vt)


def _postmix_kernel(attn_ref, diff_ref, x_ref, mod_ref, wpool_ref, pscale_ref, wout_ref, g_ref,
                    wr_hi_ref, wr_lo_ref, br_ref, before_ref,
                    x1_ref, xs_ref, ids_ref, wts_ref, pos_ref, *, tm):
    pooled = []
    for g in range(len(POOL_WINDOWS)):
        c0 = g * POOL_GROUP_DIM
        pooled.append(jnp.dot(diff_ref[0, :, c0:c0 + POOL_GROUP_DIM], wpool_ref[g], preferred_element_type=F32))
    pool_out = (jnp.concatenate(pooled, axis=1) * pscale_ref[...]).astype(BF16)
    cat = jnp.concatenate([attn_ref[0], pool_out], axis=1)
    mix = jnp.dot(cat, wout_ref[...], preferred_element_type=F32)
    x1 = x_ref[0] + mod_ref[0, 2:3, :] * mix
    x1_ref[0] = x1

    h = _rms_modulate(x1, g_ref[...], mod_ref[0, 3:4, :], mod_ref[0, 4:5, :])

    h_hi = h.astype(BF16)
    h_lo = (h - h_hi.astype(F32)).astype(BF16)
    logits = (_nt_dot(wr_hi_ref[...], h_hi) + _nt_dot(wr_lo_ref[...], h_hi) + _nt_dot(wr_hi_ref[...], h_lo)
              + br_ref[...])
    sub = lax.broadcasted_iota(jnp.int32, (SUBLANES, tm), 0)
    lg = jnp.where(sub < N_EXPERT_GROUPS, logits[0:SUBLANES, :], NEG_INF)
    g_max = jnp.max(lg, axis=0, keepdims=True)
    top_p = 1.0 / jnp.sum(jnp.exp(lg - g_max), axis=0, keepdims=True)
    top_g = jnp.min(jnp.where(lg == g_max, sub, SUBLANES), axis=0, keepdims=True)
    le = logits[SUBLANES:2 * SUBLANES, :]
    for g in range(1, N_EXPERT_GROUPS):
        le = jnp.where(top_g == g, logits[(g + 1) * SUBLANES:(g + 2) * SUBLANES, :], le)
    v1 = jnp.max(le, axis=0, keepdims=True)
    i1 = jnp.min(jnp.where(le == v1, sub, SUBLANES), axis=0, keepdims=True)
    le2 = jnp.where(sub == i1, NEG_INF, le)
    v2 = jnp.max(le2, axis=0, keepdims=True)
    i2 = jnp.min(jnp.where(le2 == v2, sub, SUBLANES), axis=0, keepdims=True)
    e2 = jnp.exp(v2 - v1)
    w1 = top_p / (1.0 + e2)
    id0 = top_g * EXPERTS_PER_GROUP + i1
    id1 = top_g * EXPERTS_PER_GROUP + i2
    ids_ref[...] = jnp.concatenate([id0, id1], axis=0)
    wts_ref[...] = jnp.concatenate([w1, w1 * e2], axis=0)

    sub_e = lax.broadcasted_iota(jnp.int32, (N_EXPERTS, tm), 0)
    onehot = jnp.concatenate([sub_e == id0, sub_e == id1], axis=1)
    oh_f = onehot.astype(F32)
    rank = jnp.dot(onehot.astype(BF16), before_ref[...], preferred_element_type=F32)
    chunks = jnp.floor((jnp.sum(oh_f, axis=1, keepdims=True) + (CHUNK - 1.0)) * (1.0 / CHUNK))
    er = lax.broadcasted_iota(jnp.int32, (N_EXPERTS, N_EXPERTS), 0)
    ec = lax.broadcasted_iota(jnp.int32, (N_EXPERTS, N_EXPERTS), 1)
    first_chunk = jnp.dot((er > ec).astype(BF16), jnp.broadcast_to(chunks, (N_EXPERTS, LANES)).astype(BF16),
                          preferred_element_type=F32)[:, 0:1]
    pos = jnp.sum(oh_f * (rank + CHUNK * first_chunk), axis=0, keepdims=True)
    pos0 = pos[:, :tm]
    pos1 = pos[:, tm:]
    pos_ref[...] = jnp.concatenate([pos0, pos1], axis=0)
    r_iota = lax.broadcasted_iota(jnp.int32, (xs_ref.shape[0], tm), 0)
    perm = ((r_iota == pos0.astype(jnp.int32)) | (r_iota == pos1.astype(jnp.int32))).astype(BF16)
    xs_ref[...] = jnp.dot(perm, h_hi, preferred_element_type=F32).astype(BF16)


def _postmix(attn, diff, x, mod, wpool, pscale, wout, g, wr_hi, wr_lo, br, *, tm):
    B, S, D = x.shape
    T = B * S
    nst = S // tm
    rows = _sorted_rows(tm)
    before = jnp.asarray(np.triu(np.ones((TOP_K * tm, TOP_K * tm), np.float32), k=1), BF16)
    row_spec = lambda w: pl.BlockSpec((1, tm, w), lambda b, s: (b, s, 0))
    const = lambda a: pl.BlockSpec(a.shape, lambda b, s: (0,) * a.ndim)
    tok_spec = pl.BlockSpec((TOP_K, tm), lambda b, s: (0, b * nst + s))
    return pl.pallas_call(
        functools.partial(_postmix_kernel, tm=tm),
        grid=(B, nst),
        in_specs=[row_spec(ATTN_WIDTH), row_spec(POOL_WIDTH), row_spec(D),
                  pl.BlockSpec((1, N_MOD, D), lambda b, s: (b, 0, 0)),
                  const(wpool), const(pscale), const(wout), const(g), const(wr_hi), const(wr_lo), const(br),
                  const(before)],
        out_specs=[row_spec(D),
                   pl.BlockSpec((rows, D), lambda b, s: (b * nst + s, 0)),
                   tok_spec, tok_spec, tok_spec],
        out_shape=[
            jax.ShapeDtypeStruct((B, S, D), F32),
            jax.ShapeDtypeStruct((B * nst * rows, D), BF16),
            jax.ShapeDtypeStruct((TOP_K, T), jnp.int32),
            jax.ShapeDtypeStruct((TOP_K, T), F32),
            jax.ShapeDtypeStruct((TOP_K, T), F32),
        ],
        compiler_params=pltpu.CompilerParams(
            dimension_semantics=("arbitrary", "arbitrary"), vmem_limit_bytes=VMEM_LIMIT),
        name="postmix_router",
    )(attn, diff, x, mod, wpool, pscale, wout, g, wr_hi, wr_lo, br, before)


def _chunk_copy(src_hbm, src_row, dst, dst_row, sem):
    return pltpu.make_async_copy(src_hbm.at[pl.ds(pl.multiple_of(src_row, CHUNK), CHUNK), :],
                                 dst.at[pl.ds(pl.multiple_of(dst_row, CHUNK), CHUNK), :], sem)


def _gather_chunks(table_ref, tile, n_chunks, src_hbm, dst, sem):
    for c in range(n_chunks):
        _chunk_copy(src_hbm, table_ref[tile * n_chunks + c], dst, c * CHUNK, sem).start()


def _gather_wait(src_hbm, dst, sem):
    pltpu.make_async_copy(src_hbm.at[pl.ds(0, dst.shape[0]), :], dst, sem).wait()


def _expert_kernel(te_ref, tv_ref, src_ref, xs_hbm, wg_ref, wu_ref, wd_ref, o_ref, buf, sem, wg_b, wu_b, wd_b,
                   *, tm, nt):
    i = pl.program_id(0)
    n_chunks = tm // CHUNK
    slot = i % 2
    nxt = jnp.minimum(i + 1, nt - 1)

    @pl.when(i == 0)
    def _():
        _gather_chunks(src_ref, 0, n_chunks, xs_hbm, buf.at[0], sem.at[0])

    @pl.when(jnp.logical_or(i == 0, te_ref[i] != te_ref[jnp.maximum(i - 1, 0)]))
    def _():
        wg_b[...] = wg_ref[0].astype(BF16)
        wu_b[...] = wu_ref[0].astype(BF16)
        wd_b[...] = wd_ref[0].astype(BF16)

    @pl.when(tv_ref[i] == 1)
    def _():
        _gather_wait(xs_hbm, buf.at[slot], sem.at[slot])
        half = tm // 2
        gate_up = []
        for r0 in (0, half):
            x = buf[slot, r0:r0 + half, :]
            gate_up.append((jnp.dot(x, wg_b[...], preferred_element_type=F32),
                            jnp.dot(x, wu_b[...], preferred_element_type=F32)))
        _gather_chunks(src_ref, nxt, n_chunks, xs_hbm, buf.at[1 - slot], sem.at[1 - slot])
        for r0, (a, b) in zip((0, half), gate_up):
            act = (_silu(a) * b).astype(BF16)
            o_ref[r0:r0 + half, :] = jnp.dot(act, wd_b[...], preferred_element_type=F32).astype(BF16)

    @pl.when(tv_ref[i] == 0)
    def _():
        _gather_chunks(src_ref, nxt, n_chunks, xs_hbm, buf.at[1 - slot], sem.at[1 - slot])
        _gather_wait(xs_hbm, buf.at[slot], sem.at[slot])
        o_ref[...] = jnp.zeros(o_ref.shape, BF16)

    @pl.when(i == nt - 1)
    def _():
        _gather_wait(xs_hbm, buf.at[1 - slot], sem.at[1 - slot])


def _experts(tile_expert, tile_valid, chunk_src, xs, wg, wu, wd, *, tm, layer):
    nt = tile_expert.shape[0]
    _, D, Fe = wg.shape
    expert = lambda i, te: layer * N_EXPERTS + te[i]
    grid_spec = pltpu.PrefetchScalarGridSpec(
        num_scalar_prefetch=3,
        grid=(nt,),
        in_specs=[
            pl.BlockSpec(memory_space=pl.ANY),
            pl.BlockSpec((1, D, Fe), lambda i, te, tv, cs: (expert(i, te), 0, 0)),
            pl.BlockSpec((1, D, Fe), lambda i, te, tv, cs: (expert(i, te), 0, 0)),
            pl.BlockSpec((1, Fe, D), lambda i, te, tv, cs: (expert(i, te), 0, 0)),
        ],
        out_specs=pl.BlockSpec((tm, D), lambda i, te, tv, cs: (i, 0)),
        scratch_shapes=[pltpu.VMEM((2, tm, D), BF16), pltpu.SemaphoreType.DMA((2,)),
                        pltpu.VMEM((D, Fe), BF16), pltpu.VMEM((D, Fe), BF16), pltpu.VMEM((Fe, D), BF16)],
    )
    return pl.pallas_call(
        functools.partial(_expert_kernel, tm=tm, nt=nt),
        grid_spec=grid_spec,
        out_shape=jax.ShapeDtypeStruct((nt * tm, D), BF16),
        compiler_params=pltpu.CompilerParams(dimension_semantics=("arbitrary",), vmem_limit_bytes=VMEM_LIMIT),
        name="moe_experts",
    )(tile_expert, tile_valid, chunk_src, xs, wg, wu, wd)


def _combine_kernel(dst_ref, o_hbm, x1_ref, pos_ref, wts_ref, mod_ref, *rest, tm, nt, final):
    if final:
        gf_ref, out_ref, buf, sem = rest
    else:
        out_ref, buf, sem = rest
    i = pl.program_id(0)
    rows = buf.shape[1]
    max_chunks = rows // CHUNK
    slot = i % 2

    @pl.when(i == 0)
    def _():
        _gather_chunks(dst_ref, 0, max_chunks, o_hbm, buf.at[0], sem.at[0])

    _gather_wait(o_hbm, buf.at[slot], sem.at[slot])
    rep = lambda r: jnp.broadcast_to(r, (LANES, tm)).T
    p0, p1 = rep(pos_ref[0:1, :]), rep(pos_ref[1:2, :])
    w0, w1 = rep(wts_ref[0:1, :]), rep(wts_ref[1:2, :])
    lane = lax.broadcasted_iota(jnp.int32, (tm, LANES), 1).astype(F32)
    cols = []
    for c in range(rows // LANES):
        r = lane + float(c * LANES)
        cols.append((jnp.where(p0 == r, w0, 0.0) + jnp.where(p1 == r, w1, 0.0)).astype(BF16))
    comb = jnp.concatenate(cols, axis=1)
    y = jnp.dot(comb, buf[slot], preferred_element_type=F32)
    _gather_chunks(dst_ref, jnp.minimum(i + 1, nt - 1), max_chunks, o_hbm, buf.at[1 - slot], sem.at[1 - slot])
    x2 = x1_ref[0] + mod_ref[0, 5:6, :] * y
    if final:
        ms = jnp.mean(x2 * x2, axis=-1, keepdims=True)
        x2 = x2 * lax.rsqrt(ms + EPS) * gf_ref[...]
    out_ref[0] = x2

    @pl.when(i == nt - 1)
    def _():
        _gather_wait(o_hbm, buf.at[1 - slot], sem.at[1 - slot])


def _combine(chunk_dst, o_sorted, x1, pos, wts, mod, g_final, *, tm):
    B, S, D = x1.shape
    nst = S // tm
    nt = B * nst
    rows = _sorted_rows(tm)
    final = g_final is not None
    tok_spec = pl.BlockSpec((TOP_K, tm), lambda i, cd: (0, i))
    in_specs = [
        pl.BlockSpec(memory_space=pl.ANY),
        pl.BlockSpec((1, tm, D), lambda i, cd: (i // nst, i % nst, 0)),
        tok_spec, tok_spec,
        pl.BlockSpec((1, N_MOD, D), lambda i, cd: (i // nst, 0, 0)),
    ]
    args = [o_sorted, x1, pos, wts, mod]
    if final:
        in_specs.append(pl.BlockSpec((1, D), lambda i, cd: (0, 0)))
        args.append(g_final)
    grid_spec = pltpu.PrefetchScalarGridSpec(
        num_scalar_prefetch=1,
        grid=(nt,),
        in_specs=in_specs,
        out_specs=pl.BlockSpec((1, tm, D), lambda i, cd: (i // nst, i % nst, 0)),
        scratch_shapes=[pltpu.VMEM((2, rows, D), BF16), pltpu.SemaphoreType.DMA((2,))],
    )
    return pl.pallas_call(
        functools.partial(_combine_kernel, tm=tm, nt=nt, final=final),
        grid_spec=grid_spec,
        out_shape=jax.ShapeDtypeStruct((B, S, D), F32),
        compiler_params=pltpu.CompilerParams(dimension_semantics=("arbitrary",), vmem_limit_bytes=VMEM_LIMIT),
        name="moe_combine",
    )(chunk_dst, *args)


def _dispatch_tables(ids, *, tm, tm_e, nt_e):
    T = ids.shape[1]
    nts = T // tm
    rows = _sorted_rows(tm)
    max_chunks = rows // CHUNK
    cpt = tm_e // CHUNK
    experts = jnp.arange(N_EXPERTS, dtype=jnp.int32)
    onehot = (ids.reshape(TOP_K, nts, tm)[..., None] == experts).astype(jnp.int32)
    seg_chunks = (jnp.sum(onehot, axis=(0, 2)) + CHUNK - 1) // CHUNK
    local_first = jnp.cumsum(seg_chunks, axis=1) - seg_chunks
    n_chunks = jnp.sum(seg_chunks, axis=1)
    expert_chunks = jnp.sum(seg_chunks, axis=0)
    region = ((expert_chunks + cpt - 1) // cpt) * cpt
    region_end = jnp.cumsum(region)
    seg_first = (region_end - region)[None, :] + jnp.cumsum(seg_chunks, axis=0) - seg_chunks
    ci = jnp.arange(max_chunks, dtype=jnp.int32)
    in_seg = (ci[None, :, None] >= local_first[:, None, :]) & (ci[None, :, None] < (local_first + seg_chunks)[:, None, :])
    gchunk = jnp.sum(in_seg * (seg_first - local_first)[:, None, :], axis=2) + ci[None, :]
    used = ci[None, :] < n_chunks[:, None]
    chunk_dst = jnp.where(used, gchunk * CHUNK, 0).reshape(-1).astype(jnp.int32)
    n_global = nt_e * cpt
    local_row = jnp.arange(nts, dtype=jnp.int32)[:, None] * rows + ci[None, :] * CHUNK
    zero_chunk_row = rows - CHUNK
    chunk_src = jnp.full((n_global,), zero_chunk_row, jnp.int32).at[
        jnp.where(used, gchunk, n_global).reshape(-1)].set(local_row.reshape(-1), mode="drop")
    tile_start = jnp.arange(nt_e, dtype=jnp.int32) * cpt
    tile_expert = jnp.minimum(jnp.sum((tile_start[:, None] >= region_end[None, :]).astype(jnp.int32), axis=1),
                              N_EXPERTS - 1)
    tile_valid = (tile_start < region_end[-1]).astype(jnp.int32)
    return chunk_dst, chunk_src, tile_expert, tile_valid


def kernel(x, c, norm_mix_g, norm_ffn_g, norm_final_g, w_ada, b_ada, w_in, b_fgate, w_pool, pool_scale, w_out,
           w_router_group, b_router_group, w_router_expert, b_router_expert, w_expert_gate, w_expert_up,
           w_expert_down):
    B, S, D = x.shape
    L = w_ada.shape[0]
    T = B * S
    tm_mix = min(512, S)
    tq = min(256, S)
    tm_e = 512
    chunks_per_tile = tm_e // CHUNK
    max_used = (T // tm_mix) * (_sorted_rows(tm_mix) // CHUNK - 1) + N_EXPERTS * (chunks_per_tile - 1)
    nt_e = -(-max_used // chunks_per_tile)

    mod_all = _ada_modulation(c, w_ada, b_ada).reshape(L, B, N_MOD, D)
    Fe = w_expert_gate.shape[-1]
    wg_all = w_expert_gate.reshape(L * N_EXPERTS, D, Fe)
    wu_all = w_expert_up.reshape(L * N_EXPERTS, D, Fe)
    wd_all = w_expert_down.reshape(L * N_EXPERTS, Fe, D)
    sel = _forget_routing()
    for l in range(L):
        mod = mod_all[l]
        w_in_l = w_in[l]
        wqkv = w_in_l[:, :3 * ATTN_WIDTH].astype(BF16)
        wf = jnp.pad(w_in_l[:, 3 * ATTN_WIDTH:3 * ATTN_WIDTH + ATTN_HEADS], ((0, 0), (0, LANES - ATTN_HEADS))).astype(BF16)
        bf = jnp.pad(b_fgate[l].astype(F32), (0, LANES - ATTN_HEADS)).reshape(1, LANES)
        wu = w_in_l[:, 3 * ATTN_WIDTH + ATTN_HEADS:].astype(BF16)
        qt, kaug, vt, diff = _premix(x, mod, norm_mix_g[l].reshape(1, D), wqkv, wf, bf, wu, sel, tm=tm_mix)
        attn = _attention(qt, kaug, vt, tq=tq)

        wr = jnp.concatenate([
            jnp.pad(w_router_group[l].T, ((0, SUBLANES - N_EXPERT_GROUPS), (0, 0))),
            w_router_expert[l].transpose(0, 2, 1).reshape(N_EXPERTS, D)], axis=0)
        wr_hi = wr.astype(BF16)
        wr_lo = (wr - wr_hi.astype(F32)).astype(BF16)
        br = jnp.concatenate([jnp.pad(b_router_group[l], (0, SUBLANES - N_EXPERT_GROUPS)),
                              b_router_expert[l].reshape(N_EXPERTS)]).reshape(ROUTER_ROWS, 1).astype(F32)
        x1, xs, ids, wts, pos = _postmix(attn, diff, x, mod, w_pool[l].astype(BF16),
                                         pool_scale[l].reshape(1, POOL_WIDTH), w_out[l].astype(BF16),
                                         norm_ffn_g[l].reshape(1, D), wr_hi, wr_lo, br, tm=tm_mix)

        chunk_dst, chunk_src, tile_expert, tile_valid = _dispatch_tables(ids, tm=tm_mix, tm_e=tm_e, nt_e=nt_e)
        o_sorted = _experts(tile_expert, tile_valid, chunk_src, xs, wg_all, wu_all, wd_all, tm=tm_e, layer=l)
        g_final = norm_final_g.reshape(1, D) if l == L - 1 else None
        x = _combine(chunk_dst, o_sorted, x1, pos, wts, mod, g_final, tm=tm_mix)
    return x
```

```python
import functools

import jax
import jax.numpy as jnp
import numpy as np
from jax import lax
from jax.experimental import pallas as pl
from jax.experimental.pallas import tpu as pltpu

ATTN_HEADS = 8
HEAD_DIM = 64
ATTN_WIDTH = ATTN_HEADS * HEAD_DIM
POOL_WINDOWS = (2, 4, 8, 16)
POOL_GROUP_DIM = 128
POOL_WIDTH = POOL_GROUP_DIM * len(POOL_WINDOWS)
POOL_HALO = 16
N_EXPERT_GROUPS = 4
EXPERTS_PER_GROUP = 8
N_EXPERTS = N_EXPERT_GROUPS * EXPERTS_PER_GROUP
N_MOD = 6
EPS = 1e-6
NEG_INF = -1e30
LOG2E = 1.4426950408889634

LANES = 128
SUBLANES = 8
AUG = 2 * LANES
DENOM_ROWS = 16
ROUTER_ROWS = 40
CHUNK = 16
TOP_K = 2
VMEM_LIMIT = 48 * 1024 * 1024


def _sorted_rows(tm):
    worst = TOP_K * tm + N_EXPERTS * (CHUNK - 1)
    return (worst // LANES + 1) * LANES

F32 = jnp.float32
BF16 = jnp.bfloat16


def _silu(a):
    return a * jax.nn.sigmoid(a)


def _nt_dot(a, b):
    return lax.dot_general(a, b, (((1,), (1,)), ((), ())), preferred_element_type=F32)


def _split3(a):
    t0 = a.astype(BF16)
    r1 = a - t0.astype(F32)
    t1 = r1.astype(BF16)
    t2 = (r1 - t1.astype(F32)).astype(BF16)
    return t0, t1, t2


def _rms_modulate(x, g, shift, scale):
    ms = jnp.mean(x * x, axis=-1, keepdims=True)
    y = x * lax.rsqrt(ms + EPS) * g
    return y * (1.0 + scale) + shift


def _ada_kernel(c_ref, w_ref, b_ref, o_ref):
    ca = _silu(c_ref[...])
    o_ref[0] = jnp.dot(ca, w_ref[0], precision=lax.Precision.HIGHEST, preferred_element_type=F32) + b_ref[0]


def _ada_modulation(c, w_ada, b_ada):
    L, D, W = w_ada.shape
    B = c.shape[0]
    tn = W // 4
    return pl.pallas_call(
        _ada_kernel,
        grid=(L, W // tn),
        in_specs=[
            pl.BlockSpec((B, D), lambda l, n: (0, 0)),
            pl.BlockSpec((1, D, tn), lambda l, n: (l, 0, n)),
            pl.BlockSpec((1, 1, tn), lambda l, n: (l, 0, n)),
        ],
        out_specs=pl.BlockSpec((1, B, tn), lambda l, n: (l, 0, n)),
        out_shape=jax.ShapeDtypeStruct((L, B, W), F32),
        compiler_params=pltpu.CompilerParams(vmem_limit_bytes=VMEM_LIMIT),
        name="ada_modulation",
    )(c, w_ada, b_ada.reshape(L, 1, W))


def _premix_kernel(x_ref, mod_ref, g_ref, wqkv_ref, wf_ref, bf_ref, wu_ref, sel_ref,
                   qt_ref, kaug_ref, vt_ref, diff_ref, carry_ref, ubuf_ref, *, tm):
    si = pl.program_id(1)

    @pl.when(si == 0)
    def _():
        carry_ref[...] = jnp.zeros_like(carry_ref)
        ubuf_ref[0:POOL_HALO, :] = jnp.zeros((POOL_HALO, POOL_WIDTH), F32)

    h = _rms_modulate(x_ref[0], g_ref[...], mod_ref[0, 0:1, :], mod_ref[0, 1:2, :]).astype(BF16)
    u = jnp.dot(h, wu_ref[...], preferred_element_type=F32)
    fl = jnp.dot(h, wf_ref[...], preferred_element_type=F32) + bf_ref[...]
    qkv = jnp.dot(h, wqkv_ref[...], preferred_element_type=F32)

    ubuf_ref[POOL_HALO:POOL_HALO + tm, :] = u
    pos = si * tm + lax.broadcasted_iota(jnp.int32, (tm, POOL_GROUP_DIM), 0)
    diffs = []
    for g, w in enumerate(POOL_WINDOWS):
        c0 = g * POOL_GROUP_DIM
        ug = u[:, c0:c0 + POOL_GROUP_DIM]
        acc = ug
        for j in range(1, w):
            acc = acc + ubuf_ref[POOL_HALO - j:POOL_HALO - j + tm, c0:c0 + POOL_GROUP_DIM]
        cnt = jnp.minimum(pos + 1, w).astype(F32)
        diffs.append((acc / cnt - ug).astype(BF16))
    diff_ref[0] = jnp.concatenate(diffs, axis=1)
    ubuf_ref[0:POOL_HALO, :] = u[tm - POOL_HALO:, :]

    lf = jnp.minimum(fl, 0.0) - jnp.log1p(jnp.exp(-jnp.abs(fl)))
    row = lax.broadcasted_iota(jnp.int32, (tm, tm), 0)
    col = lax.broadcasted_iota(jnp.int32, (tm, tm), 1)
    tri = (row >= col).astype(BF16)
    cs = None
    for term in _split3(lf):
        d = jnp.dot(tri, term, preferred_element_type=F32)
        cs = d if cs is None else cs + d
    f_cum = cs + carry_ref[...]
    carry_ref[...] = f_cum[tm - 1:tm, :]

    aug = None
    for i, term in enumerate(_split3(-LOG2E * f_cum)):
        d = jnp.dot(term, sel_ref[i], preferred_element_type=F32)
        aug = d if aug is None else aug + d
    kaug_ref[0] = jnp.concatenate([qkv[:, ATTN_WIDTH:2 * ATTN_WIDTH], aug], axis=1).astype(BF16)
    qt_ref[0] = (qkv[:, :ATTN_WIDTH] * (LOG2E * HEAD_DIM ** -0.5)).T.astype(BF16)
    vt_ref[0] = qkv[:, 2 * ATTN_WIDTH:].T.astype(BF16)


def _premix(x, mod, g, wqkv, wf, bf, wu, sel, *, tm):
    B, S, D = x.shape
    row_spec = lambda w: pl.BlockSpec((1, tm, w), lambda b, s: (b, s, 0))
    const = lambda a: pl.BlockSpec(a.shape, lambda b, s: (0,) * a.ndim)
    kaug_w = ATTN_WIDTH + LANES
    return pl.pallas_call(
        functools.partial(_premix_kernel, tm=tm),
        grid=(B, S // tm),
        in_specs=[
            row_spec(D),
            pl.BlockSpec((1, N_MOD, D), lambda b, s: (b, 0, 0)),
            const(g), const(wqkv), const(wf), const(bf), const(wu), const(sel),
        ],
        out_specs=[pl.BlockSpec((1, ATTN_WIDTH, tm), lambda b, s: (b, 0, s)), row_spec(kaug_w),
                   pl.BlockSpec((1, ATTN_WIDTH, tm), lambda b, s: (b, 0, s)), row_spec(POOL_WIDTH)],
        out_shape=[
            jax.ShapeDtypeStruct((B, ATTN_WIDTH, S), BF16),
            jax.ShapeDtypeStruct((B, S, kaug_w), BF16),
            jax.ShapeDtypeStruct((B, ATTN_WIDTH, S), BF16),
            jax.ShapeDtypeStruct((B, S, POOL_WIDTH), BF16),
        ],
        scratch_shapes=[pltpu.VMEM((1, LANES), F32), pltpu.VMEM((POOL_HALO + tm, POOL_WIDTH), F32)],
        compiler_params=pltpu.CompilerParams(
            dimension_semantics=("arbitrary", "arbitrary"), vmem_limit_bytes=VMEM_LIMIT),
        name="premix",
    )(x, mod, g, wqkv, wf, bf, wu, sel)


def _forget_routing():
    sel = np.zeros((3, LANES, LANES), np.float32)
    for i in range(3):
        for h in range(ATTN_HEADS):
            sel[i, h, 3 * h + i] = 1.0
    return jnp.asarray(sel, BF16)


def _attn_kernel(qt_in_ref, kaug_ref, vt_ref, o_ref, qt_ref, m_ref, acc_ref, s_ref, mb_ref, *, tq, n_pairs):
    tk = tq
    qi = pl.program_id(1)
    n_heads = 2 * n_pairs
    @pl.when(qi == 0)
    def _():
        r128 = lax.broadcasted_iota(jnp.int32, (LANES, tq), 0)
        for h in range(n_heads):
            qt_ref[h, 0:LANES, :] = jnp.zeros((LANES, tq), BF16)
            qt_ref[h, LANES:AUG, :] = ((r128 >= 3 * h) & (r128 < 3 * h + 3)).astype(BF16)

    for h in range(n_heads):
        r0 = HEAD_DIM * (h % 2)
        qt_ref[h, r0:r0 + HEAD_DIM, :] = qt_in_ref[0, h * HEAD_DIM:(h + 1) * HEAD_DIM, :]
    m_ref[...] = jnp.full(m_ref.shape, NEG_INF, F32)
    acc_ref[...] = jnp.zeros(acc_ref.shape, F32)
    ones = jnp.ones((DENOM_ROWS, tk), BF16)

    def step(new=None, cur=None):
        if new is not None:
            jn, slot_n, masked = new
            k0n = pl.multiple_of(jn * tk, tk)
            f_terms = kaug_ref[0, pl.ds(k0n, tk), n_pairs * LANES:(n_pairs + 1) * LANES]
        if cur is not None:
            jc, slot_c = cur
            k0c = pl.multiple_of(jc * tk, tk)
        for h in range(n_heads):
            if new is not None:
                k_pair = kaug_ref[0, pl.ds(k0n, tk), (h // 2) * LANES:(h // 2 + 1) * LANES]
                s = jnp.dot(jnp.concatenate([k_pair, f_terms], axis=1), qt_ref[h],
                            preferred_element_type=F32)
                if masked:
                    key = lax.broadcasted_iota(jnp.int32, (tk, tq), 0)
                    qry = lax.broadcasted_iota(jnp.int32, (tk, tq), 1)
                    s = jnp.where(key <= qry, s, NEG_INF)
                s_ref[slot_n, h] = s
                mb_ref[slot_n, h] = jnp.max(s, axis=0, keepdims=True)
            if cur is not None:
                m_prev = m_ref[h]
                m_new = jnp.maximum(m_prev, mb_ref[slot_c, h])
                pt = jnp.exp2(s_ref[slot_c, h] - m_new).astype(BF16)
                alpha = jnp.exp2(m_prev - m_new)
                vtb = vt_ref[0, pl.ds(h * HEAD_DIM, HEAD_DIM), pl.ds(k0c, tk)]
                lhs = jnp.concatenate([vtb, ones], axis=0)
                acc_ref[h] = alpha * acc_ref[h] + jnp.dot(lhs, pt, preferred_element_type=F32)
                m_ref[h] = m_new

    @pl.when(qi == 0)
    def _():
        step(new=(0, 0, True))

    @pl.when(qi > 0)
    def _():
        step(new=(0, 0, False))

    def body(jj, c):
        j = 2 * jj
        step(new=(j + 1, 1, False), cur=(j, 0))
        step(new=(j + 2, 0, False), cur=(j + 1, 1))
        return c

    n_double = jnp.maximum(qi - 1, 0) // 2
    lax.fori_loop(0, n_double, body, 0)
    j0 = 2 * n_double
    rem = qi - j0

    @pl.when(rem == 0)
    def _():
        step(cur=(0, 0))

    @pl.when(rem == 1)
    def _():
        step(new=(qi, 1, True), cur=(j0, 0))
        step(cur=(qi, 1))

    @pl.when(rem == 2)
    def _():
        step(new=(j0 + 1, 1, False), cur=(j0, 0))
        step(new=(qi, 0, True), cur=(j0 + 1, 1))
        step(cur=(qi, 0))

    for p in range(n_pairs):
        outs = []
        for hh in range(2):
            a = acc_ref[2 * p + hh]
            outs.append(a[:HEAD_DIM, :] / a[HEAD_DIM:HEAD_DIM + 1, :])
        o_ref[0, :, p * LANES:(p + 1) * LANES] = jnp.concatenate(outs, axis=0).T.astype(BF16)


def _attention(qt, kaug, vt, *, tq):
    B, W, S = qt.shape
    n_pairs = W // LANES
    return pl.pallas_call(
        functools.partial(_attn_kernel, tq=tq, n_pairs=n_pairs),
        grid=(B, S // tq),
        in_specs=[pl.BlockSpec((1, W, tq), lambda b, i: (b, 0, i)),
                  pl.BlockSpec((1, S, (n_pairs + 1) * LANES), lambda b, i: (b, 0, 0)),
                  pl.BlockSpec((1, W, S), lambda b, i: (b, 0, 0))],
        out_specs=pl.BlockSpec((1, tq, W), lambda b, i: (b, i, 0)),
        out_shape=jax.ShapeDtypeStruct((B, S, W), BF16),
        scratch_shapes=[pltpu.VMEM((2 * n_pairs, AUG, tq), BF16),
                        pltpu.VMEM((2 * n_pairs, 1, tq), F32),
                        pltpu.VMEM((2 * n_pairs, HEAD_DIM + DENOM_ROWS, tq), F32),
                        pltpu.VMEM((2, 2 * n_pairs, tq, tq), F32),
                        pltpu.VMEM((2, 2 * n_pairs, 1, tq), F32)],
        compiler_params=pltpu.CompilerParams(
            dimension_semantics=("arbitrary", "arbitrary"), vmem_limit_bytes=VMEM_LIMIT),
        name="fox_attention",
    )(qt, kaug, vt)


def _postmix_kernel(attn_ref, diff_ref, x_ref, mod_ref, wpool_ref, pscale_ref, wout_ref, g_ref,
                    wr_hi_ref, wr_lo_ref, br_ref, before_ref,
                    x1_ref, xs_ref, ids_ref, wts_ref, pos_ref, *, tm):
    pooled = []
    for g in range(len(POOL_WINDOWS)):
        c0 = g * POOL_GROUP_DIM
        pooled.append(jnp.dot(diff_ref[0, :, c0:c0 + POOL_GROUP_DIM], wpool_ref[g], preferred_element_type=F32))
    pool_out = (jnp.concatenate(pooled, axis=1) * pscale_ref[...]).astype(BF16)
    cat = jnp.concatenate([attn_ref[0], pool_out], axis=1)
    mix = jnp.dot(cat, wout_ref[...], preferred_element_type=F32)
    x1 = x_ref[0] + mod_ref[0, 2:3, :] * mix
    x1_ref[0] = x1

    h = _rms_modulate(x1, g_ref[...], mod_ref[0, 3:4, :], mod_ref[0, 4:5, :])

    h_hi = h.astype(BF16)
    h_lo = (h - h_hi.astype(F32)).astype(BF16)
    logits = (_nt_dot(wr_hi_ref[...], h_hi) + _nt_dot(wr_lo_ref[...], h_hi) + _nt_dot(wr_hi_ref[...], h_lo)
              + br_ref[...])
    sub = lax.broadcasted_iota(jnp.int32, (SUBLANES, tm), 0)
    lg = jnp.where(sub < N_EXPERT_GROUPS, logits[0:SUBLANES, :], NEG_INF)
    g_max = jnp.max(lg, axis=0, keepdims=True)
    top_p = 1.0 / jnp.sum(jnp.exp(lg - g_max), axis=0, keepdims=True)
    top_g = jnp.min(jnp.where(lg == g_max, sub, SUBLANES), axis=0, keepdims=True)
    le = logits[SUBLANES:2 * SUBLANES, :]
    for g in range(1, N_EXPERT_GROUPS):
        le = jnp.where(top_g == g, logits[(g + 1) * SUBLANES:(g + 2) * SUBLANES, :], le)
    v1 = jnp.max(le, axis=0, keepdims=True)
    i1 = jnp.min(jnp.where(le == v1, sub, SUBLANES), axis=0, keepdims=True)
    le2 = jnp.where(sub == i1, NEG_INF, le)
    v2 = jnp.max(le2, axis=0, keepdims=True)
    i2 = jnp.min(jnp.where(le2 == v2, sub, SUBLANES), axis=0, keepdims=True)
    e2 = jnp.exp(v2 - v1)
    w1 = top_p / (1.0 + e2)
    id0 = top_g * EXPERTS_PER_GROUP + i1
    id1 = top_g * EXPERTS_PER_GROUP + i2
    ids_ref[...] = jnp.concatenate([id0, id1], axis=0)
    wts_ref[...] = jnp.concatenate([w1, w1 * e2], axis=0)

    sub_e = lax.broadcasted_iota(jnp.int32, (N_EXPERTS, tm), 0)
    onehot = jnp.concatenate([sub_e == id0, sub_e == id1], axis=1)
    oh_f = onehot.astype(F32)
    rank = jnp.dot(onehot.astype(BF16), before_ref[...], preferred_element_type=F32)
    chunks = jnp.floor((jnp.sum(oh_f, axis=1, keepdims=True) + (CHUNK - 1.0)) * (1.0 / CHUNK))
    er = lax.broadcasted_iota(jnp.int32, (N_EXPERTS, N_EXPERTS), 0)
    ec = lax.broadcasted_iota(jnp.int32, (N_EXPERTS, N_EXPERTS), 1)
    first_chunk = jnp.dot((er > ec).astype(BF16), jnp.broadcast_to(chunks, (N_EXPERTS, LANES)).astype(BF16),
                          preferred_element_type=F32)[:, 0:1]
    pos = jnp.sum(oh_f * (rank + CHUNK * first_chunk), axis=0, keepdims=True)
    pos0 = pos[:, :tm]
    pos1 = pos[:, tm:]
    pos_ref[...] = jnp.concatenate([pos0, pos1], axis=0)
    r_iota = lax.broadcasted_iota(jnp.int32, (xs_ref.shape[0], tm), 0)
    perm = ((r_iota == pos0.astype(jnp.int32)) | (r_iota == pos1.astype(jnp.int32))).astype(BF16)
    xs_ref[...] = jnp.dot(perm, h_hi, preferred_element_type=F32).astype(BF16)


def _postmix(attn, diff, x, mod, wpool, pscale, wout, g, wr_hi, wr_lo, br, *, tm):
    B, S, D = x.shape
    T = B * S
    nst = S // tm
    rows = _sorted_rows(tm)
    before = jnp.asarray(np.triu(np.ones((TOP_K * tm, TOP_K * tm), np.float32), k=1), BF16)
    row_spec = lambda w: pl.BlockSpec((1, tm, w), lambda b, s: (b, s, 0))
    const = lambda a: pl.BlockSpec(a.shape, lambda b, s: (0,) * a.ndim)
    tok_spec = pl.BlockSpec((TOP_K, tm), lambda b, s: (0, b * nst + s))
    return pl.pallas_call(
        functools.partial(_postmix_kernel, tm=tm),
        grid=(B, nst),
        in_specs=[row_spec(ATTN_WIDTH), row_spec(POOL_WIDTH), row_spec(D),
                  pl.BlockSpec((1, N_MOD, D), lambda b, s: (b, 0, 0)),
                  const(wpool), const(pscale), const(wout), const(g), const(wr_hi), const(wr_lo), const(br),
                  const(before)],
        out_specs=[row_spec(D),
                   pl.BlockSpec((rows, D), lambda b, s: (b * nst + s, 0)),
                   tok_spec, tok_spec, tok_spec],
        out_shape=[
            jax.ShapeDtypeStruct((B, S, D), F32),
            jax.ShapeDtypeStruct((B * nst * rows, D), BF16),
            jax.ShapeDtypeStruct((TOP_K, T), jnp.int32),
            jax.ShapeDtypeStruct((TOP_K, T), F32),
            jax.ShapeDtypeStruct((TOP_K, T), F32),
        ],
        compiler_params=pltpu.CompilerParams(
            dimension_semantics=("arbitrary", "arbitrary"), vmem_limit_bytes=VMEM_LIMIT),
        name="postmix_router",
    )(attn, diff, x, mod, wpool, pscale, wout, g, wr_hi, wr_lo, br, before)


def _chunk_copy(src_hbm, src_row, dst, dst_row, sem):
    return pltpu.make_async_copy(src_hbm.at[pl.ds(pl.multiple_of(src_row, CHUNK), CHUNK), :],
                                 dst.at[pl.ds(pl.multiple_of(dst_row, CHUNK), CHUNK), :], sem)


def _gather_chunks(table_ref, tile, n_chunks, src_hbm, dst, sem):
    for c in range(n_chunks):
        _chunk_copy(src_hbm, table_ref[tile * n_chunks + c], dst, c * CHUNK, sem).start(priority=c % 2)


def _gather_wait(src_hbm, dst, sem):
    pltpu.make_async_copy(src_hbm.at[pl.ds(0, dst.shape[0]), :], dst, sem).wait()


def _expert_kernel(te_ref, tv_ref, src_ref, xs_hbm, wg_ref, wu_ref, wd_ref, o_ref, buf, sem, wg_b, wu_b, wd_b,
                   *, tm, nt):
    i = pl.program_id(0)
    n_chunks = tm // CHUNK
    slot = i % 2

    @pl.when(jnp.logical_and(i == 0, tv_ref[0] == 1))
    def _():
        _gather_chunks(src_ref, 0, n_chunks, xs_hbm, buf.at[0], sem.at[0])

    @pl.when(jnp.logical_or(i == 0, te_ref[i] != te_ref[jnp.maximum(i - 1, 0)]))
    def _():
        wg_b[...] = wg_ref[0].astype(BF16)
        wu_b[...] = wu_ref[0].astype(BF16)
        wd_b[...] = wd_ref[0].astype(BF16)

    @pl.when(tv_ref[i] == 1)
    def _():
        @pl.when(jnp.logical_and(i + 1 < nt, tv_ref[jnp.minimum(i + 1, nt - 1)] == 1))
        def _():
            _gather_chunks(src_ref, i + 1, n_chunks, xs_hbm, buf.at[1 - slot], sem.at[1 - slot])

        _gather_wait(xs_hbm, buf.at[slot], sem.at[slot])
        half = tm // 2
        gate_up = []
        for r0 in (0, half):
            x = buf[slot, r0:r0 + half, :]
            gate_up.append((jnp.dot(x, wg_b[...], preferred_element_type=F32),
                            jnp.dot(x, wu_b[...], preferred_element_type=F32)))
        for r0, (a, b) in zip((0, half), gate_up):
            act = (_silu(a) * b).astype(BF16)
            o_ref[r0:r0 + half, :] = jnp.dot(act, wd_b[...], preferred_element_type=F32).astype(BF16)

    @pl.when(tv_ref[i] == 0)
    def _():
        o_ref[...] = jnp.zeros(o_ref.shape, BF16)


def _experts(tile_expert, tile_valid, chunk_src, xs, wg, wu, wd, *, tm, layer):
    nt = tile_expert.shape[0]
    _, D, Fe = wg.shape
    expert = lambda i, te: layer * N_EXPERTS + te[i]
    grid_spec = pltpu.PrefetchScalarGridSpec(
        num_scalar_prefetch=3,
        grid=(nt,),
        in_specs=[
            pl.BlockSpec(memory_space=pl.ANY),
            pl.BlockSpec((1, D, Fe), lambda i, te, tv, cs: (expert(i, te), 0, 0)),
            pl.BlockSpec((1, D, Fe), lambda i, te, tv, cs: (expert(i, te), 0, 0)),
            pl.BlockSpec((1, Fe, D), lambda i, te, tv, cs: (expert(i, te), 0, 0)),
        ],
        out_specs=pl.BlockSpec((tm, D), lambda i, te, tv, cs: (i, 0)),
        scratch_shapes=[pltpu.VMEM((2, tm, D), BF16), pltpu.SemaphoreType.DMA((2,)),
                        pltpu.VMEM((D, Fe), BF16), pltpu.VMEM((D, Fe), BF16), pltpu.VMEM((Fe, D), BF16)],
    )
    return pl.pallas_call(
        functools.partial(_expert_kernel, tm=tm, nt=nt),
        grid_spec=grid_spec,
        out_shape=jax.ShapeDtypeStruct((nt * tm, D), BF16),
        compiler_params=pltpu.CompilerParams(dimension_semantics=("arbitrary",), vmem_limit_bytes=VMEM_LIMIT),
        name="moe_experts",
    )(tile_expert, tile_valid, chunk_src, xs, wg, wu, wd)


def _combine_kernel(npair_ref, dst_ref, o_hbm, x1_ref, pos_ref, wts_ref, mod_ref, *rest, tm, nt, final):
    if final:
        gf_ref, out_ref, buf, sem = rest
    else:
        out_ref, buf, sem = rest
    i = pl.program_id(0)
    rows = buf.shape[1]
    max_chunks = rows // CHUNK
    slot = i % 2

    def gather(t, s):
        def body(p, carry):
            for k in range(2):
                c = 2 * p + k
                _chunk_copy(o_hbm, dst_ref[t * max_chunks + c], buf.at[s], c * CHUNK, sem.at[s]).start(priority=k)
            return carry

        lax.fori_loop(0, npair_ref[t], body, 0)

    def gather_wait(t, s):
        def body(p, carry):
            for k in range(2):
                _chunk_copy(o_hbm, 0, buf.at[s], (2 * p + k) * CHUNK, sem.at[s]).wait()
            return carry

        lax.fori_loop(0, npair_ref[t], body, 0)

    @pl.when(i == 0)
    def _():
        buf[...] = jnp.zeros(buf.shape, BF16)
        gather(0, 0)

    @pl.when(i + 1 < nt)
    def _():
        gather(jnp.minimum(i + 1, nt - 1), 1 - slot)

    gather_wait(i, slot)
    rep = lambda r: jnp.broadcast_to(r, (LANES, tm)).T
    p0, p1 = rep(pos_ref[0:1, :]), rep(pos_ref[1:2, :])
    w0, w1 = rep(wts_ref[0:1, :]), rep(wts_ref[1:2, :])
    lane = lax.broadcasted_iota(jnp.int32, (tm, LANES), 1).astype(F32)
    cols = []
    for c in range(rows // LANES):
        r = lane + float(c * LANES)
        cols.append((jnp.where(p0 == r, w0, 0.0) + jnp.where(p1 == r, w1, 0.0)).astype(BF16))
    comb = jnp.concatenate(cols, axis=1)
    y = jnp.dot(comb, buf[slot], preferred_element_type=F32)
    x2 = x1_ref[0] + mod_ref[0, 5:6, :] * y
    if final:
        ms = jnp.mean(x2 * x2, axis=-1, keepdims=True)
        x2 = x2 * lax.rsqrt(ms + EPS) * gf_ref[...]
    out_ref[0] = x2


def _combine(n_chunks, chunk_dst, o_sorted, x1, pos, wts, mod, g_final, *, tm):
    B, S, D = x1.shape
    nst = S // tm
    nt = B * nst
    rows = _sorted_rows(tm)
    final = g_final is not None
    tok_spec = pl.BlockSpec((TOP_K, tm), lambda i, np_, cd: (0, i))
    in_specs = [
        pl.BlockSpec(memory_space=pl.ANY),
        pl.BlockSpec((1, tm, D), lambda i, np_, cd: (i // nst, i % nst, 0)),
        tok_spec, tok_spec,
        pl.BlockSpec((1, N_MOD, D), lambda i, np_, cd: (i // nst, 0, 0)),
    ]
    args = [o_sorted, x1, pos, wts, mod]
    if final:
        in_specs.append(pl.BlockSpec((1, D), lambda i, np_, cd: (0, 0)))
        args.append(g_final)
    grid_spec = pltpu.PrefetchScalarGridSpec(
        num_scalar_prefetch=2,
        grid=(nt,),
        in_specs=in_specs,
        out_specs=pl.BlockSpec((1, tm, D), lambda i, np_, cd: (i // nst, i % nst, 0)),
        scratch_shapes=[pltpu.VMEM((2, rows, D), BF16), pltpu.SemaphoreType.DMA((2,))],
    )
    return pl.pallas_call(
        functools.partial(_combine_kernel, tm=tm, nt=nt, final=final),
        grid_spec=grid_spec,
        out_shape=jax.ShapeDtypeStruct((B, S, D), F32),
        compiler_params=pltpu.CompilerParams(dimension_semantics=("arbitrary",), vmem_limit_bytes=VMEM_LIMIT),
        name="moe_combine",
    )((n_chunks + 1) // 2, chunk_dst, *args)


def _dispatch_tables(ids, *, tm, tm_e, nt_e):
    T = ids.shape[1]
    nts = T // tm
    rows = _sorted_rows(tm)
    max_chunks = rows // CHUNK
    cpt = tm_e // CHUNK
    experts = jnp.arange(N_EXPERTS, dtype=jnp.int32)
    onehot = (ids.reshape(TOP_K, nts, tm)[..., None] == experts).astype(jnp.int32)
    seg_chunks = (jnp.sum(onehot, axis=(0, 2)) + CHUNK - 1) // CHUNK
    local_first = jnp.cumsum(seg_chunks, axis=1) - seg_chunks
    n_chunks = jnp.sum(seg_chunks, axis=1)
    expert_chunks = jnp.sum(seg_chunks, axis=0)
    region = ((expert_chunks + cpt - 1) // cpt) * cpt
    region_end = jnp.cumsum(region)
    seg_first = (region_end - region)[None, :] + jnp.cumsum(seg_chunks, axis=0) - seg_chunks
    ci = jnp.arange(max_chunks, dtype=jnp.int32)
    in_seg = (ci[None, :, None] >= local_first[:, None, :]) & (ci[None, :, None] < (local_first + seg_chunks)[:, None, :])
    gchunk = jnp.sum(in_seg * (seg_first - local_first)[:, None, :], axis=2) + ci[None, :]
    used = ci[None, :] < n_chunks[:, None]
    chunk_dst = jnp.where(used, gchunk * CHUNK, 0).reshape(-1).astype(jnp.int32)
    n_global = nt_e * cpt
    local_row = jnp.arange(nts, dtype=jnp.int32)[:, None] * rows + ci[None, :] * CHUNK
    zero_chunk_row = rows - CHUNK
    chunk_src = jnp.full((n_global,), zero_chunk_row, jnp.int32).at[
        jnp.where(used, gchunk, n_global).reshape(-1)].set(local_row.reshape(-1), mode="drop")
    tile_start = jnp.arange(nt_e, dtype=jnp.int32) * cpt
    tile_expert = jnp.minimum(jnp.sum((tile_start[:, None] >= region_end[None, :]).astype(jnp.int32), axis=1),
                              N_EXPERTS - 1)
    tile_valid = (tile_start < region_end[-1]).astype(jnp.int32)
    return n_chunks.astype(jnp.int32), chunk_dst, chunk_src, tile_expert, tile_valid


def kernel(x, c, norm_mix_g, norm_ffn_g, norm_final_g, w_ada, b_ada, w_in, b_fgate, w_pool, pool_scale, w_out,
           w_router_group, b_router_group, w_router_expert, b_router_expert, w_expert_gate, w_expert_up,
           w_expert_down):
    B, S, D = x.shape
    L = w_ada.shape[0]
    T = B * S
    tm_mix = min(512, S)
    tq = min(256, S)
    tm_e = 512
    chunks_per_tile = tm_e // CHUNK
    max_used = (T // tm_mix) * (_sorted_rows(tm_mix) // CHUNK - 1) + N_EXPERTS * (chunks_per_tile - 1)
    nt_e = -(-max_used // chunks_per_tile)

    mod_all = _ada_modulation(c, w_ada, b_ada).reshape(L, B, N_MOD, D)
    Fe = w_expert_gate.shape[-1]
    wg_all = w_expert_gate.reshape(L * N_EXPERTS, D, Fe)
    wu_all = w_expert_up.reshape(L * N_EXPERTS, D, Fe)
    wd_all = w_expert_down.reshape(L * N_EXPERTS, Fe, D)
    sel = _forget_routing()
    for l in range(L):
        mod = mod_all[l]
        w_in_l = w_in[l]
        wqkv = w_in_l[:, :3 * ATTN_WIDTH].astype(BF16)
        wf = jnp.pad(w_in_l[:, 3 * ATTN_WIDTH:3 * ATTN_WIDTH + ATTN_HEADS], ((0, 0), (0, LANES - ATTN_HEADS))).astype(BF16)
        bf = jnp.pad(b_fgate[l].astype(F32), (0, LANES - ATTN_HEADS)).reshape(1, LANES)
        wu = w_in_l[:, 3 * ATTN_WIDTH + ATTN_HEADS:].astype(BF16)
        qt, kaug, vt, diff = _premix(x, mod, norm_mix_g[l].reshape(1, D), wqkv, wf, bf, wu, sel, tm=tm_mix)
        attn = _attention(qt, kaug, vt, tq=tq)

        wr = jnp.concatenate([
            jnp.pad(w_router_group[l].T, ((0, SUBLANES - N_EXPERT_GROUPS), (0, 0))),
            w_router_expert[l].transpose(0, 2, 1).reshape(N_EXPERTS, D)], axis=0)
        wr_hi = wr.astype(BF16)
        wr_lo = (wr - wr_hi.astype(F32)).astype(BF16)
        br = jnp.concatenate([jnp.pad(b_router_group[l], (0, SUBLANES - N_EXPERT_GROUPS)),
                              b_router_expert[l].reshape(N_EXPERTS)]).reshape(ROUTER_ROWS, 1).astype(F32)
        x1, xs, ids, wts, pos = _postmix(attn, diff, x, mod, w_pool[l].astype(BF16),
                                         pool_scale[l].reshape(1, POOL_WIDTH), w_out[l].astype(BF16),
                                         norm_ffn_g[l].reshape(1, D), wr_hi, wr_lo, br, tm=tm_mix)

        n_chunks, chunk_dst, chunk_src, tile_expert, tile_valid = _dispatch_tables(ids, tm=tm_mix, tm_e=tm_e, nt_e=nt_e)
        o_sorted = _experts(tile_expert, tile_valid, chunk_src, xs, wg_all, wu_all, wd_all, tm=tm_e, layer=l)
        g_final = norm_final_g.reshape(1, D) if l == L - 1 else None
        x = _combine(n_chunks, chunk_dst, o_sorted, x1, pos, wts, mod, g_final, tm=tm_mix)
    return x
```

```python
import functools

import jax
import jax.numpy as jnp
import numpy as np
from jax import lax
from jax.experimental import pallas as pl
from jax.experimental.pallas import tpu as pltpu

ATTN_HEADS = 8
HEAD_DIM = 64
ATTN_WIDTH = ATTN_HEADS * HEAD_DIM
POOL_WINDOWS = (2, 4, 8, 16)
POOL_GROUP_DIM = 128
POOL_WIDTH = POOL_GROUP_DIM * len(POOL_WINDOWS)
POOL_HALO = 16
N_EXPERT_GROUPS = 4
EXPERTS_PER_GROUP = 8
N_EXPERTS = N_EXPERT_GROUPS * EXPERTS_PER_GROUP
N_MOD = 6
EPS = 1e-6
NEG_INF = -1e30
LOG2E = 1.4426950408889634

LANES = 128
SUBLANES = 8
AUG = 2 * LANES
DENOM_ROWS = 16
ROUTER_ROWS = 40
CHUNK = 32
TOP_K = 2
VMEM_LIMIT = 48 * 1024 * 1024


def _sorted_rows(tm):
    worst = TOP_K * tm + N_EXPERTS * (CHUNK - 1)
    return (worst // LANES + 1) * LANES

F32 = jnp.float32
BF16 = jnp.bfloat16


def _silu(a):
    return a * jax.nn.sigmoid(a)


def _nt_dot(a, b):
    return lax.dot_general(a, b, (((1,), (1,)), ((), ())), preferred_element_type=F32)


def _split3(a):
    t0 = a.astype(BF16)
    r1 = a - t0.astype(F32)
    t1 = r1.astype(BF16)
    t2 = (r1 - t1.astype(F32)).astype(BF16)
    return t0, t1, t2


def _rms_modulate(x, g, shift, scale):
    ms = jnp.mean(x * x, axis=-1, keepdims=True)
    y = x * lax.rsqrt(ms + EPS) * g
    return y * (1.0 + scale) + shift


def _ada_kernel(c_ref, w_ref, b_ref, o_ref):
    ca = _silu(c_ref[...])
    o_ref[0] = jnp.dot(ca, w_ref[0], precision=lax.Precision.HIGHEST, preferred_element_type=F32) + b_ref[0]


def _ada_modulation(c, w_ada, b_ada):
    L, D, W = w_ada.shape
    B = c.shape[0]
    tn = W // 4
    return pl.pallas_call(
        _ada_kernel,
        grid=(L, W // tn),
        in_specs=[
            pl.BlockSpec((B, D), lambda l, n: (0, 0)),
            pl.BlockSpec((1, D, tn), lambda l, n: (l, 0, n)),
            pl.BlockSpec((1, 1, tn), lambda l, n: (l, 0, n)),
        ],
        out_specs=pl.BlockSpec((1, B, tn), lambda l, n: (l, 0, n)),
        out_shape=jax.ShapeDtypeStruct((L, B, W), F32),
        compiler_params=pltpu.CompilerParams(vmem_limit_bytes=VMEM_LIMIT),
        name="ada_modulation",
    )(c, w_ada, b_ada.reshape(L, 1, W))


def _premix_kernel(x_ref, mod_ref, g_ref, wqkv_ref, wf_ref, bf_ref, wu_ref, sel_ref,
                   qt_ref, kaug_ref, vt_ref, diff_ref, carry_ref, ubuf_ref, *, tm):
    si = pl.program_id(1)

    @pl.when(si == 0)
    def _():
        carry_ref[...] = jnp.zeros_like(carry_ref)
        ubuf_ref[0:POOL_HALO, :] = jnp.zeros((POOL_HALO, POOL_WIDTH), F32)

    h = _rms_modulate(x_ref[0], g_ref[...], mod_ref[0, 0:1, :], mod_ref[0, 1:2, :]).astype(BF16)
    u = jnp.dot(h, wu_ref[...], preferred_element_type=F32)
    fl = jnp.dot(h, wf_ref[...], preferred_element_type=F32) + bf_ref[...]
    qkv = jnp.dot(h, wqkv_ref[...], preferred_element_type=F32)

    ubuf_ref[POOL_HALO:POOL_HALO + tm, :] = u
    pos = si * tm + lax.broadcasted_iota(jnp.int32, (tm, POOL_GROUP_DIM), 0)
    diffs = []
    for g, w in enumerate(POOL_WINDOWS):
        c0 = g * POOL_GROUP_DIM
        ug = u[:, c0:c0 + POOL_GROUP_DIM]
        acc = ug
        for j in range(1, w):
            acc = acc + ubuf_ref[POOL_HALO - j:POOL_HALO - j + tm, c0:c0 + POOL_GROUP_DIM]
        cnt = jnp.minimum(pos + 1, w).astype(F32)
        diffs.append((acc / cnt - ug).astype(BF16))
    diff_ref[0] = jnp.concatenate(diffs, axis=1)
    ubuf_ref[0:POOL_HALO, :] = u[tm - POOL_HALO:, :]

    lf = jnp.minimum(fl, 0.0) - jnp.log1p(jnp.exp(-jnp.abs(fl)))
    row = lax.broadcasted_iota(jnp.int32, (tm, tm), 0)
    col = lax.broadcasted_iota(jnp.int32, (tm, tm), 1)
    tri = (row >= col).astype(BF16)
    cs = None
    for term in _split3(lf):
        d = jnp.dot(tri, term, preferred_element_type=F32)
        cs = d if cs is None else cs + d
    f_cum = cs + carry_ref[...]
    carry_ref[...] = f_cum[tm - 1:tm, :]

    aug = None
    for i, term in enumerate(_split3(-LOG2E * f_cum)):
        d = jnp.dot(term, sel_ref[i], preferred_element_type=F32)
        aug = d if aug is None else aug + d
    kaug_ref[0] = jnp.concatenate([qkv[:, ATTN_WIDTH:2 * ATTN_WIDTH], aug], axis=1).astype(BF16)
    qt_ref[0] = (qkv[:, :ATTN_WIDTH] * (LOG2E * HEAD_DIM ** -0.5)).T.astype(BF16)
    vt_ref[0] = qkv[:, 2 * ATTN_WIDTH:].T.astype(BF16)


def _premix(x, mod, g, wqkv, wf, bf, wu, sel, *, tm):
    B, S, D = x.shape
    row_spec = lambda w: pl.BlockSpec((1, tm, w), lambda b, s: (b, s, 0))
    const = lambda a: pl.BlockSpec(a.shape, lambda b, s: (0,) * a.ndim)
    kaug_w = ATTN_WIDTH + LANES
    return pl.pallas_call(
        functools.partial(_premix_kernel, tm=tm),
        grid=(B, S // tm),
        in_specs=[
            row_spec(D),
            pl.BlockSpec((1, N_MOD, D), lambda b, s: (b, 0, 0)),
            const(g), const(wqkv), const(wf), const(bf), const(wu), const(sel),
        ],
        out_specs=[pl.BlockSpec((1, ATTN_WIDTH, tm), lambda b, s: (b, 0, s)), row_spec(kaug_w),
                   pl.BlockSpec((1, ATTN_WIDTH, tm), lambda b, s: (b, 0, s)), row_spec(POOL_WIDTH)],
        out_shape=[
            jax.ShapeDtypeStruct((B, ATTN_WIDTH, S), BF16),
            jax.ShapeDtypeStruct((B, S, kaug_w), BF16),
            jax.ShapeDtypeStruct((B, ATTN_WIDTH, S), BF16),
            jax.ShapeDtypeStruct((B, S, POOL_WIDTH), BF16),
        ],
        scratch_shapes=[pltpu.VMEM((1, LANES), F32), pltpu.VMEM((POOL_HALO + tm, POOL_WIDTH), F32)],
        compiler_params=pltpu.CompilerParams(
            dimension_semantics=("arbitrary", "arbitrary"), vmem_limit_bytes=VMEM_LIMIT),
        name="premix",
    )(x, mod, g, wqkv, wf, bf, wu, sel)


def _forget_routing():
    sel = np.zeros((3, LANES, LANES), np.float32)
    for i in range(3):
        for h in range(ATTN_HEADS):
            sel[i, h, 3 * h + i] = 1.0
    return jnp.asarray(sel, BF16)


def _attn_kernel(qt_in_ref, kaug_ref, vt_ref, o_ref, qt_ref, m_ref, acc_ref, s_ref, mb_ref, *, tq, n_pairs):
    tk = tq
    qi = pl.program_id(1)
    n_heads = 2 * n_pairs
    @pl.when(qi == 0)
    def _():
        r128 = lax.broadcasted_iota(jnp.int32, (LANES, tq), 0)
        for h in range(n_heads):
            qt_ref[h, 0:LANES, :] = jnp.zeros((LANES, tq), BF16)
            qt_ref[h, LANES:AUG, :] = ((r128 >= 3 * h) & (r128 < 3 * h + 3)).astype(BF16)

    for h in range(n_heads):
        r0 = HEAD_DIM * (h % 2)
        qt_ref[h, r0:r0 + HEAD_DIM, :] = qt_in_ref[0, h * HEAD_DIM:(h + 1) * HEAD_DIM, :]
    m_ref[...] = jnp.full(m_ref.shape, NEG_INF, F32)
    acc_ref[...] = jnp.zeros(acc_ref.shape, F32)
    ones = jnp.ones((DENOM_ROWS, tk), BF16)

    def step(new=None, cur=None):
        if new is not None:
            jn, slot_n, masked = new
            k0n = pl.multiple_of(jn * tk, tk)
            f_terms = kaug_ref[0, pl.ds(k0n, tk), n_pairs * LANES:(n_pairs + 1) * LANES]
        if cur is not None:
            jc, slot_c = cur
            k0c = pl.multiple_of(jc * tk, tk)
        for h in range(n_heads):
            if new is not None:
                k_pair = kaug_ref[0, pl.ds(k0n, tk), (h // 2) * LANES:(h // 2 + 1) * LANES]
                s = jnp.dot(jnp.concatenate([k_pair, f_terms], axis=1), qt_ref[h],
                            preferred_element_type=F32)
                if masked:
                    key = lax.broadcasted_iota(jnp.int32, (tk, tq), 0)
                    qry = lax.broadcasted_iota(jnp.int32, (tk, tq), 1)
                    s = jnp.where(key <= qry, s, NEG_INF)
                s_ref[slot_n, h] = s
                mb_ref[slot_n, h] = jnp.max(s, axis=0, keepdims=True)
            if cur is not None:
                m_prev = m_ref[h]
                m_new = jnp.maximum(m_prev, mb_ref[slot_c, h])
                pt = jnp.exp2(s_ref[slot_c, h] - m_new).astype(BF16)
                alpha = jnp.exp2(m_prev - m_new)
                vtb = vt_ref[0, pl.ds(h * HEAD_DIM, HEAD_DIM), pl.ds(k0c, tk)]
                lhs = jnp.concatenate([vtb, ones], axis=0)
                acc_ref[h] = alpha * acc_ref[h] + jnp.dot(lhs, pt, preferred_element_type=F32)
                m_ref[h] = m_new

    @pl.when(qi == 0)
    def _():
        step(new=(0, 0, True))

    @pl.when(qi > 0)
    def _():
        step(new=(0, 0, False))

    def body(jj, c):
        j = 2 * jj
        step(new=(j + 1, 1, False), cur=(j, 0))
        step(new=(j + 2, 0, False), cur=(j + 1, 1))
        return c

    n_double = jnp.maximum(qi - 1, 0) // 2
    lax.fori_loop(0, n_double, body, 0)
    j0 = 2 * n_double
    rem = qi - j0

    @pl.when(rem == 0)
    def _():
        step(cur=(0, 0))

    @pl.when(rem == 1)
    def _():
        step(new=(qi, 1, True), cur=(j0, 0))
        step(cur=(qi, 1))

    @pl.when(rem == 2)
    def _():
        step(new=(j0 + 1, 1, False), cur=(j0, 0))
        step(new=(qi, 0, True), cur=(j0 + 1, 1))
        step(cur=(qi, 0))

    for p in range(n_pairs):
        outs = []
        for hh in range(2):
            a = acc_ref[2 * p + hh]
            outs.append(a[:HEAD_DIM, :] / a[HEAD_DIM:HEAD_DIM + 1, :])
        o_ref[0, :, p * LANES:(p + 1) * LANES] = jnp.concatenate(outs, axis=0).T.astype(BF16)


def _attention(qt, kaug, vt, *, tq):
    B, W, S = qt.shape
    n_pairs = W // LANES
    return pl.pallas_call(
        functools.partial(_attn_kernel, tq=tq, n_pairs=n_pairs),
        grid=(B, S // tq),
        in_specs=[pl.BlockSpec((1, W, tq), lambda b, i: (b, 0, i)),
                  pl.BlockSpec((1, S, (n_pairs + 1) * LANES), lambda b, i: (b, 0, 0)),
                  pl.BlockSpec((1, W, S), lambda b, i: (b, 0, 0))],
        out_specs=pl.BlockSpec((1, tq, W), lambda b, i: (b, i, 0)),
        out_shape=jax.ShapeDtypeStruct((B, S, W), BF16),
        scratch_shapes=[pltpu.VMEM((2 * n_pairs, AUG, tq), BF16),
                        pltpu.VMEM((2 * n_pairs, 1, tq), F32),
                        pltpu.VMEM((2 * n_pairs, HEAD_DIM + DENOM_ROWS, tq), F32),
                        pltpu.VMEM((2, 2 * n_pairs, tq, tq), F32),
                        pltpu.VMEM((2, 2 * n_pairs, 1, tq), F32)],
        compiler_params=pltpu.CompilerParams(
            dimension_semantics=("arbitrary", "arbitrary"), vmem_limit_bytes=VMEM_LIMIT),
        name="fox_attention",
    )(qt, kaug, vt)


def _postmix_kernel(attn_ref, diff_ref, x_ref, mod_ref, wpool_ref, pscale_ref, wout_ref, g_ref,
                    wr_hi_ref, wr_lo_ref, br_ref, before_ref,
                    x1_ref, xs_ref, ids_ref, wts_ref, pos_ref, *, tm):
    pooled = []
    for g in range(len(POOL_WINDOWS)):
        c0 = g * POOL_GROUP_DIM
        pooled.append(jnp.dot(diff_ref[0, :, c0:c0 + POOL_GROUP_DIM], wpool_ref[g], preferred_element_type=F32))
    pool_out = (jnp.concatenate(pooled, axis=1) * pscale_ref[...]).astype(BF16)
    cat = jnp.concatenate([attn_ref[0], pool_out], axis=1)
    mix = jnp.dot(cat, wout_ref[...], preferred_element_type=F32)
    x1 = x_ref[0] + mod_ref[0, 2:3, :] * mix
    x1_ref[0] = x1

    h = _rms_modulate(x1, g_ref[...], mod_ref[0, 3:4, :], mod_ref[0, 4:5, :])

    h_hi = h.astype(BF16)
    h_lo = (h - h_hi.astype(F32)).astype(BF16)
    logits = (_nt_dot(wr_hi_ref[...], h_hi) + _nt_dot(wr_lo_ref[...], h_hi) + _nt_dot(wr_hi_ref[...], h_lo)
              + br_ref[...])
    sub = lax.broadcasted_iota(jnp.int32, (SUBLANES, tm), 0)
    lg = jnp.where(sub < N_EXPERT_GROUPS, logits[0:SUBLANES, :], NEG_INF)
    g_max = jnp.max(lg, axis=0, keepdims=True)
    top_p = 1.0 / jnp.sum(jnp.exp(lg - g_max), axis=0, keepdims=True)
    top_g = jnp.min(jnp.where(lg == g_max, sub, SUBLANES), axis=0, keepdims=True)
    le = logits[SUBLANES:2 * SUBLANES, :]
    for g in range(1, N_EXPERT_GROUPS):
        le = jnp.where(top_g == g, logits[(g + 1) * SUBLANES:(g + 2) * SUBLANES, :], le)
    v1 = jnp.max(le, axis=0, keepdims=True)
    i1 = jnp.min(jnp.where(le == v1, sub, SUBLANES), axis=0, keepdims=True)
    le2 = jnp.where(sub == i1, NEG_INF, le)
    v2 = jnp.max(le2, axis=0, keepdims=True)
    i2 = jnp.min(jnp.where(le2 == v2, sub, SUBLANES), axis=0, keepdims=True)
    e2 = jnp.exp(v2 - v1)
    w1 = top_p / (1.0 + e2)
    id0 = top_g * EXPERTS_PER_GROUP + i1
    id1 = top_g * EXPERTS_PER_GROUP + i2
    ids_ref[...] = jnp.concatenate([id0, id1], axis=0)
    wts_ref[...] = jnp.concatenate([w1, w1 * e2], axis=0)

    sub_e = lax.broadcasted_iota(jnp.int32, (N_EXPERTS, tm), 0)
    onehot = jnp.concatenate([sub_e == id0, sub_e == id1], axis=1)
    oh_f = onehot.astype(F32)
    rank = jnp.dot(onehot.astype(BF16), before_ref[...], preferred_element_type=F32)
    chunks = jnp.floor((jnp.sum(oh_f, axis=1, keepdims=True) + (CHUNK - 1.0)) * (1.0 / CHUNK))
    er = lax.broadcasted_iota(jnp.int32, (N_EXPERTS, N_EXPERTS), 0)
    ec = lax.broadcasted_iota(jnp.int32, (N_EXPERTS, N_EXPERTS), 1)
    first_chunk = jnp.dot((er > ec).astype(BF16), jnp.broadcast_to(chunks, (N_EXPERTS, LANES)).astype(BF16),
                          preferred_element_type=F32)[:, 0:1]
    pos = jnp.sum(oh_f * (rank + CHUNK * first_chunk), axis=0, keepdims=True)
    pos0 = pos[:, :tm]
    pos1 = pos[:, tm:]
    pos_ref[...] = jnp.concatenate([pos0, pos1], axis=0)
    r_iota = lax.broadcasted_iota(jnp.int32, (xs_ref.shape[0], tm), 0)
    perm = ((r_iota == pos0.astype(jnp.int32)) | (r_iota == pos1.astype(jnp.int32))).astype(BF16)
    xs_ref[...] = jnp.dot(perm, h_hi, preferred_element_type=F32).astype(BF16)


def _postmix(attn, diff, x, mod, wpool, pscale, wout, g, wr_hi, wr_lo, br, *, tm):
    B, S, D = x.shape
    T = B * S
    nst = S // tm
    rows = _sorted_rows(tm)
    before = jnp.asarray(np.triu(np.ones((TOP_K * tm, TOP_K * tm), np.float32), k=1), BF16)
    row_spec = lambda w: pl.BlockSpec((1, tm, w), lambda b, s: (b, s, 0))
    const = lambda a: pl.BlockSpec(a.shape, lambda b, s: (0,) * a.ndim)
    tok_spec = pl.BlockSpec((TOP_K, tm), lambda b, s: (0, b * nst + s))
    return pl.pallas_call(
        functools.partial(_postmix_kernel, tm=tm),
        grid=(B, nst),
        in_specs=[row_spec(ATTN_WIDTH), row_spec(POOL_WIDTH), row_spec(D),
                  pl.BlockSpec((1, N_MOD, D), lambda b, s: (b, 0, 0)),
                  const(wpool), const(pscale), const(wout), const(g), const(wr_hi), const(wr_lo), const(br),
                  const(before)],
        out_specs=[row_spec(D),
                   pl.BlockSpec((rows, D), lambda b, s: (b * nst + s, 0)),
                   tok_spec, tok_spec, tok_spec],
        out_shape=[
            jax.ShapeDtypeStruct((B, S, D), F32),
            jax.ShapeDtypeStruct((B * nst * rows, D), BF16),
            jax.ShapeDtypeStruct((TOP_K, T), jnp.int32),
            jax.ShapeDtypeStruct((TOP_K, T), F32),
            jax.ShapeDtypeStruct((TOP_K, T), F32),
        ],
        compiler_params=pltpu.CompilerParams(
            dimension_semantics=("arbitrary", "arbitrary"), vmem_limit_bytes=VMEM_LIMIT),
        name="postmix_router",
    )(attn, diff, x, mod, wpool, pscale, wout, g, wr_hi, wr_lo, br, before)


def _chunk_copy(src_hbm, src_chunk, dst, dst_chunk, sem):
    return pltpu.make_async_copy(src_hbm.at[src_chunk], dst.at[dst_chunk], sem)


def _gather_chunks(table_ref, tile, n_chunks, src_hbm, dst, sem):
    for c in range(n_chunks):
        _chunk_copy(src_hbm, table_ref[tile * n_chunks + c], dst, c, sem).start(priority=c % 2)


def _gather_wait(src_hbm, dst, sem):
    pltpu.make_async_copy(src_hbm.at[pl.ds(0, dst.shape[0])], dst, sem).wait()


def _expert_kernel(te_ref, tv_ref, src_ref, xs_hbm, wg_ref, wu_ref, wd_ref, o_ref, buf, sem, wg_b, wu_b, wd_b,
                   *, tm, nt):
    i = pl.program_id(0)
    n_chunks = tm // CHUNK
    slot = i % 2

    @pl.when(jnp.logical_and(i == 0, tv_ref[0] == 1))
    def _():
        _gather_chunks(src_ref, 0, n_chunks, xs_hbm, buf.at[0], sem.at[0])

    @pl.when(jnp.logical_or(i == 0, te_ref[i] != te_ref[jnp.maximum(i - 1, 0)]))
    def _():
        wg_b[...] = wg_ref[0].astype(BF16)
        wu_b[...] = wu_ref[0].astype(BF16)
        wd_b[...] = wd_ref[0].astype(BF16)

    @pl.when(tv_ref[i] == 1)
    def _():
        @pl.when(jnp.logical_and(i + 1 < nt, tv_ref[jnp.minimum(i + 1, nt - 1)] == 1))
        def _():
            _gather_chunks(src_ref, i + 1, n_chunks, xs_hbm, buf.at[1 - slot], sem.at[1 - slot])

        _gather_wait(xs_hbm, buf.at[slot], sem.at[slot])
        half = tm // 2
        gate_up = []
        for r0 in (0, half):
            x = buf[slot, r0 // CHUNK:(r0 + half) // CHUNK].reshape(half, buf.shape[-1])
            gate_up.append((jnp.dot(x, wg_b[...], preferred_element_type=F32),
                            jnp.dot(x, wu_b[...], preferred_element_type=F32)))
        for r0, (a, b) in zip((0, half), gate_up):
            act = (_silu(a) * b).astype(BF16)
            o_ref[r0:r0 + half, :] = jnp.dot(act, wd_b[...], preferred_element_type=F32).astype(BF16)

    @pl.when(tv_ref[i] == 0)
    def _():
        o_ref[...] = jnp.zeros(o_ref.shape, BF16)


def _experts(tile_expert, tile_valid, chunk_src, xs, wg, wu, wd, *, tm, layer):
    nt = tile_expert.shape[0]
    _, D, Fe = wg.shape
    expert = lambda i, te: layer * N_EXPERTS + te[i]
    grid_spec = pltpu.PrefetchScalarGridSpec(
        num_scalar_prefetch=3,
        grid=(nt,),
        in_specs=[
            pl.BlockSpec(memory_space=pl.ANY),
            pl.BlockSpec((1, D, Fe), lambda i, te, tv, cs: (expert(i, te), 0, 0)),
            pl.BlockSpec((1, D, Fe), lambda i, te, tv, cs: (expert(i, te), 0, 0)),
            pl.BlockSpec((1, Fe, D), lambda i, te, tv, cs: (expert(i, te), 0, 0)),
        ],
        out_specs=pl.BlockSpec((tm, D), lambda i, te, tv, cs: (i, 0)),
        scratch_shapes=[pltpu.VMEM((2, tm // CHUNK, CHUNK, D), BF16), pltpu.SemaphoreType.DMA((2,)),
                        pltpu.VMEM((D, Fe), BF16), pltpu.VMEM((D, Fe), BF16), pltpu.VMEM((Fe, D), BF16)],
    )
    return pl.pallas_call(
        functools.partial(_expert_kernel, tm=tm, nt=nt),
        grid_spec=grid_spec,
        out_shape=jax.ShapeDtypeStruct((nt * tm, D), BF16),
        compiler_params=pltpu.CompilerParams(dimension_semantics=("arbitrary",), vmem_limit_bytes=VMEM_LIMIT),
        name="moe_experts",
    )(tile_expert, tile_valid, chunk_src, xs.reshape(-1, CHUNK, D), wg, wu, wd)


def _combine_kernel(npair_ref, dst_ref, o_hbm, x1_ref, pos_ref, wts_ref, mod_ref, *rest, tm, nt, final):
    if final:
        gf_ref, out_ref, buf, sem = rest
    else:
        out_ref, buf, sem = rest
    i = pl.program_id(0)
    max_chunks = buf.shape[1]
    rows = max_chunks * CHUNK
    slot = i % 2

    def gather(t, s):
        def body(p, carry):
            for k in range(2):
                c = 2 * p + k
                _chunk_copy(o_hbm, dst_ref[t * max_chunks + c], buf.at[s], c, sem.at[s]).start(priority=k)
            return carry

        lax.fori_loop(0, npair_ref[t], body, 0)

    def gather_wait(t, s):
        def body(p, carry):
            for k in range(2):
                _chunk_copy(o_hbm, 0, buf.at[s], 2 * p + k, sem.at[s]).wait()
            return carry

        lax.fori_loop(0, npair_ref[t], body, 0)

    @pl.when(i == 0)
    def _():
        buf[...] = jnp.zeros(buf.shape, BF16)
        gather(0, 0)

    @pl.when(i + 1 < nt)
    def _():
        gather(jnp.minimum(i + 1, nt - 1), 1 - slot)

    gather_wait(i, slot)
    rep = lambda r: jnp.broadcast_to(r, (LANES, tm)).T
    p0, p1 = rep(pos_ref[0:1, :]), rep(pos_ref[1:2, :])
    w0, w1 = rep(wts_ref[0:1, :]), rep(wts_ref[1:2, :])
    lane = lax.broadcasted_iota(jnp.int32, (tm, LANES), 1).astype(F32)
    cols = []
    for c in range(rows // LANES):
        r = lane + float(c * LANES)
        cols.append((jnp.where(p0 == r, w0, 0.0) + jnp.where(p1 == r, w1, 0.0)).astype(BF16))
    comb = jnp.concatenate(cols, axis=1)
    y = jnp.dot(comb, buf[slot].reshape(rows, buf.shape[-1]), preferred_element_type=F32)
    x2 = x1_ref[0] + mod_ref[0, 5:6, :] * y
    if final:
        ms = jnp.mean(x2 * x2, axis=-1, keepdims=True)
        x2 = x2 * lax.rsqrt(ms + EPS) * gf_ref[...]
    out_ref[0] = x2


def _combine(n_chunks, chunk_dst, o_sorted, x1, pos, wts, mod, g_final, *, tm):
    B, S, D = x1.shape
    nst = S // tm
    nt = B * nst
    rows = _sorted_rows(tm)
    final = g_final is not None
    tok_spec = pl.BlockSpec((TOP_K, tm), lambda i, np_, cd: (0, i))
    in_specs = [
        pl.BlockSpec(memory_space=pl.ANY),
        pl.BlockSpec((1, tm, D), lambda i, np_, cd: (i // nst, i % nst, 0)),
        tok_spec, tok_spec,
        pl.BlockSpec((1, N_MOD, D), lambda i, np_, cd: (i // nst, 0, 0)),
    ]
    args = [o_sorted.reshape(-1, CHUNK, D), x1, pos, wts, mod]
    if final:
        in_specs.append(pl.BlockSpec((1, D), lambda i, np_, cd: (0, 0)))
        args.append(g_final)
    grid_spec = pltpu.PrefetchScalarGridSpec(
        num_scalar_prefetch=2,
        grid=(nt,),
        in_specs=in_specs,
        out_specs=pl.BlockSpec((1, tm, D), lambda i, np_, cd: (i // nst, i % nst, 0)),
        scratch_shapes=[pltpu.VMEM((2, rows // CHUNK, CHUNK, D), BF16), pltpu.SemaphoreType.DMA((2,))],
    )
    return pl.pallas_call(
        functools.partial(_combine_kernel, tm=tm, nt=nt, final=final),
        grid_spec=grid_spec,
        out_shape=jax.ShapeDtypeStruct((B, S, D), F32),
        compiler_params=pltpu.CompilerParams(dimension_semantics=("arbitrary",), vmem_limit_bytes=VMEM_LIMIT),
        name="moe_combine",
    )((n_chunks + 1) // 2, chunk_dst, *args)


def _dispatch_tables(ids, *, tm, tm_e, nt_e):
    T = ids.shape[1]
    nts = T // tm
    rows = _sorted_rows(tm)
    max_chunks = rows // CHUNK
    cpt = tm_e // CHUNK
    experts = jnp.arange(N_EXPERTS, dtype=jnp.int32)
    onehot = (ids.reshape(TOP_K, nts, tm)[..., None] == experts).astype(jnp.int32)
    seg_chunks = (jnp.sum(onehot, axis=(0, 2)) + CHUNK - 1) // CHUNK
    local_first = jnp.cumsum(seg_chunks, axis=1) - seg_chunks
    n_chunks = jnp.sum(seg_chunks, axis=1)
    expert_chunks = jnp.sum(seg_chunks, axis=0)
    region = ((expert_chunks + cpt - 1) // cpt) * cpt
    region_end = jnp.cumsum(region)
    seg_first = (region_end - region)[None, :] + jnp.cumsum(seg_chunks, axis=0) - seg_chunks
    ci = jnp.arange(max_chunks, dtype=jnp.int32)
    in_seg = (ci[None, :, None] >= local_first[:, None, :]) & (ci[None, :, None] < (local_first + seg_chunks)[:, None, :])
    gchunk = jnp.sum(in_seg * (seg_first - local_first)[:, None, :], axis=2) + ci[None, :]
    used = ci[None, :] < n_chunks[:, None]
    chunk_dst = jnp.where(used, gchunk, 0).reshape(-1).astype(jnp.int32)
    n_global = nt_e * cpt
    local_chunk = jnp.arange(nts, dtype=jnp.int32)[:, None] * max_chunks + ci[None, :]
    zero_chunk = max_chunks - 1
    chunk_src = jnp.full((n_global,), zero_chunk, jnp.int32).at[
        jnp.where(used, gchunk, n_global).reshape(-1)].set(local_chunk.reshape(-1), mode="drop")
    tile_start = jnp.arange(nt_e, dtype=jnp.int32) * cpt
    tile_expert = jnp.minimum(jnp.sum((tile_start[:, None] >= region_end[None, :]).astype(jnp.int32), axis=1),
                              N_EXPERTS - 1)
    tile_valid = (tile_start < region_end[-1]).astype(jnp.int32)
    return n_chunks.astype(jnp.int32), chunk_dst, chunk_src, tile_expert, tile_valid


def kernel(x, c, norm_mix_g, norm_ffn_g, norm_final_g, w_ada, b_ada, w_in, b_fgate, w_pool, pool_scale, w_out,
           w_router_group, b_router_group, w_router_expert, b_router_expert, w_expert_gate, w_expert_up,
           w_expert_down):
    B, S, D = x.shape
    L = w_ada.shape[0]
    T = B * S
    tm_mix = min(512, S)
    tq = min(256, S)
    tm_e = 512
    chunks_per_tile = tm_e // CHUNK
    max_used = (T // tm_mix) * (_sorted_rows(tm_mix) // CHUNK - 1) + N_EXPERTS * (chunks_per_tile - 1)
    nt_e = -(-max_used // chunks_per_tile)

    mod_all = _ada_modulation(c, w_ada, b_ada).reshape(L, B, N_MOD, D)
    Fe = w_expert_gate.shape[-1]
    wg_all = w_expert_gate.reshape(L * N_EXPERTS, D, Fe)
    wu_all = w_expert_up.reshape(L * N_EXPERTS, D, Fe)
    wd_all = w_expert_down.reshape(L * N_EXPERTS, Fe, D)
    sel = _forget_routing()
    for l in range(L):
        mod = mod_all[l]
        w_in_l = w_in[l]
        wqkv = w_in_l[:, :3 * ATTN_WIDTH].astype(BF16)
        wf = jnp.pad(w_in_l[:, 3 * ATTN_WIDTH:3 * ATTN_WIDTH + ATTN_HEADS], ((0, 0), (0, LANES - ATTN_HEADS))).astype(BF16)
        bf = jnp.pad(b_fgate[l].astype(F32), (0, LANES - ATTN_HEADS)).reshape(1, LANES)
        wu = w_in_l[:, 3 * ATTN_WIDTH + ATTN_HEADS:].astype(BF16)
        qt, kaug, vt, diff = _premix(x, mod, norm_mix_g[l].reshape(1, D), wqkv, wf, bf, wu, sel, tm=tm_mix)
        attn = _attention(qt, kaug, vt, tq=tq)

        wr = jnp.concatenate([
            jnp.pad(w_router_group[l].T, ((0, SUBLANES - N_EXPERT_GROUPS), (0, 0))),
            w_router_expert[l].transpose(0, 2, 1).reshape(N_EXPERTS, D)], axis=0)
        wr_hi = wr.astype(BF16)
        wr_lo = (wr - wr_hi.astype(F32)).astype(BF16)
        br = jnp.concatenate([jnp.pad(b_router_group[l], (0, SUBLANES - N_EXPERT_GROUPS)),
                              b_router_expert[l].reshape(N_EXPERTS)]).reshape(ROUTER_ROWS, 1).astype(F32)
        x1, xs, ids, wts, pos = _postmix(attn, diff, x, mod, w_pool[l].astype(BF16),
                                         pool_scale[l].reshape(1, POOL_WIDTH), w_out[l].astype(BF16),
                                         norm_ffn_g[l].reshape(1, D), wr_hi, wr_lo, br, tm=tm_mix)

        n_chunks, chunk_dst, chunk_src, tile_expert, tile_valid = _dispatch_tables(ids, tm=tm_mix, tm_e=tm_e, nt_e=nt_e)
        o_sorted = _experts(tile_expert, tile_valid, chunk_src, xs, wg_all, wu_all, wd_all, tm=tm_e, layer=l)
        g_final = norm_final_g.reshape(1, D) if l == L - 1 else None
        x = _combine(n_chunks, chunk_dst, o_sorted, x1, pos, wts, mod, g_final, tm=tm_mix)
    return x
```

```python
import functools

import jax
import jax.numpy as jnp
import numpy as np
from jax import lax
from jax.experimental import pallas as pl
from jax.experimental.pallas import tpu as pltpu

ATTN_HEADS = 8
HEAD_DIM = 64
ATTN_WIDTH = ATTN_HEADS * HEAD_DIM
POOL_WINDOWS = (2, 4, 8, 16)
POOL_GROUP_DIM = 128
POOL_WIDTH = POOL_GROUP_DIM * len(POOL_WINDOWS)
POOL_HALO = 16
N_EXPERT_GROUPS = 4
EXPERTS_PER_GROUP = 8
N_EXPERTS = N_EXPERT_GROUPS * EXPERTS_PER_GROUP
N_MOD = 6
EPS = 1e-6
NEG_INF = -1e30
LOG2E = 1.4426950408889634

LANES = 128
SUBLANES = 8
AUG = 2 * LANES
DENOM_ROWS = 16
ATTN_UNROLL = 4
ROUTER_ROWS = 40
CHUNK = 16
TOP_K = 2
VMEM_LIMIT = 48 * 1024 * 1024


def _sorted_rows(tm):
    worst = TOP_K * tm + N_EXPERTS * (CHUNK - 1)
    return (worst // LANES + 1) * LANES

F32 = jnp.float32
BF16 = jnp.bfloat16


def _silu(a):
    return a * jax.nn.sigmoid(a)


def _nt_dot(a, b):
    return lax.dot_general(a, b, (((1,), (1,)), ((), ())), preferred_element_type=F32)


def _split3(a):
    t0 = a.astype(BF16)
    r1 = a - t0.astype(F32)
    t1 = r1.astype(BF16)
    t2 = (r1 - t1.astype(F32)).astype(BF16)
    return t0, t1, t2


def _rms_modulate(x, g, shift, scale):
    ms = jnp.mean(x * x, axis=-1, keepdims=True)
    y = x * lax.rsqrt(ms + EPS) * g
    return y * (1.0 + scale) + shift


def _ada_kernel(c_ref, w_ref, b_ref, o_ref):
    ca = _silu(c_ref[...])
    o_ref[0] = jnp.dot(ca, w_ref[0], precision=lax.Precision.HIGHEST, preferred_element_type=F32) + b_ref[0]


def _ada_modulation(c, w_ada, b_ada):
    L, D, W = w_ada.shape
    B = c.shape[0]
    tn = W // 4
    return pl.pallas_call(
        _ada_kernel,
        grid=(L, W // tn),
        in_specs=[
            pl.BlockSpec((B, D), lambda l, n: (0, 0)),
            pl.BlockSpec((1, D, tn), lambda l, n: (l, 0, n)),
            pl.BlockSpec((1, 1, tn), lambda l, n: (l, 0, n)),
        ],
        out_specs=pl.BlockSpec((1, B, tn), lambda l, n: (l, 0, n)),
        out_shape=jax.ShapeDtypeStruct((L, B, W), F32),
        compiler_params=pltpu.CompilerParams(vmem_limit_bytes=VMEM_LIMIT),
        name="ada_modulation",
    )(c, w_ada, b_ada.reshape(L, 1, W))


def _premix_kernel(x_ref, mod_ref, g_ref, wqkv_ref, wf_ref, bf_ref, wu_ref, sel_ref,
                   qt_ref, kaug_ref, vt_ref, diff_ref, carry_ref, ubuf_ref, *, tm):
    si = pl.program_id(1)

    @pl.when(si == 0)
    def _():
        carry_ref[...] = jnp.zeros_like(carry_ref)
        ubuf_ref[0:POOL_HALO, :] = jnp.zeros((POOL_HALO, POOL_WIDTH), F32)

    h = _rms_modulate(x_ref[0], g_ref[...], mod_ref[0, 0:1, :], mod_ref[0, 1:2, :]).astype(BF16)
    u = jnp.dot(h, wu_ref[...], preferred_element_type=F32)
    fl = jnp.dot(h, wf_ref[...], preferred_element_type=F32) + bf_ref[...]
    qkv = jnp.dot(h, wqkv_ref[...], preferred_element_type=F32)

    ubuf_ref[POOL_HALO:POOL_HALO + tm, :] = u
    pos = si * tm + lax.broadcasted_iota(jnp.int32, (tm, POOL_GROUP_DIM), 0)
    diffs = []
    for g, w in enumerate(POOL_WINDOWS):
        c0 = g * POOL_GROUP_DIM
        ug = u[:, c0:c0 + POOL_GROUP_DIM]
        acc = ug
        for j in range(1, w):
            acc = acc + ubuf_ref[POOL_HALO - j:POOL_HALO - j + tm, c0:c0 + POOL_GROUP_DIM]
        cnt = jnp.minimum(pos + 1, w).astype(F32)
        diffs.append((acc / cnt - ug).astype(BF16))
    diff_ref[0] = jnp.concatenate(diffs, axis=1)
    ubuf_ref[0:POOL_HALO, :] = u[tm - POOL_HALO:, :]

    lf = jnp.minimum(fl, 0.0) - jnp.log1p(jnp.exp(-jnp.abs(fl)))
    row = lax.broadcasted_iota(jnp.int32, (tm, tm), 0)
    col = lax.broadcasted_iota(jnp.int32, (tm, tm), 1)
    tri = (row >= col).astype(BF16)
    cs = None
    for term in _split3(lf):
        d = jnp.dot(tri, term, preferred_element_type=F32)
        cs = d if cs is None else cs + d
    f_cum = cs + carry_ref[...]
    carry_ref[...] = f_cum[tm - 1:tm, :]

    aug = None
    for i, term in enumerate(_split3(-LOG2E * f_cum)):
        d = jnp.dot(term, sel_ref[i], preferred_element_type=F32)
        aug = d if aug is None else aug + d
    kaug_ref[0] = jnp.concatenate([qkv[:, ATTN_WIDTH:2 * ATTN_WIDTH], aug], axis=1).astype(BF16)
    qt_ref[0] = (qkv[:, :ATTN_WIDTH] * (LOG2E * HEAD_DIM ** -0.5)).T.astype(BF16)
    vt_ref[0] = qkv[:, 2 * ATTN_WIDTH:].T.astype(BF16)


def _premix(x, mod, g, wqkv, wf, bf, wu, sel, *, tm):
    B, S, D = x.shape
    row_spec = lambda w: pl.BlockSpec((1, tm, w), lambda b, s: (b, s, 0))
    const = lambda a: pl.BlockSpec(a.shape, lambda b, s: (0,) * a.ndim)
    kaug_w = ATTN_WIDTH + LANES
    return pl.pallas_call(
        functools.partial(_premix_kernel, tm=tm),
        grid=(B, S // tm),
        in_specs=[
            row_spec(D),
            pl.BlockSpec((1, N_MOD, D), lambda b, s: (b, 0, 0)),
            const(g), const(wqkv), const(wf), const(bf), const(wu), const(sel),
        ],
        out_specs=[pl.BlockSpec((1, ATTN_WIDTH, tm), lambda b, s: (b, 0, s)), row_spec(kaug_w),
                   pl.BlockSpec((1, ATTN_WIDTH, tm), lambda b, s: (b, 0, s)), row_spec(POOL_WIDTH)],
        out_shape=[
            jax.ShapeDtypeStruct((B, ATTN_WIDTH, S), BF16),
            jax.ShapeDtypeStruct((B, S, kaug_w), BF16),
            jax.ShapeDtypeStruct((B, ATTN_WIDTH, S), BF16),
            jax.ShapeDtypeStruct((B, S, POOL_WIDTH), BF16),
        ],
        scratch_shapes=[pltpu.VMEM((1, LANES), F32), pltpu.VMEM((POOL_HALO + tm, POOL_WIDTH), F32)],
        compiler_params=pltpu.CompilerParams(
            dimension_semantics=("arbitrary", "arbitrary"), vmem_limit_bytes=VMEM_LIMIT),
        name="premix",
    )(x, mod, g, wqkv, wf, bf, wu, sel)


def _forget_routing():
    sel = np.zeros((3, LANES, LANES), np.float32)
    for i in range(3):
        for h in range(ATTN_HEADS):
            sel[i, h, 3 * h + i] = 1.0
    return jnp.asarray(sel, BF16)


def _attn_kernel(qt_in_ref, kaug_ref, vt_ref, o_ref, qt_ref, m_ref, acc_ref, s_ref, mb_ref, *, tq, n_pairs):
    tk = tq
    qi = pl.program_id(1)
    n_heads = 2 * n_pairs
    @pl.when(qi == 0)
    def _():
        r128 = lax.broadcasted_iota(jnp.int32, (LANES, tq), 0)
        for h in range(n_heads):
            qt_ref[h, 0:LANES, :] = jnp.zeros((LANES, tq), BF16)
            qt_ref[h, LANES:AUG, :] = ((r128 >= 3 * h) & (r128 < 3 * h + 3)).astype(BF16)

    for h in range(n_heads):
        r0 = HEAD_DIM * (h % 2)
        qt_ref[h, r0:r0 + HEAD_DIM, :] = qt_in_ref[0, h * HEAD_DIM:(h + 1) * HEAD_DIM, :]
    m_ref[...] = jnp.full(m_ref.shape, NEG_INF, F32)
    acc_ref[...] = jnp.zeros(acc_ref.shape, F32)
    ones = jnp.ones((DENOM_ROWS, tk), BF16)

    def step(new=None, cur=None):
        if new is not None:
            jn, slot_n, masked = new
            k0n = pl.multiple_of(jn * tk, tk)
            f_terms = kaug_ref[0, pl.ds(k0n, tk), n_pairs * LANES:(n_pairs + 1) * LANES]
        if cur is not None:
            jc, slot_c = cur
            k0c = pl.multiple_of(jc * tk, tk)
        for h in range(n_heads):
            if new is not None:
                k_pair = kaug_ref[0, pl.ds(k0n, tk), (h // 2) * LANES:(h // 2 + 1) * LANES]
                s = jnp.dot(jnp.concatenate([k_pair, f_terms], axis=1), qt_ref[h],
                            preferred_element_type=F32)
                if masked:
                    key = lax.broadcasted_iota(jnp.int32, (tk, tq), 0)
                    qry = lax.broadcasted_iota(jnp.int32, (tk, tq), 1)
                    s = jnp.where(key <= qry, s, NEG_INF)
                s_ref[slot_n, h] = s
                mb_ref[slot_n, h] = jnp.max(s, axis=0, keepdims=True)
            if cur is not None:
                m_prev = m_ref[h]
                m_new = jnp.maximum(m_prev, mb_ref[slot_c, h])
                pt = jnp.exp2(s_ref[slot_c, h] - m_new).astype(BF16)
                alpha = jnp.exp2(m_prev - m_new)
                vtb = vt_ref[0, pl.ds(h * HEAD_DIM, HEAD_DIM), pl.ds(k0c, tk)]
                lhs = jnp.concatenate([vtb, ones], axis=0)
                acc_ref[h] = alpha * acc_ref[h] + jnp.dot(lhs, pt, preferred_element_type=F32)
                m_ref[h] = m_new

    @pl.when(qi == 0)
    def _():
        step(new=(0, 0, True))

    @pl.when(qi > 0)
    def _():
        step(new=(0, 0, False))

    def body(jj, c):
        j = ATTN_UNROLL * jj
        for k in range(1, ATTN_UNROLL + 1):
            step(new=(j + k, k % 2, False), cur=(j + k - 1, (k - 1) % 2))
        return c

    n_loops = jnp.maximum(qi - 1, 0) // ATTN_UNROLL
    lax.fori_loop(0, n_loops, body, 0)
    j0 = ATTN_UNROLL * n_loops
    rem = qi - j0

    @pl.when(rem == 0)
    def _():
        step(cur=(0, 0))

    for r in range(1, ATTN_UNROLL + 1):
        @pl.when(rem == r)
        def _(r=r):
            for k in range(1, r + 1):
                step(new=(j0 + k, k % 2, k == r), cur=(j0 + k - 1, (k - 1) % 2))
            step(cur=(qi, r % 2))

    for p in range(n_pairs):
        outs = []
        for hh in range(2):
            a = acc_ref[2 * p + hh]
            outs.append(a[:HEAD_DIM, :] / a[HEAD_DIM:HEAD_DIM + 1, :])
        o_ref[0, :, p * LANES:(p + 1) * LANES] = jnp.concatenate(outs, axis=0).T.astype(BF16)


def _attention(qt, kaug, vt, *, tq):
    B, W, S = qt.shape
    n_pairs = W // LANES
    return pl.pallas_call(
        functools.partial(_attn_kernel, tq=tq, n_pairs=n_pairs),
        grid=(B, S // tq),
        in_specs=[pl.BlockSpec((1, W, tq), lambda b, i: (b, 0, i)),
                  pl.BlockSpec((1, S, (n_pairs + 1) * LANES), lambda b, i: (b, 0, 0)),
                  pl.BlockSpec((1, W, S), lambda b, i: (b, 0, 0))],
        out_specs=pl.BlockSpec((1, tq, W), lambda b, i: (b, i, 0)),
        out_shape=jax.ShapeDtypeStruct((B, S, W), BF16),
        scratch_shapes=[pltpu.VMEM((2 * n_pairs, AUG, tq), BF16),
                        pltpu.VMEM((2 * n_pairs, 1, tq), F32),
                        pltpu.VMEM((2 * n_pairs, HEAD_DIM + DENOM_ROWS, tq), F32),
                        pltpu.VMEM((2, 2 * n_pairs, tq, tq), F32),
                        pltpu.VMEM((2, 2 * n_pairs, 1, tq), F32)],
        compiler_params=pltpu.CompilerParams(
            dimension_semantics=("arbitrary", "arbitrary"), vmem_limit_bytes=VMEM_LIMIT),
        name="fox_attention",
    )(qt, kaug, vt)


def _postmix_kernel(attn_ref, diff_ref, x_ref, mod_ref, wpool_ref, pscale_ref, wout_ref, g_ref,
                    wr_hi_ref, wr_lo_ref, br_ref, before_ref,
                    x1_ref, xs_ref, ids_ref, wts_ref, pos_ref, *, tm):
    pooled = []
    for g in range(len(POOL_WINDOWS)):
        c0 = g * POOL_GROUP_DIM
        pooled.append(jnp.dot(diff_ref[0, :, c0:c0 + POOL_GROUP_DIM], wpool_ref[g], preferred_element_type=F32))
    pool_out = (jnp.concatenate(pooled, axis=1) * pscale_ref[...]).astype(BF16)
    cat = jnp.concatenate([attn_ref[0], pool_out], axis=1)
    mix = jnp.dot(cat, wout_ref[...], preferred_element_type=F32)
    x1 = x_ref[0] + mod_ref[0, 2:3, :] * mix
    x1_ref[0] = x1

    h = _rms_modulate(x1, g_ref[...], mod_ref[0, 3:4, :], mod_ref[0, 4:5, :])

    h_hi = h.astype(BF16)
    h_lo = (h - h_hi.astype(F32)).astype(BF16)
    logits = (_nt_dot(wr_hi_ref[...], h_hi) + _nt_dot(wr_lo_ref[...], h_hi) + _nt_dot(wr_hi_ref[...], h_lo)
              + br_ref[...])
    sub = lax.broadcasted_iota(jnp.int32, (SUBLANES, tm), 0)
    lg = jnp.where(sub < N_EXPERT_GROUPS, logits[0:SUBLANES, :], NEG_INF)
    g_max = jnp.max(lg, axis=0, keepdims=True)
    top_p = 1.0 / jnp.sum(jnp.exp(lg - g_max), axis=0, keepdims=True)
    top_g = jnp.min(jnp.where(lg == g_max, sub, SUBLANES), axis=0, keepdims=True)
    le = logits[SUBLANES:2 * SUBLANES, :]
    for g in range(1, N_EXPERT_GROUPS):
        le = jnp.where(top_g == g, logits[(g + 1) * SUBLANES:(g + 2) * SUBLANES, :], le)
    v1 = jnp.max(le, axis=0, keepdims=True)
    i1 = jnp.min(jnp.where(le == v1, sub, SUBLANES), axis=0, keepdims=True)
    le2 = jnp.where(sub == i1, NEG_INF, le)
    v2 = jnp.max(le2, axis=0, keepdims=True)
    i2 = jnp.min(jnp.where(le2 == v2, sub, SUBLANES), axis=0, keepdims=True)
    e2 = jnp.exp(v2 - v1)
    w1 = top_p / (1.0 + e2)
    id0 = top_g * EXPERTS_PER_GROUP + i1
    id1 = top_g * EXPERTS_PER_GROUP + i2
    ids_ref[...] = jnp.concatenate([id0, id1], axis=0)
    wts_ref[...] = jnp.concatenate([w1, w1 * e2], axis=0)

    sub_e = lax.broadcasted_iota(jnp.int32, (N_EXPERTS, tm), 0)
    onehot = jnp.concatenate([sub_e == id0, sub_e == id1], axis=1)
    oh_f = onehot.astype(F32)
    rank = jnp.dot(onehot.astype(BF16), before_ref[...], preferred_element_type=F32)
    chunks = jnp.floor((jnp.sum(oh_f, axis=1, keepdims=True) + (CHUNK - 1.0)) * (1.0 / CHUNK))
    er = lax.broadcasted_iota(jnp.int32, (N_EXPERTS, N_EXPERTS), 0)
    ec = lax.broadcasted_iota(jnp.int32, (N_EXPERTS, N_EXPERTS), 1)
    first_chunk = jnp.dot((er > ec).astype(BF16), jnp.broadcast_to(chunks, (N_EXPERTS, LANES)).astype(BF16),
                          preferred_element_type=F32)[:, 0:1]
    pos = jnp.sum(oh_f * (rank + CHUNK * first_chunk), axis=0, keepdims=True)
    pos0 = pos[:, :tm]
    pos1 = pos[:, tm:]
    pos_ref[...] = jnp.concatenate([pos0, pos1], axis=0)
    r_iota = lax.broadcasted_iota(jnp.int32, (xs_ref.shape[0], tm), 0)
    perm = ((r_iota == pos0.astype(jnp.int32)) | (r_iota == pos1.astype(jnp.int32))).astype(BF16)
    xs_ref[...] = jnp.dot(perm, h_hi, preferred_element_type=F32).astype(BF16)


def _postmix(attn, diff, x, mod, wpool, pscale, wout, g, wr_hi, wr_lo, br, *, tm):
    B, S, D = x.shape
    T = B * S
    nst = S // tm
    rows = _sorted_rows(tm)
    before = jnp.asarray(np.triu(np.ones((TOP_K * tm, TOP_K * tm), np.float32), k=1), BF16)
    row_spec = lambda w: pl.BlockSpec((1, tm, w), lambda b, s: (b, s, 0))
    const = lambda a: pl.BlockSpec(a.shape, lambda b, s: (0,) * a.ndim)
    tok_spec = pl.BlockSpec((TOP_K, tm), lambda b, s: (0, b * nst + s))
    return pl.pallas_call(
        functools.partial(_postmix_kernel, tm=tm),
        grid=(B, nst),
        in_specs=[row_spec(ATTN_WIDTH), row_spec(POOL_WIDTH), row_spec(D),
                  pl.BlockSpec((1, N_MOD, D), lambda b, s: (b, 0, 0)),
                  const(wpool), const(pscale), const(wout), const(g), const(wr_hi), const(wr_lo), const(br),
                  const(before)],
        out_specs=[row_spec(D),
                   pl.BlockSpec((rows, D), lambda b, s: (b * nst + s, 0)),
                   tok_spec, tok_spec, tok_spec],
        out_shape=[
            jax.ShapeDtypeStruct((B, S, D), F32),
            jax.ShapeDtypeStruct((B * nst * rows, D), BF16),
            jax.ShapeDtypeStruct((TOP_K, T), jnp.int32),
            jax.ShapeDtypeStruct((TOP_K, T), F32),
            jax.ShapeDtypeStruct((TOP_K, T), F32),
        ],
        compiler_params=pltpu.CompilerParams(
            dimension_semantics=("arbitrary", "arbitrary"), vmem_limit_bytes=VMEM_LIMIT),
        name="postmix_router",
    )(attn, diff, x, mod, wpool, pscale, wout, g, wr_hi, wr_lo, br, before)


def _chunk_copy(src_hbm, src_chunk, dst, dst_chunk, sem):
    return pltpu.make_async_copy(src_hbm.at[src_chunk], dst.at[dst_chunk], sem)


def _gather_chunks(table_ref, tile, n_chunks, src_hbm, dst, sem):
    for c in range(n_chunks):
        _chunk_copy(src_hbm, table_ref[tile * n_chunks + c], dst, c, sem).start(priority=c % 2)


def _gather_wait(src_hbm, dst, sem):
    pltpu.make_async_copy(src_hbm.at[pl.ds(0, dst.shape[0])], dst, sem).wait()


def _expert_kernel(te_ref, tv_ref, src_ref, xs_hbm, wg_ref, wu_ref, wd_ref, o_ref, buf, sem, wg_b, wu_b, wd_b,
                   *, tm, nt):
    i = pl.program_id(0)
    n_chunks = tm // CHUNK
    slot = i % 2

    @pl.when(jnp.logical_and(i == 0, tv_ref[0] == 1))
    def _():
        _gather_chunks(src_ref, 0, n_chunks, xs_hbm, buf.at[0], sem.at[0])

    @pl.when(jnp.logical_or(i == 0, te_ref[i] != te_ref[jnp.maximum(i - 1, 0)]))
    def _():
        wg_b[...] = wg_ref[0].astype(BF16)
        wu_b[...] = wu_ref[0].astype(BF16)
        wd_b[...] = wd_ref[0].astype(BF16)

    @pl.when(tv_ref[i] == 1)
    def _():
        @pl.when(jnp.logical_and(i + 1 < nt, tv_ref[jnp.minimum(i + 1, nt - 1)] == 1))
        def _():
            _gather_chunks(src_ref, i + 1, n_chunks, xs_hbm, buf.at[1 - slot], sem.at[1 - slot])

        _gather_wait(xs_hbm, buf.at[slot], sem.at[slot])
        half = tm // 2
        gate_up = []
        for r0 in (0, half):
            x = buf[slot, r0 // CHUNK:(r0 + half) // CHUNK].reshape(half, buf.shape[-1])
            gate_up.append((jnp.dot(x, wg_b[...], preferred_element_type=F32),
                            jnp.dot(x, wu_b[...], preferred_element_type=F32)))
        for r0, (a, b) in zip((0, half), gate_up):
            act = (_silu(a) * b).astype(BF16)
            o_ref[r0:r0 + half, :] = jnp.dot(act, wd_b[...], preferred_element_type=F32).astype(BF16)

    @pl.when(tv_ref[i] == 0)
    def _():
        o_ref[...] = jnp.zeros(o_ref.shape, BF16)


def _experts(tile_expert, tile_valid, chunk_src, xs, wg, wu, wd, *, tm, layer):
    nt = tile_expert.shape[0]
    _, D, Fe = wg.shape
    expert = lambda i, te: layer * N_EXPERTS + te[i]
    grid_spec = pltpu.PrefetchScalarGridSpec(
        num_scalar_prefetch=3,
        grid=(nt,),
        in_specs=[
            pl.BlockSpec(memory_space=pl.ANY),
            pl.BlockSpec((1, D, Fe), lambda i, te, tv, cs: (expert(i, te), 0, 0)),
            pl.BlockSpec((1, D, Fe), lambda i, te, tv, cs: (expert(i, te), 0, 0)),
            pl.BlockSpec((1, Fe, D), lambda i, te, tv, cs: (expert(i, te), 0, 0)),
        ],
        out_specs=pl.BlockSpec((tm, D), lambda i, te, tv, cs: (i, 0)),
        scratch_shapes=[pltpu.VMEM((2, tm // CHUNK, CHUNK, D), BF16), pltpu.SemaphoreType.DMA((2,)),
                        pltpu.VMEM((D, Fe), BF16), pltpu.VMEM((D, Fe), BF16), pltpu.VMEM((Fe, D), BF16)],
    )
    return pl.pallas_call(
        functools.partial(_expert_kernel, tm=tm, nt=nt),
        grid_spec=grid_spec,
        out_shape=jax.ShapeDtypeStruct((nt * tm, D), BF16),
        compiler_params=pltpu.CompilerParams(dimension_semantics=("arbitrary",), vmem_limit_bytes=VMEM_LIMIT),
        name="moe_experts",
    )(tile_expert, tile_valid, chunk_src, xs.reshape(-1, CHUNK, D), wg, wu, wd)


def _combine_kernel(npair_ref, dst_ref, o_hbm, x1_ref, pos_ref, wts_ref, mod_ref, *rest, tm, nt, final):
    if final:
        gf_ref, out_ref, buf, sem = rest
    else:
        out_ref, buf, sem = rest
    i = pl.program_id(0)
    max_chunks = buf.shape[1]
    rows = max_chunks * CHUNK
    slot = i % 2

    def gather(t, s):
        def body(p, carry):
            for k in range(2):
                c = 2 * p + k
                _chunk_copy(o_hbm, dst_ref[t * max_chunks + c], buf.at[s], c, sem.at[s]).start(priority=k)
            return carry

        lax.fori_loop(0, npair_ref[t], body, 0)

    def gather_wait(t, s):
        def body(p, carry):
            for k in range(2):
                _chunk_copy(o_hbm, 0, buf.at[s], 2 * p + k, sem.at[s]).wait()
            return carry

        lax.fori_loop(0, npair_ref[t], body, 0)

    @pl.when(i == 0)
    def _():
        buf[...] = jnp.zeros(buf.shape, BF16)
        gather(0, 0)

    @pl.when(i + 1 < nt)
    def _():
        gather(jnp.minimum(i + 1, nt - 1), 1 - slot)

    gather_wait(i, slot)
    rep = lambda r: jnp.broadcast_to(r, (LANES, tm)).T
    p0, p1 = rep(pos_ref[0:1, :]), rep(pos_ref[1:2, :])
    w0, w1 = rep(wts_ref[0:1, :]), rep(wts_ref[1:2, :])
    lane = lax.broadcasted_iota(jnp.int32, (tm, LANES), 1).astype(F32)
    cols = []
    for c in range(rows // LANES):
        r = lane + float(c * LANES)
        cols.append((jnp.where(p0 == r, w0, 0.0) + jnp.where(p1 == r, w1, 0.0)).astype(BF16))
    comb = jnp.concatenate(cols, axis=1)
    y = jnp.dot(comb, buf[slot].reshape(rows, buf.shape[-1]), preferred_element_type=F32)
    x2 = x1_ref[0] + mod_ref[0, 5:6, :] * y
    if final:
        ms = jnp.mean(x2 * x2, axis=-1, keepdims=True)
        x2 = x2 * lax.rsqrt(ms + EPS) * gf_ref[...]
    out_ref[0] = x2


def _combine(n_chunks, chunk_dst, o_sorted, x1, pos, wts, mod, g_final, *, tm):
    B, S, D = x1.shape
    nst = S // tm
    nt = B * nst
    rows = _sorted_rows(tm)
    final = g_final is not None
    tok_spec = pl.BlockSpec((TOP_K, tm), lambda i, np_, cd: (0, i))
    in_specs = [
        pl.BlockSpec(memory_space=pl.ANY),
        pl.BlockSpec((1, tm, D), lambda i, np_, cd: (i // nst, i % nst, 0)),
        tok_spec, tok_spec,
        pl.BlockSpec((1, N_MOD, D), lambda i, np_, cd: (i // nst, 0, 0)),
    ]
    args = [o_sorted.reshape(-1, CHUNK, D), x1, pos, wts, mod]
    if final:
        in_specs.append(pl.BlockSpec((1, D), lambda i, np_, cd: (0, 0)))
        args.append(g_final)
    grid_spec = pltpu.PrefetchScalarGridSpec(
        num_scalar_prefetch=2,
        grid=(nt,),
        in_specs=in_specs,
        out_specs=pl.BlockSpec((1, tm, D), lambda i, np_, cd: (i // nst, i % nst, 0)),
        scratch_shapes=[pltpu.VMEM((2, rows // CHUNK, CHUNK, D), BF16), pltpu.SemaphoreType.DMA((2,))],
    )
    return pl.pallas_call(
        functools.partial(_combine_kernel, tm=tm, nt=nt, final=final),
        grid_spec=grid_spec,
        out_shape=jax.ShapeDtypeStruct((B, S, D), F32),
        compiler_params=pltpu.CompilerParams(dimension_semantics=("arbitrary",), vmem_limit_bytes=VMEM_LIMIT),
        name="moe_combine",
    )((n_chunks + 1) // 2, chunk_dst, *args)


def _dispatch_tables(ids, *, tm, tm_e, nt_e):
    T = ids.shape[1]
    nts = T // tm
    rows = _sorted_rows(tm)
    max_chunks = rows // CHUNK
    cpt = tm_e // CHUNK
    experts = jnp.arange(N_EXPERTS, dtype=jnp.int32)
    onehot = (ids.reshape(TOP_K, nts, tm)[..., None] == experts).astype(jnp.int32)
    seg_chunks = (jnp.sum(onehot, axis=(0, 2)) + CHUNK - 1) // CHUNK
    local_first = jnp.cumsum(seg_chunks, axis=1) - seg_chunks
    n_chunks = jnp.sum(seg_chunks, axis=1)
    expert_chunks = jnp.sum(seg_chunks, axis=0)
    region = ((expert_chunks + cpt - 1) // cpt) * cpt
    region_end = jnp.cumsum(region)
    seg_first = (region_end - region)[None, :] + jnp.cumsum(seg_chunks, axis=0) - seg_chunks
    ci = jnp.arange(max_chunks, dtype=jnp.int32)
    in_seg = (ci[None, :, None] >= local_first[:, None, :]) & (ci[None, :, None] < (local_first + seg_chunks)[:, None, :])
    gchunk = jnp.sum(in_seg * (seg_first - local_first)[:, None, :], axis=2) + ci[None, :]
    used = ci[None, :] < n_chunks[:, None]
    chunk_dst = jnp.where(used, gchunk, 0).reshape(-1).astype(jnp.int32)
    n_global = nt_e * cpt
    local_chunk = jnp.arange(nts, dtype=jnp.int32)[:, None] * max_chunks + ci[None, :]
    zero_chunk = max_chunks - 1
    chunk_src = jnp.full((n_global,), zero_chunk, jnp.int32).at[
        jnp.where(used, gchunk, n_global).reshape(-1)].set(local_chunk.reshape(-1), mode="drop")
    tile_start = jnp.arange(nt_e, dtype=jnp.int32) * cpt
    tile_expert = jnp.minimum(jnp.sum((tile_start[:, None] >= region_end[None, :]).astype(jnp.int32), axis=1),
                              N_EXPERTS - 1)
    tile_valid = (tile_start < region_end[-1]).astype(jnp.int32)
    return n_chunks.astype(jnp.int32), chunk_dst, chunk_src, tile_expert, tile_valid


def kernel(x, c, norm_mix_g, norm_ffn_g, norm_final_g, w_ada, b_ada, w_in, b_fgate, w_pool, pool_scale, w_out,
           w_router_group, b_router_group, w_router_expert, b_router_expert, w_expert_gate, w_expert_up,
           w_expert_down):
    B, S, D = x.shape
    L = w_ada.shape[0]
    T = B * S
    tm_mix = min(512, S)
    tq = min(256, S)
    tm_e = 512
    chunks_per_tile = tm_e // CHUNK
    max_used = (T // tm_mix) * (_sorted_rows(tm_mix) // CHUNK - 1) + N_EXPERTS * (chunks_per_tile - 1)
    nt_e = -(-max_used // chunks_per_tile)

    mod_all = _ada_modulation(c, w_ada, b_ada).reshape(L, B, N_MOD, D)
    Fe = w_expert_gate.shape[-1]
    wg_all = w_expert_gate.reshape(L * N_EXPERTS, D, Fe)
    wu_all = w_expert_up.reshape(L * N_EXPERTS, D, Fe)
    wd_all = w_expert_down.reshape(L * N_EXPERTS, Fe, D)
    sel = _forget_routing()
    for l in range(L):
        mod = mod_all[l]
        w_in_l = w_in[l]
        wqkv = w_in_l[:, :3 * ATTN_WIDTH].astype(BF16)
        wf = jnp.pad(w_in_l[:, 3 * ATTN_WIDTH:3 * ATTN_WIDTH + ATTN_HEADS], ((0, 0), (0, LANES - ATTN_HEADS))).astype(BF16)
        bf = jnp.pad(b_fgate[l].astype(F32), (0, LANES - ATTN_HEADS)).reshape(1, LANES)
        wu = w_in_l[:, 3 * ATTN_WIDTH + ATTN_HEADS:].astype(BF16)
        qt, kaug, vt, diff = _premix(x, mod, norm_mix_g[l].reshape(1, D), wqkv, wf, bf, wu, sel, tm=tm_mix)
        attn = _attention(qt, kaug, vt, tq=tq)

        wr = jnp.concatenate([
            jnp.pad(w_router_group[l].T, ((0, SUBLANES - N_EXPERT_GROUPS), (0, 0))),
            w_router_expert[l].transpose(0, 2, 1).reshape(N_EXPERTS, D)], axis=0)
        wr_hi = wr.astype(BF16)
        wr_lo = (wr - wr_hi.astype(F32)).astype(BF16)
        br = jnp.concatenate([jnp.pad(b_router_group[l], (0, SUBLANES - N_EXPERT_GROUPS)),
                              b_router_expert[l].reshape(N_EXPERTS)]).reshape(ROUTER_ROWS, 1).astype(F32)
        x1, xs, ids, wts, pos = _postmix(attn, diff, x, mod, w_pool[l].astype(BF16),
                                         pool_scale[l].reshape(1, POOL_WIDTH), w_out[l].astype(BF16),
                                         norm_ffn_g[l].reshape(1, D), wr_hi, wr_lo, br, tm=tm_mix)

        n_chunks, chunk_dst, chunk_src, tile_expert, tile_valid = _dispatch_tables(ids, tm=tm_mix, tm_e=tm_e, nt_e=nt_e)
        o_sorted = _experts(tile_expert, tile_valid, chunk_src, xs, wg_all, wu_all, wd_all, tm=tm_e, layer=l)
        g_final = norm_final_g.reshape(1, D) if l == L - 1 else None
        x = _combine(n_chunks, chunk_dst, o_sorted, x1, pos, wts, mod, g_final, tm=tm_mix)
    return x
```

```python
import functools

import jax
import jax.numpy as jnp
import numpy as np
from jax import lax
from jax.experimental import pallas as pl
from jax.experimental.pallas import tpu as pltpu

ATTN_HEADS = 8
HEAD_DIM = 64
ATTN_WIDTH = ATTN_HEADS * HEAD_DIM
POOL_WINDOWS = (2, 4, 8, 16)
POOL_GROUP_DIM = 128
POOL_WIDTH = POOL_GROUP_DIM * len(POOL_WINDOWS)
POOL_HALO = 16
N_EXPERT_GROUPS = 4
EXPERTS_PER_GROUP = 8
N_EXPERTS = N_EXPERT_GROUPS * EXPERTS_PER_GROUP
N_MOD = 6
EPS = 1e-6
NEG_INF = -1e30
LOG2E = 1.4426950408889634

LANES = 128
SUBLANES = 8
AUG = 2 * LANES
DENOM_ROWS = 16
ATTN_UNROLL = 4
ROUTER_ROWS = 40
CHUNK = 16
TOP_K = 2
VMEM_LIMIT = 48 * 1024 * 1024


def _sorted_rows(tm):
    worst = TOP_K * tm + N_EXPERTS * (CHUNK - 1)
    return (worst // LANES + 1) * LANES

F32 = jnp.float32
BF16 = jnp.bfloat16


def _silu(a):
    return a * jax.nn.sigmoid(a)


def _nt_dot(a, b):
    return lax.dot_general(a, b, (((1,), (1,)), ((), ())), preferred_element_type=F32)


def _split3(a):
    t0 = a.astype(BF16)
    r1 = a - t0.astype(F32)
    t1 = r1.astype(BF16)
    t2 = (r1 - t1.astype(F32)).astype(BF16)
    return t0, t1, t2


def _rms_modulate(x, g, shift, scale):
    ms = jnp.mean(x * x, axis=-1, keepdims=True)
    y = x * lax.rsqrt(ms + EPS) * g
    return y * (1.0 + scale) + shift


def _ada_kernel(c_ref, w_ref, b_ref, o_ref):
    ca = _silu(c_ref[...])
    o_ref[0] = jnp.dot(ca, w_ref[0], precision=lax.Precision.HIGHEST, preferred_element_type=F32) + b_ref[0]


def _ada_modulation(c, w_ada, b_ada):
    L, D, W = w_ada.shape
    B = c.shape[0]
    tn = W // 4
    return pl.pallas_call(
        _ada_kernel,
        grid=(L, W // tn),
        in_specs=[
            pl.BlockSpec((B, D), lambda l, n: (0, 0)),
            pl.BlockSpec((1, D, tn), lambda l, n: (l, 0, n)),
            pl.BlockSpec((1, 1, tn), lambda l, n: (l, 0, n)),
        ],
        out_specs=pl.BlockSpec((1, B, tn), lambda l, n: (l, 0, n)),
        out_shape=jax.ShapeDtypeStruct((L, B, W), F32),
        compiler_params=pltpu.CompilerParams(vmem_limit_bytes=VMEM_LIMIT),
        name="ada_modulation",
    )(c, w_ada, b_ada.reshape(L, 1, W))


def _premix_kernel(x_ref, mod_ref, g_ref, wqkv_ref, wf_ref, bf_ref, wu_ref, sel_ref,
                   qt_ref, kaug_ref, vt_ref, diff_ref, carry_ref, ubuf_ref, *, tm):
    _premix_compute(x_ref[0], pl.program_id(1), mod_ref, g_ref, wqkv_ref, wf_ref, bf_ref, wu_ref, sel_ref,
                    qt_ref, kaug_ref, vt_ref, diff_ref, carry_ref, ubuf_ref, tm=tm)


def _premix_compute(x, si, mod_ref, g_ref, wqkv_ref, wf_ref, bf_ref, wu_ref, sel_ref,
                    qt_ref, kaug_ref, vt_ref, diff_ref, carry_ref, ubuf_ref, *, tm):
    @pl.when(si == 0)
    def _():
        carry_ref[...] = jnp.zeros_like(carry_ref)
        ubuf_ref[0:POOL_HALO, :] = jnp.zeros((POOL_HALO, POOL_WIDTH), F32)

    h = _rms_modulate(x, g_ref[...], mod_ref[0, 0:1, :], mod_ref[0, 1:2, :]).astype(BF16)
    u = jnp.dot(h, wu_ref[...], preferred_element_type=F32)
    fl = jnp.dot(h, wf_ref[...], preferred_element_type=F32) + bf_ref[...]
    qkv = jnp.dot(h, wqkv_ref[...], preferred_element_type=F32)

    ubuf_ref[POOL_HALO:POOL_HALO + tm, :] = u
    pos = si * tm + lax.broadcasted_iota(jnp.int32, (tm, POOL_GROUP_DIM), 0)
    diffs = []
    for g, w in enumerate(POOL_WINDOWS):
        c0 = g * POOL_GROUP_DIM
        ug = u[:, c0:c0 + POOL_GROUP_DIM]
        acc = ug
        for j in range(1, w):
            acc = acc + ubuf_ref[POOL_HALO - j:POOL_HALO - j + tm, c0:c0 + POOL_GROUP_DIM]
        cnt = jnp.minimum(pos + 1, w).astype(F32)
        diffs.append((acc / cnt - ug).astype(BF16))
    diff_ref[0] = jnp.concatenate(diffs, axis=1)
    ubuf_ref[0:POOL_HALO, :] = u[tm - POOL_HALO:, :]

    lf = jnp.minimum(fl, 0.0) - jnp.log1p(jnp.exp(-jnp.abs(fl)))
    row = lax.broadcasted_iota(jnp.int32, (tm, tm), 0)
    col = lax.broadcasted_iota(jnp.int32, (tm, tm), 1)
    tri = (row >= col).astype(BF16)
    cs = None
    for term in _split3(lf):
        d = jnp.dot(tri, term, preferred_element_type=F32)
        cs = d if cs is None else cs + d
    f_cum = cs + carry_ref[...]
    carry_ref[...] = f_cum[tm - 1:tm, :]

    aug = None
    for i, term in enumerate(_split3(-LOG2E * f_cum)):
        d = jnp.dot(term, sel_ref[i], preferred_element_type=F32)
        aug = d if aug is None else aug + d
    kaug_ref[0] = jnp.concatenate([qkv[:, ATTN_WIDTH:2 * ATTN_WIDTH], aug], axis=1).astype(BF16)
    qt_ref[0] = (qkv[:, :ATTN_WIDTH] * (LOG2E * HEAD_DIM ** -0.5)).T.astype(BF16)
    vt_ref[0] = qkv[:, 2 * ATTN_WIDTH:].T.astype(BF16)


def _premix_specs(B, S, D, tm, consts, bs):
    def at(f):
        return lambda *idx: f(*bs(*idx))

    row_spec = lambda w: pl.BlockSpec((1, tm, w), at(lambda b, s: (b, s, 0)))
    col_spec = pl.BlockSpec((1, ATTN_WIDTH, tm), at(lambda b, s: (b, 0, s)))
    const = lambda a: pl.BlockSpec(a.shape, lambda *idx: (0,) * a.ndim)
    kaug_w = ATTN_WIDTH + LANES
    in_specs = [pl.BlockSpec((1, N_MOD, D), at(lambda b, s: (b, 0, 0)))] + [const(a) for a in consts]
    out_specs = [col_spec, row_spec(kaug_w), col_spec, row_spec(POOL_WIDTH)]
    out_shape = [
        jax.ShapeDtypeStruct((B, ATTN_WIDTH, S), BF16),
        jax.ShapeDtypeStruct((B, S, kaug_w), BF16),
        jax.ShapeDtypeStruct((B, ATTN_WIDTH, S), BF16),
        jax.ShapeDtypeStruct((B, S, POOL_WIDTH), BF16),
    ]
    scratch = [pltpu.VMEM((1, LANES), F32), pltpu.VMEM((POOL_HALO + tm, POOL_WIDTH), F32)]
    return in_specs, out_specs, out_shape, scratch


def _premix(x, mod, consts, *, tm):
    B, S, D = x.shape
    in_specs, out_specs, out_shape, scratch = _premix_specs(B, S, D, tm, consts, lambda b, s: (b, s))
    return pl.pallas_call(
        functools.partial(_premix_kernel, tm=tm),
        grid=(B, S // tm),
        in_specs=[pl.BlockSpec((1, tm, D), lambda b, s: (b, s, 0))] + in_specs,
        out_specs=out_specs,
        out_shape=out_shape,
        scratch_shapes=scratch,
        compiler_params=pltpu.CompilerParams(
            dimension_semantics=("arbitrary", "arbitrary"), vmem_limit_bytes=VMEM_LIMIT),
        name="premix",
    )(x, mod, *consts)


def _forget_routing():
    sel = np.zeros((3, LANES, LANES), np.float32)
    for i in range(3):
        for h in range(ATTN_HEADS):
            sel[i, h, 3 * h + i] = 1.0
    return jnp.asarray(sel, BF16)


def _attn_kernel(qt_in_ref, kaug_ref, vt_ref, o_ref, qt_ref, m_ref, acc_ref, s_ref, mb_ref, *, tq, n_pairs):
    tk = tq
    qi = pl.program_id(1)
    n_heads = 2 * n_pairs
    @pl.when(qi == 0)
    def _():
        r128 = lax.broadcasted_iota(jnp.int32, (LANES, tq), 0)
        for h in range(n_heads):
            qt_ref[h, 0:LANES, :] = jnp.zeros((LANES, tq), BF16)
            qt_ref[h, LANES:AUG, :] = ((r128 >= 3 * h) & (r128 < 3 * h + 3)).astype(BF16)

    for h in range(n_heads):
        r0 = HEAD_DIM * (h % 2)
        qt_ref[h, r0:r0 + HEAD_DIM, :] = qt_in_ref[0, h * HEAD_DIM:(h + 1) * HEAD_DIM, :]
    m_ref[...] = jnp.full(m_ref.shape, NEG_INF, F32)
    acc_ref[...] = jnp.zeros(acc_ref.shape, F32)
    ones = jnp.ones((DENOM_ROWS, tk), BF16)

    def step(new=None, cur=None):
        if new is not None:
            jn, slot_n, masked = new
            k0n = pl.multiple_of(jn * tk, tk)
            f_terms = kaug_ref[0, pl.ds(k0n, tk), n_pairs * LANES:(n_pairs + 1) * LANES]
        if cur is not None:
            jc, slot_c = cur
            k0c = pl.multiple_of(jc * tk, tk)
        for h in range(n_heads):
            if new is not None:
                k_pair = kaug_ref[0, pl.ds(k0n, tk), (h // 2) * LANES:(h // 2 + 1) * LANES]
                s = jnp.dot(jnp.concatenate([k_pair, f_terms], axis=1), qt_ref[h],
                            preferred_element_type=F32)
                if masked:
                    key = lax.broadcasted_iota(jnp.int32, (tk, tq), 0)
                    qry = lax.broadcasted_iota(jnp.int32, (tk, tq), 1)
                    s = jnp.where(key <= qry, s, NEG_INF)
                s_ref[slot_n, h] = s
                mb_ref[slot_n, h] = jnp.max(s, axis=0, keepdims=True)
            if cur is not None:
                m_prev = m_ref[h]
                m_new = jnp.maximum(m_prev, mb_ref[slot_c, h])
                pt = jnp.exp2(s_ref[slot_c, h] - m_new).astype(BF16)
                alpha = jnp.exp2(m_prev - m_new)
                vtb = vt_ref[0, pl.ds(h * HEAD_DIM, HEAD_DIM), pl.ds(k0c, tk)]
                lhs = jnp.concatenate([vtb, ones], axis=0)
                acc_ref[h] = alpha * acc_ref[h] + jnp.dot(lhs, pt, preferred_element_type=F32)
                m_ref[h] = m_new

    @pl.when(qi == 0)
    def _():
        step(new=(0, 0, True))

    @pl.when(qi > 0)
    def _():
        step(new=(0, 0, False))

    def body(jj, c):
        j = ATTN_UNROLL * jj
        for k in range(1, ATTN_UNROLL + 1):
            step(new=(j + k, k % 2, False), cur=(j + k - 1, (k - 1) % 2))
        return c

    n_loops = jnp.maximum(qi - 1, 0) // ATTN_UNROLL
    lax.fori_loop(0, n_loops, body, 0)
    j0 = ATTN_UNROLL * n_loops
    rem = qi - j0

    @pl.when(rem == 0)
    def _():
        step(cur=(0, 0))

    for r in range(1, ATTN_UNROLL + 1):
        @pl.when(rem == r)
        def _(r=r):
            for k in range(1, r + 1):
                step(new=(j0 + k, k % 2, k == r), cur=(j0 + k - 1, (k - 1) % 2))
            step(cur=(qi, r % 2))

    for p in range(n_pairs):
        outs = []
        for hh in range(2):
            a = acc_ref[2 * p + hh]
            outs.append(a[:HEAD_DIM, :] / a[HEAD_DIM:HEAD_DIM + 1, :])
        o_ref[0, :, p * LANES:(p + 1) * LANES] = jnp.concatenate(outs, axis=0).T.astype(BF16)


def _attention(qt, kaug, vt, *, tq):
    B, W, S = qt.shape
    n_pairs = W // LANES
    return pl.pallas_call(
        functools.partial(_attn_kernel, tq=tq, n_pairs=n_pairs),
        grid=(B, S // tq),
        in_specs=[pl.BlockSpec((1, W, tq), lambda b, i: (b, 0, i)),
                  pl.BlockSpec((1, S, (n_pairs + 1) * LANES), lambda b, i: (b, 0, 0)),
                  pl.BlockSpec((1, W, S), lambda b, i: (b, 0, 0))],
        out_specs=pl.BlockSpec((1, tq, W), lambda b, i: (b, i, 0)),
        out_shape=jax.ShapeDtypeStruct((B, S, W), BF16),
        scratch_shapes=[pltpu.VMEM((2 * n_pairs, AUG, tq), BF16),
                        pltpu.VMEM((2 * n_pairs, 1, tq), F32),
                        pltpu.VMEM((2 * n_pairs, HEAD_DIM + DENOM_ROWS, tq), F32),
                        pltpu.VMEM((2, 2 * n_pairs, tq, tq), F32),
                        pltpu.VMEM((2, 2 * n_pairs, 1, tq), F32)],
        compiler_params=pltpu.CompilerParams(
            dimension_semantics=("arbitrary", "arbitrary"), vmem_limit_bytes=VMEM_LIMIT),
        name="fox_attention",
    )(qt, kaug, vt)


def _postmix_kernel(attn_ref, diff_ref, x_ref, mod_ref, wpool_ref, pscale_ref, wout_ref, g_ref,
                    wr_hi_ref, wr_lo_ref, br_ref, before_ref,
                    x1_ref, xs_ref, ids_ref, wts_ref, pos_ref, *, tm):
    pooled = []
    for g in range(len(POOL_WINDOWS)):
        c0 = g * POOL_GROUP_DIM
        pooled.append(jnp.dot(diff_ref[0, :, c0:c0 + POOL_GROUP_DIM], wpool_ref[g], preferred_element_type=F32))
    pool_out = (jnp.concatenate(pooled, axis=1) * pscale_ref[...]).astype(BF16)
    cat = jnp.concatenate([attn_ref[0], pool_out], axis=1)
    mix = jnp.dot(cat, wout_ref[...], preferred_element_type=F32)
    x1 = x_ref[0] + mod_ref[0, 2:3, :] * mix
    x1_ref[0] = x1

    h = _rms_modulate(x1, g_ref[...], mod_ref[0, 3:4, :], mod_ref[0, 4:5, :])

    h_hi = h.astype(BF16)
    h_lo = (h - h_hi.astype(F32)).astype(BF16)
    logits = (_nt_dot(wr_hi_ref[...], h_hi) + _nt_dot(wr_lo_ref[...], h_hi) + _nt_dot(wr_hi_ref[...], h_lo)
              + br_ref[...])
    sub = lax.broadcasted_iota(jnp.int32, (SUBLANES, tm), 0)
    lg = jnp.where(sub < N_EXPERT_GROUPS, logits[0:SUBLANES, :], NEG_INF)
    g_max = jnp.max(lg, axis=0, keepdims=True)
    top_p = 1.0 / jnp.sum(jnp.exp(lg - g_max), axis=0, keepdims=True)
    top_g = jnp.min(jnp.where(lg == g_max, sub, SUBLANES), axis=0, keepdims=True)
    le = logits[SUBLANES:2 * SUBLANES, :]
    for g in range(1, N_EXPERT_GROUPS):
        le = jnp.where(top_g == g, logits[(g + 1) * SUBLANES:(g + 2) * SUBLANES, :], le)
    v1 = jnp.max(le, axis=0, keepdims=True)
    i1 = jnp.min(jnp.where(le == v1, sub, SUBLANES), axis=0, keepdims=True)
    le2 = jnp.where(sub == i1, NEG_INF, le)
    v2 = jnp.max(le2, axis=0, keepdims=True)
    i2 = jnp.min(jnp.where(le2 == v2, sub, SUBLANES), axis=0, keepdims=True)
    e2 = jnp.exp(v2 - v1)
    w1 = top_p / (1.0 + e2)
    id0 = top_g * EXPERTS_PER_GROUP + i1
    id1 = top_g * EXPERTS_PER_GROUP + i2
    ids_ref[...] = jnp.concatenate([id0, id1], axis=0)
    wts_ref[...] = jnp.concatenate([w1, w1 * e2], axis=0)

    sub_e = lax.broadcasted_iota(jnp.int32, (N_EXPERTS, tm), 0)
    onehot = jnp.concatenate([sub_e == id0, sub_e == id1], axis=1)
    oh_f = onehot.astype(F32)
    rank = jnp.dot(onehot.astype(BF16), before_ref[...], preferred_element_type=F32)
    chunks = jnp.floor((jnp.sum(oh_f, axis=1, keepdims=True) + (CHUNK - 1.0)) * (1.0 / CHUNK))
    er = lax.broadcasted_iota(jnp.int32, (N_EXPERTS, N_EXPERTS), 0)
    ec = lax.broadcasted_iota(jnp.int32, (N_EXPERTS, N_EXPERTS), 1)
    first_chunk = jnp.dot((er > ec).astype(BF16), jnp.broadcast_to(chunks, (N_EXPERTS, LANES)).astype(BF16),
                          preferred_element_type=F32)[:, 0:1]
    pos = jnp.sum(oh_f * (rank + CHUNK * first_chunk), axis=0, keepdims=True)
    pos0 = pos[:, :tm]
    pos1 = pos[:, tm:]
    pos_ref[...] = jnp.concatenate([pos0, pos1], axis=0)
    r_iota = lax.broadcasted_iota(jnp.int32, (xs_ref.shape[0], tm), 0)
    perm = ((r_iota == pos0.astype(jnp.int32)) | (r_iota == pos1.astype(jnp.int32))).astype(BF16)
    xs_ref[...] = jnp.dot(perm, h_hi, preferred_element_type=F32).astype(BF16)


def _postmix(attn, diff, x, mod, wpool, pscale, wout, g, wr_hi, wr_lo, br, *, tm):
    B, S, D = x.shape
    T = B * S
    nst = S // tm
    rows = _sorted_rows(tm)
    before = jnp.asarray(np.triu(np.ones((TOP_K * tm, TOP_K * tm), np.float32), k=1), BF16)
    row_spec = lambda w: pl.BlockSpec((1, tm, w), lambda b, s: (b, s, 0))
    const = lambda a: pl.BlockSpec(a.shape, lambda b, s: (0,) * a.ndim)
    tok_spec = pl.BlockSpec((TOP_K, tm), lambda b, s: (0, b * nst + s))
    return pl.pallas_call(
        functools.partial(_postmix_kernel, tm=tm),
        grid=(B, nst),
        in_specs=[row_spec(ATTN_WIDTH), row_spec(POOL_WIDTH), row_spec(D),
                  pl.BlockSpec((1, N_MOD, D), lambda b, s: (b, 0, 0)),
                  const(wpool), const(pscale), const(wout), const(g), const(wr_hi), const(wr_lo), const(br),
                  const(before)],
        out_specs=[row_spec(D),
                   pl.BlockSpec((rows, D), lambda b, s: (b * nst + s, 0)),
                   tok_spec, tok_spec, tok_spec],
        out_shape=[
            jax.ShapeDtypeStruct((B, S, D), F32),
            jax.ShapeDtypeStruct((B * nst * rows, D), BF16),
            jax.ShapeDtypeStruct((TOP_K, T), jnp.int32),
            jax.ShapeDtypeStruct((TOP_K, T), F32),
            jax.ShapeDtypeStruct((TOP_K, T), F32),
        ],
        compiler_params=pltpu.CompilerParams(
            dimension_semantics=("arbitrary", "arbitrary"), vmem_limit_bytes=VMEM_LIMIT),
        name="postmix_router",
    )(attn, diff, x, mod, wpool, pscale, wout, g, wr_hi, wr_lo, br, before)


def _chunk_copy(src_hbm, src_chunk, dst, dst_chunk, sem):
    return pltpu.make_async_copy(src_hbm.at[src_chunk], dst.at[dst_chunk], sem)


def _gather_chunks(table_ref, tile, n_chunks, src_hbm, dst, sem):
    for c in range(n_chunks):
        _chunk_copy(src_hbm, table_ref[tile * n_chunks + c], dst, c, sem).start(priority=c % 2)


def _gather_wait(src_hbm, dst, sem):
    pltpu.make_async_copy(src_hbm.at[pl.ds(0, dst.shape[0])], dst, sem).wait()


def _expert_kernel(te_ref, tv_ref, src_ref, xs_hbm, wg_ref, wu_ref, wd_ref, o_ref, buf, sem, wg_b, wu_b, wd_b,
                   *, tm, nt):
    i = pl.program_id(0)
    n_chunks = tm // CHUNK
    slot = i % 2

    @pl.when(jnp.logical_and(i == 0, tv_ref[0] == 1))
    def _():
        _gather_chunks(src_ref, 0, n_chunks, xs_hbm, buf.at[0], sem.at[0])

    @pl.when(jnp.logical_or(i == 0, te_ref[i] != te_ref[jnp.maximum(i - 1, 0)]))
    def _():
        wg_b[...] = wg_ref[0].astype(BF16)
        wu_b[...] = wu_ref[0].astype(BF16)
        wd_b[...] = wd_ref[0].astype(BF16)

    @pl.when(tv_ref[i] == 1)
    def _():
        @pl.when(jnp.logical_and(i + 1 < nt, tv_ref[jnp.minimum(i + 1, nt - 1)] == 1))
        def _():
            _gather_chunks(src_ref, i + 1, n_chunks, xs_hbm, buf.at[1 - slot], sem.at[1 - slot])

        _gather_wait(xs_hbm, buf.at[slot], sem.at[slot])
        half = tm // 2
        gate_up = []
        for r0 in (0, half):
            x = buf[slot, r0 // CHUNK:(r0 + half) // CHUNK].reshape(half, buf.shape[-1])
            gate_up.append((jnp.dot(x, wg_b[...], preferred_element_type=F32),
                            jnp.dot(x, wu_b[...], preferred_element_type=F32)))
        for r0, (a, b) in zip((0, half), gate_up):
            act = (_silu(a) * b).astype(BF16)
            o_ref[r0:r0 + half, :] = jnp.dot(act, wd_b[...], preferred_element_type=F32).astype(BF16)

    @pl.when(tv_ref[i] == 0)
    def _():
        o_ref[...] = jnp.zeros(o_ref.shape, BF16)


def _experts(tile_expert, tile_valid, chunk_src, xs, wg, wu, wd, *, tm, layer):
    nt = tile_expert.shape[0]
    _, D, Fe = wg.shape
    expert = lambda i, te: layer * N_EXPERTS + te[i]
    grid_spec = pltpu.PrefetchScalarGridSpec(
        num_scalar_prefetch=3,
        grid=(nt,),
        in_specs=[
            pl.BlockSpec(memory_space=pl.ANY),
            pl.BlockSpec((1, D, Fe), lambda i, te, tv, cs: (expert(i, te), 0, 0)),
            pl.BlockSpec((1, D, Fe), lambda i, te, tv, cs: (expert(i, te), 0, 0)),
            pl.BlockSpec((1, Fe, D), lambda i, te, tv, cs: (expert(i, te), 0, 0)),
        ],
        out_specs=pl.BlockSpec((tm, D), lambda i, te, tv, cs: (i, 0)),
        scratch_shapes=[pltpu.VMEM((2, tm // CHUNK, CHUNK, D), BF16), pltpu.SemaphoreType.DMA((2,)),
                        pltpu.VMEM((D, Fe), BF16), pltpu.VMEM((D, Fe), BF16), pltpu.VMEM((Fe, D), BF16)],
    )
    return pl.pallas_call(
        functools.partial(_expert_kernel, tm=tm, nt=nt),
        grid_spec=grid_spec,
        out_shape=jax.ShapeDtypeStruct((nt * tm, D), BF16),
        compiler_params=pltpu.CompilerParams(dimension_semantics=("arbitrary",), vmem_limit_bytes=VMEM_LIMIT),
        name="moe_experts",
    )(tile_expert, tile_valid, chunk_src, xs.reshape(-1, CHUNK, D), wg, wu, wd)


def _combine_final_kernel(npair_ref, dst_ref, o_hbm, x1_ref, pos_ref, wts_ref, mod_ref, gf_ref, out_ref, buf, sem,
                          *, tm, nt):
    x2 = _combine_compute(npair_ref, dst_ref, o_hbm, x1_ref, pos_ref, wts_ref, mod_ref, buf, sem, tm=tm, nt=nt)
    ms = jnp.mean(x2 * x2, axis=-1, keepdims=True)
    out_ref[0] = x2 * lax.rsqrt(ms + EPS) * gf_ref[...]


def _combine_premix_kernel(npair_ref, dst_ref, o_hbm, x1_ref, pos_ref, wts_ref, mod_ref,
                           mod_next_ref, g_ref, wqkv_ref, wf_ref, bf_ref, wu_ref, sel_ref,
                           out_ref, qt_ref, kaug_ref, vt_ref, diff_ref, buf, sem, carry_ref, ubuf_ref, *, tm, nt, nst):
    x2 = _combine_compute(npair_ref, dst_ref, o_hbm, x1_ref, pos_ref, wts_ref, mod_ref, buf, sem, tm=tm, nt=nt)
    out_ref[0] = x2
    _premix_compute(x2, pl.program_id(0) % nst, mod_next_ref, g_ref, wqkv_ref, wf_ref, bf_ref, wu_ref, sel_ref,
                    qt_ref, kaug_ref, vt_ref, diff_ref, carry_ref, ubuf_ref, tm=tm)


def _combine_compute(npair_ref, dst_ref, o_hbm, x1_ref, pos_ref, wts_ref, mod_ref, buf, sem, *, tm, nt):
    i = pl.program_id(0)
    max_chunks = buf.shape[1]
    rows = max_chunks * CHUNK
    slot = i % 2

    def gather(t, s):
        def body(p, carry):
            for k in range(2):
                c = 2 * p + k
                _chunk_copy(o_hbm, dst_ref[t * max_chunks + c], buf.at[s], c, sem.at[s]).start(priority=k)
            return carry

        lax.fori_loop(0, npair_ref[t], body, 0)

    def gather_wait(t, s):
        def body(p, carry):
            for k in range(2):
                _chunk_copy(o_hbm, 0, buf.at[s], 2 * p + k, sem.at[s]).wait()
            return carry

        lax.fori_loop(0, npair_ref[t], body, 0)

    @pl.when(i == 0)
    def _():
        buf[...] = jnp.zeros(buf.shape, BF16)
        gather(0, 0)

    @pl.when(i + 1 < nt)
    def _():
        gather(jnp.minimum(i + 1, nt - 1), 1 - slot)

    gather_wait(i, slot)
    rep = lambda r: jnp.broadcast_to(r, (LANES, tm)).T
    p0, p1 = rep(pos_ref[0:1, :]), rep(pos_ref[1:2, :])
    w0, w1 = rep(wts_ref[0:1, :]), rep(wts_ref[1:2, :])
    lane = lax.broadcasted_iota(jnp.int32, (tm, LANES), 1).astype(F32)
    cols = []
    for c in range(rows // LANES):
        r = lane + float(c * LANES)
        cols.append((jnp.where(p0 == r, w0, 0.0) + jnp.where(p1 == r, w1, 0.0)).astype(BF16))
    comb = jnp.concatenate(cols, axis=1)
    y = jnp.dot(comb, buf[slot].reshape(rows, buf.shape[-1]), preferred_element_type=F32)
    return x1_ref[0] + mod_ref[0, 5:6, :] * y


def _combine(n_chunks, chunk_dst, o_sorted, x1, pos, wts, mod, *, tm, g_final=None, next_premix=None):
    B, S, D = x1.shape
    nst = S // tm
    nt = B * nst
    rows = _sorted_rows(tm)
    bs = lambda i, np_, cd: (i // nst, i % nst)
    x_spec = pl.BlockSpec((1, tm, D), lambda i, np_, cd: (i // nst, i % nst, 0))
    tok_spec = pl.BlockSpec((TOP_K, tm), lambda i, np_, cd: (0, i))
    in_specs = [pl.BlockSpec(memory_space=pl.ANY), x_spec, tok_spec, tok_spec,
                pl.BlockSpec((1, N_MOD, D), lambda i, np_, cd: (i // nst, 0, 0))]
    args = [o_sorted.reshape(-1, CHUNK, D), x1, pos, wts, mod]
    out_specs = [x_spec]
    out_shape = [jax.ShapeDtypeStruct((B, S, D), F32)]
    scratch = [pltpu.VMEM((2, rows // CHUNK, CHUNK, D), BF16), pltpu.SemaphoreType.DMA((2,))]
    if next_premix is None:
        body = functools.partial(_combine_final_kernel, tm=tm, nt=nt)
        in_specs.append(pl.BlockSpec((1, D), lambda i, np_, cd: (0, 0)))
        args.append(g_final)
        name = "moe_combine_final"
    else:
        mod_next, consts = next_premix
        body = functools.partial(_combine_premix_kernel, tm=tm, nt=nt, nst=nst)
        p_in, p_out, p_shape, p_scratch = _premix_specs(B, S, D, tm, consts, bs)
        in_specs += p_in
        args += [mod_next, *consts]
        out_specs += p_out
        out_shape += p_shape
        scratch += p_scratch
        name = "moe_combine_premix"
    grid_spec = pltpu.PrefetchScalarGridSpec(
        num_scalar_prefetch=2, grid=(nt,), in_specs=in_specs, out_specs=out_specs, scratch_shapes=scratch)
    return pl.pallas_call(
        body,
        grid_spec=grid_spec,
        out_shape=out_shape,
        compiler_params=pltpu.CompilerParams(dimension_semantics=("arbitrary",), vmem_limit_bytes=VMEM_LIMIT),
        name=name,
    )((n_chunks + 1) // 2, chunk_dst, *args)


def _dispatch_tables(ids, *, tm, tm_e, nt_e):
    T = ids.shape[1]
    nts = T // tm
    rows = _sorted_rows(tm)
    max_chunks = rows // CHUNK
    cpt = tm_e // CHUNK
    experts = jnp.arange(N_EXPERTS, dtype=jnp.int32)
    onehot = (ids.reshape(TOP_K, nts, tm)[..., None] == experts).astype(jnp.int32)
    seg_chunks = (jnp.sum(onehot, axis=(0, 2)) + CHUNK - 1) // CHUNK
    local_first = jnp.cumsum(seg_chunks, axis=1) - seg_chunks
    n_chunks = jnp.sum(seg_chunks, axis=1)
    expert_chunks = jnp.sum(seg_chunks, axis=0)
    region = ((expert_chunks + cpt - 1) // cpt) * cpt
    region_end = jnp.cumsum(region)
    seg_first = (region_end - region)[None, :] + jnp.cumsum(seg_chunks, axis=0) - seg_chunks
    ci = jnp.arange(max_chunks, dtype=jnp.int32)
    in_seg = (ci[None, :, None] >= local_first[:, None, :]) & (ci[None, :, None] < (local_first + seg_chunks)[:, None, :])
    gchunk = jnp.sum(in_seg * (seg_first - local_first)[:, None, :], axis=2) + ci[None, :]
    used = ci[None, :] < n_chunks[:, None]
    chunk_dst = jnp.where(used, gchunk, 0).reshape(-1).astype(jnp.int32)
    n_global = nt_e * cpt
    local_chunk = jnp.arange(nts, dtype=jnp.int32)[:, None] * max_chunks + ci[None, :]
    zero_chunk = max_chunks - 1
    chunk_src = jnp.full((n_global,), zero_chunk, jnp.int32).at[
        jnp.where(used, gchunk, n_global).reshape(-1)].set(local_chunk.reshape(-1), mode="drop")
    tile_start = jnp.arange(nt_e, dtype=jnp.int32) * cpt
    tile_expert = jnp.minimum(jnp.sum((tile_start[:, None] >= region_end[None, :]).astype(jnp.int32), axis=1),
                              N_EXPERTS - 1)
    tile_valid = (tile_start < region_end[-1]).astype(jnp.int32)
    return n_chunks.astype(jnp.int32), chunk_dst, chunk_src, tile_expert, tile_valid


def kernel(x, c, norm_mix_g, norm_ffn_g, norm_final_g, w_ada, b_ada, w_in, b_fgate, w_pool, pool_scale, w_out,
           w_router_group, b_router_group, w_router_expert, b_router_expert, w_expert_gate, w_expert_up,
           w_expert_down):
    B, S, D = x.shape
    L = w_ada.shape[0]
    T = B * S
    tm_mix = min(512, S)
    tq = min(256, S)
    tm_e = 512
    chunks_per_tile = tm_e // CHUNK
    max_used = (T // tm_mix) * (_sorted_rows(tm_mix) // CHUNK - 1) + N_EXPERTS * (chunks_per_tile - 1)
    nt_e = -(-max_used // chunks_per_tile)

    mod_all = _ada_modulation(c, w_ada, b_ada).reshape(L, B, N_MOD, D)
    Fe = w_expert_gate.shape[-1]
    wg_all = w_expert_gate.reshape(L * N_EXPERTS, D, Fe)
    wu_all = w_expert_up.reshape(L * N_EXPERTS, D, Fe)
    wd_all = w_expert_down.reshape(L * N_EXPERTS, Fe, D)
    sel = _forget_routing()

    def premix_consts(l):
        w_in_l = w_in[l]
        wqkv = w_in_l[:, :3 * ATTN_WIDTH].astype(BF16)
        wf = jnp.pad(w_in_l[:, 3 * ATTN_WIDTH:3 * ATTN_WIDTH + ATTN_HEADS], ((0, 0), (0, LANES - ATTN_HEADS))).astype(BF16)
        bf = jnp.pad(b_fgate[l].astype(F32), (0, LANES - ATTN_HEADS)).reshape(1, LANES)
        wu = w_in_l[:, 3 * ATTN_WIDTH + ATTN_HEADS:].astype(BF16)
        return (norm_mix_g[l].reshape(1, D), wqkv, wf, bf, wu, sel)

    qt, kaug, vt, diff = _premix(x, mod_all[0], premix_consts(0), tm=tm_mix)
    for l in range(L):
        mod = mod_all[l]
        attn = _attention(qt, kaug, vt, tq=tq)

        wr = jnp.concatenate([
            jnp.pad(w_router_group[l].T, ((0, SUBLANES - N_EXPERT_GROUPS), (0, 0))),
            w_router_expert[l].transpose(0, 2, 1).reshape(N_EXPERTS, D)], axis=0)
        wr_hi = wr.astype(BF16)
        wr_lo = (wr - wr_hi.astype(F32)).astype(BF16)
        br = jnp.concatenate([jnp.pad(b_router_group[l], (0, SUBLANES - N_EXPERT_GROUPS)),
                              b_router_expert[l].reshape(N_EXPERTS)]).reshape(ROUTER_ROWS, 1).astype(F32)
        x1, xs, ids, wts, pos = _postmix(attn, diff, x, mod, w_pool[l].astype(BF16),
                                         pool_scale[l].reshape(1, POOL_WIDTH), w_out[l].astype(BF16),
                                         norm_ffn_g[l].reshape(1, D), wr_hi, wr_lo, br, tm=tm_mix)

        n_chunks, chunk_dst, chunk_src, tile_expert, tile_valid = _dispatch_tables(ids, tm=tm_mix, tm_e=tm_e, nt_e=nt_e)
        o_sorted = _experts(tile_expert, tile_valid, chunk_src, xs, wg_all, wu_all, wd_all, tm=tm_e, layer=l)
        if l == L - 1:
            (x,) = _combine(n_chunks, chunk_dst, o_sorted, x1, pos, wts, mod, tm=tm_mix,
                            g_final=norm_final_g.reshape(1, D))
        else:
            x, qt, kaug, vt, diff = _combine(n_chunks, chunk_dst, o_sorted, x1, pos, wts, mod, tm=tm_mix,
                                             next_premix=(mod_all[l + 1], premix_consts(l + 1)))
    return x
```

```python
import functools

import jax
import jax.numpy as jnp
import numpy as np
from jax import lax
from jax.experimental import pallas as pl
from jax.experimental.pallas import tpu as pltpu

ATTN_HEADS = 8
HEAD_DIM = 64
ATTN_WIDTH = ATTN_HEADS * HEAD_DIM
POOL_WINDOWS = (2, 4, 8, 16)
POOL_GROUP_DIM = 128
POOL_WIDTH = POOL_GROUP_DIM * len(POOL_WINDOWS)
POOL_HALO = 16
N_EXPERT_GROUPS = 4
EXPERTS_PER_GROUP = 8
N_EXPERTS = N_EXPERT_GROUPS * EXPERTS_PER_GROUP
N_MOD = 6
EPS = 1e-6
NEG_INF = -1e30
LOG2E = 1.4426950408889634

LANES = 128
SUBLANES = 8
AUG = 2 * LANES
DENOM_ROWS = 16
ATTN_UNROLL = 4
ROUTER_ROWS = 40
CHUNK = 16
TOP_K = 2
VMEM_LIMIT = 48 * 1024 * 1024


def _sorted_rows(tm):
    worst = TOP_K * tm + N_EXPERTS * (CHUNK - 1)
    return (worst // LANES + 1) * LANES

F32 = jnp.float32
BF16 = jnp.bfloat16


def _silu(a):
    return a * jax.nn.sigmoid(a)


def _nt_dot(a, b):
    return lax.dot_general(a, b, (((1,), (1,)), ((), ())), preferred_element_type=F32)


def _split3(a):
    t0 = a.astype(BF16)
    r1 = a - t0.astype(F32)
    t1 = r1.astype(BF16)
    t2 = (r1 - t1.astype(F32)).astype(BF16)
    return t0, t1, t2


def _rms_modulate(x, g, shift, scale):
    ms = jnp.mean(x * x, axis=-1, keepdims=True)
    y = x * lax.rsqrt(ms + EPS) * g
    return y * (1.0 + scale) + shift


def _ada_kernel(c_ref, w_ref, b_ref, o_ref):
    ca = _silu(c_ref[...])
    o_ref[0] = jnp.dot(ca, w_ref[0], precision=lax.Precision.HIGHEST, preferred_element_type=F32) + b_ref[0]


def _ada_modulation(c, w_ada, b_ada):
    L, D, W = w_ada.shape
    B = c.shape[0]
    tn = W // 4
    return pl.pallas_call(
        _ada_kernel,
        grid=(L, W // tn),
        in_specs=[
            pl.BlockSpec((B, D), lambda l, n: (0, 0)),
            pl.BlockSpec((1, D, tn), lambda l, n: (l, 0, n)),
            pl.BlockSpec((1, 1, tn), lambda l, n: (l, 0, n)),
        ],
        out_specs=pl.BlockSpec((1, B, tn), lambda l, n: (l, 0, n)),
        out_shape=jax.ShapeDtypeStruct((L, B, W), F32),
        compiler_params=pltpu.CompilerParams(vmem_limit_bytes=VMEM_LIMIT),
        name="ada_modulation",
    )(c, w_ada, b_ada.reshape(L, 1, W))


def _premix_kernel(x_ref, mod_ref, g_ref, wqkv_ref, wf_ref, bf_ref, wu_ref, sel_ref,
                   qt_ref, kaug_ref, vt_ref, diff_ref, carry_ref, ubuf_ref, *, tm):
    _premix_compute(x_ref[0], pl.program_id(1), mod_ref, g_ref, wqkv_ref, wf_ref, bf_ref, wu_ref, sel_ref,
                    qt_ref, kaug_ref, vt_ref, diff_ref, carry_ref, ubuf_ref, tm=tm)


def _premix_compute(x, si, mod_ref, g_ref, wqkv_ref, wf_ref, bf_ref, wu_ref, sel_ref,
                    qt_ref, kaug_ref, vt_ref, diff_ref, carry_ref, ubuf_ref, *, tm):
    @pl.when(si == 0)
    def _():
        carry_ref[...] = jnp.zeros_like(carry_ref)
        ubuf_ref[0:POOL_HALO, :] = jnp.zeros((POOL_HALO, POOL_WIDTH), F32)

    h = _rms_modulate(x, g_ref[...], mod_ref[0, 0:1, :], mod_ref[0, 1:2, :]).astype(BF16)
    u = jnp.dot(h, wu_ref[...], preferred_element_type=F32)
    fl = jnp.dot(h, wf_ref[...], preferred_element_type=F32) + bf_ref[...]
    qkv = jnp.dot(h, wqkv_ref[...], preferred_element_type=F32)

    ubuf_ref[POOL_HALO:POOL_HALO + tm, :] = u
    pos = si * tm + lax.broadcasted_iota(jnp.int32, (tm, POOL_GROUP_DIM), 0)
    diffs = []
    for g, w in enumerate(POOL_WINDOWS):
        c0 = g * POOL_GROUP_DIM
        ug = u[:, c0:c0 + POOL_GROUP_DIM]
        acc = ug
        for j in range(1, w):
            acc = acc + ubuf_ref[POOL_HALO - j:POOL_HALO - j + tm, c0:c0 + POOL_GROUP_DIM]
        cnt = jnp.minimum(pos + 1, w).astype(F32)
        diffs.append((acc / cnt - ug).astype(BF16))
    diff_ref[0] = jnp.concatenate(diffs, axis=1)
    ubuf_ref[0:POOL_HALO, :] = u[tm - POOL_HALO:, :]

    lf = jnp.minimum(fl, 0.0) - jnp.log1p(jnp.exp(-jnp.abs(fl)))
    row = lax.broadcasted_iota(jnp.int32, (tm, tm), 0)
    col = lax.broadcasted_iota(jnp.int32, (tm, tm), 1)
    tri = (row >= col).astype(BF16)
    cs = None
    for term in _split3(lf):
        d = jnp.dot(tri, term, preferred_element_type=F32)
        cs = d if cs is None else cs + d
    f_cum = cs + carry_ref[...]
    carry_ref[...] = f_cum[tm - 1:tm, :]

    aug = None
    for i, term in enumerate(_split3(-LOG2E * f_cum)):
        d = jnp.dot(term, sel_ref[i], preferred_element_type=F32)
        aug = d if aug is None else aug + d
    kaug_ref[0] = jnp.concatenate([qkv[:, ATTN_WIDTH:2 * ATTN_WIDTH], aug], axis=1).astype(BF16)
    qt_ref[0] = (qkv[:, :ATTN_WIDTH] * (LOG2E * HEAD_DIM ** -0.5)).T.astype(BF16)
    vt_ref[0] = qkv[:, 2 * ATTN_WIDTH:].T.astype(BF16)


def _premix_specs(B, S, D, tm, consts, bs):
    def at(f):
        return lambda *idx: f(*bs(*idx))

    row_spec = lambda w: pl.BlockSpec((1, tm, w), at(lambda b, s: (b, s, 0)))
    col_spec = pl.BlockSpec((1, ATTN_WIDTH, tm), at(lambda b, s: (b, 0, s)))
    const = lambda a: pl.BlockSpec(a.shape, lambda *idx: (0,) * a.ndim)
    kaug_w = ATTN_WIDTH + LANES
    in_specs = [pl.BlockSpec((1, N_MOD, D), at(lambda b, s: (b, 0, 0)))] + [const(a) for a in consts]
    out_specs = [col_spec, row_spec(kaug_w), col_spec, row_spec(POOL_WIDTH)]
    out_shape = [
        jax.ShapeDtypeStruct((B, ATTN_WIDTH, S), BF16),
        jax.ShapeDtypeStruct((B, S, kaug_w), BF16),
        jax.ShapeDtypeStruct((B, ATTN_WIDTH, S), BF16),
        jax.ShapeDtypeStruct((B, S, POOL_WIDTH), BF16),
    ]
    scratch = [pltpu.VMEM((1, LANES), F32), pltpu.VMEM((POOL_HALO + tm, POOL_WIDTH), F32)]
    return in_specs, out_specs, out_shape, scratch


def _premix(x, mod, consts, *, tm):
    B, S, D = x.shape
    in_specs, out_specs, out_shape, scratch = _premix_specs(B, S, D, tm, consts, lambda b, s: (b, s))
    return pl.pallas_call(
        functools.partial(_premix_kernel, tm=tm),
        grid=(B, S // tm),
        in_specs=[pl.BlockSpec((1, tm, D), lambda b, s: (b, s, 0))] + in_specs,
        out_specs=out_specs,
        out_shape=out_shape,
        scratch_shapes=scratch,
        compiler_params=pltpu.CompilerParams(
            dimension_semantics=("arbitrary", "arbitrary"), vmem_limit_bytes=VMEM_LIMIT),
        name="premix",
    )(x, mod, *consts)


def _forget_routing():
    sel = np.zeros((3, LANES, LANES), np.float32)
    for i in range(3):
        for h in range(ATTN_HEADS):
            sel[i, h, 3 * h + i] = 1.0
    return jnp.asarray(sel, BF16)


def _attn_kernel(qt_in_ref, kaug_ref, vt_ref, o_ref, qt_ref, m_ref, acc_ref, s_ref, mb_ref, *, tq, tk, n_pairs):
    n_diag = tq // tk
    qi = pl.program_id(1)
    n_heads = 2 * n_pairs
    @pl.when(qi == 0)
    def _():
        r128 = lax.broadcasted_iota(jnp.int32, (LANES, tq), 0)
        for h in range(n_heads):
            qt_ref[h, 0:LANES, :] = jnp.zeros((LANES, tq), BF16)
            qt_ref[h, LANES:AUG, :] = ((r128 >= 3 * h) & (r128 < 3 * h + 3)).astype(BF16)

    for h in range(n_heads):
        r0 = HEAD_DIM * (h % 2)
        qt_ref[h, r0:r0 + HEAD_DIM, :] = qt_in_ref[0, h * HEAD_DIM:(h + 1) * HEAD_DIM, :]
    m_ref[...] = jnp.full(m_ref.shape, NEG_INF, F32)
    acc_ref[...] = jnp.zeros(acc_ref.shape, F32)
    ones = jnp.ones((DENOM_ROWS, tk), BF16)

    def step(new=None, cur=None):
        if new is not None:
            jn, slot_n, masked = new
            k0n = pl.multiple_of(jn * tk, tk)
            f_terms = kaug_ref[0, pl.ds(k0n, tk), n_pairs * LANES:(n_pairs + 1) * LANES]
        if cur is not None:
            jc, slot_c = cur
            k0c = pl.multiple_of(jc * tk, tk)
        for h in range(n_heads):
            if new is not None:
                k_pair = kaug_ref[0, pl.ds(k0n, tk), (h // 2) * LANES:(h // 2 + 1) * LANES]
                s = jnp.dot(jnp.concatenate([k_pair, f_terms], axis=1), qt_ref[h],
                            preferred_element_type=F32)
                if masked:
                    key = jn * tk + lax.broadcasted_iota(jnp.int32, (tk, tq), 0)
                    qry = qi * tq + lax.broadcasted_iota(jnp.int32, (tk, tq), 1)
                    s = jnp.where(key <= qry, s, NEG_INF)
                s_ref[slot_n, h] = s
                mb_ref[slot_n, h] = jnp.max(s, axis=0, keepdims=True)
            if cur is not None:
                m_prev = m_ref[h]
                m_new = jnp.maximum(m_prev, mb_ref[slot_c, h])
                pt = jnp.exp2(s_ref[slot_c, h] - m_new).astype(BF16)
                alpha = jnp.exp2(m_prev - m_new)
                vtb = vt_ref[0, pl.ds(h * HEAD_DIM, HEAD_DIM), pl.ds(k0c, tk)]
                lhs = jnp.concatenate([vtb, ones], axis=0)
                acc_ref[h] = alpha * acc_ref[h] + jnp.dot(lhs, pt, preferred_element_type=F32)
                m_ref[h] = m_new

    n_vis = qi * n_diag
    last = n_vis + n_diag - 1

    @pl.when(qi == 0)
    def _():
        step(new=(0, 0, True))

    @pl.when(qi > 0)
    def _():
        step(new=(0, 0, False))

    def body(jj, c):
        j = ATTN_UNROLL * jj
        for k in range(1, ATTN_UNROLL + 1):
            step(new=(j + k, k % 2, False), cur=(j + k - 1, (k - 1) % 2))
        return c

    n_loops = jnp.maximum(n_vis - 1, 0) // ATTN_UNROLL
    lax.fori_loop(0, n_loops, body, 0)
    j0 = ATTN_UNROLL * n_loops
    rem = last - j0

    for r in range(0, ATTN_UNROLL + n_diag):
        @pl.when(rem == r)
        def _(r=r):
            for k in range(1, r + 1):
                step(new=(j0 + k, k % 2, k > r - n_diag), cur=(j0 + k - 1, (k - 1) % 2))
            step(cur=(last, r % 2))

    for p in range(n_pairs):
        outs = []
        for hh in range(2):
            a = acc_ref[2 * p + hh]
            outs.append(a[:HEAD_DIM, :] / a[HEAD_DIM:HEAD_DIM + 1, :])
        o_ref[0, :, p * LANES:(p + 1) * LANES] = jnp.concatenate(outs, axis=0).T.astype(BF16)


def _attention(qt, kaug, vt, *, tq, tk):
    B, W, S = qt.shape
    n_pairs = W // LANES
    assert tq % tk == 0 and S % tq == 0, (tq, tk, S)
    return pl.pallas_call(
        functools.partial(_attn_kernel, tq=tq, tk=tk, n_pairs=n_pairs),
        grid=(B, S // tq),
        in_specs=[pl.BlockSpec((1, W, tq), lambda b, i: (b, 0, i)),
                  pl.BlockSpec((1, S, (n_pairs + 1) * LANES), lambda b, i: (b, 0, 0)),
                  pl.BlockSpec((1, W, S), lambda b, i: (b, 0, 0))],
        out_specs=pl.BlockSpec((1, tq, W), lambda b, i: (b, i, 0)),
        out_shape=jax.ShapeDtypeStruct((B, S, W), BF16),
        scratch_shapes=[pltpu.VMEM((2 * n_pairs, AUG, tq), BF16),
                        pltpu.VMEM((2 * n_pairs, 1, tq), F32),
                        pltpu.VMEM((2 * n_pairs, HEAD_DIM + DENOM_ROWS, tq), F32),
                        pltpu.VMEM((2, 2 * n_pairs, tk, tq), F32),
                        pltpu.VMEM((2, 2 * n_pairs, 1, tq), F32)],
        compiler_params=pltpu.CompilerParams(
            dimension_semantics=("arbitrary", "arbitrary"), vmem_limit_bytes=VMEM_LIMIT),
        name="fox_attention",
    )(qt, kaug, vt)


def _postmix_kernel(attn_ref, diff_ref, x_ref, mod_ref, wpool_ref, pscale_ref, wout_ref, g_ref,
                    wr_hi_ref, wr_lo_ref, br_ref, before_ref,
                    x1_ref, xs_ref, ids_ref, wts_ref, pos_ref, *, tm):
    pooled = []
    for g in range(len(POOL_WINDOWS)):
        c0 = g * POOL_GROUP_DIM
        pooled.append(jnp.dot(diff_ref[0, :, c0:c0 + POOL_GROUP_DIM], wpool_ref[g], preferred_element_type=F32))
    pool_out = (jnp.concatenate(pooled, axis=1) * pscale_ref[...]).astype(BF16)
    cat = jnp.concatenate([attn_ref[0], pool_out], axis=1)
    mix = jnp.dot(cat, wout_ref[...], preferred_element_type=F32)
    x1 = x_ref[0] + mod_ref[0, 2:3, :] * mix
    x1_ref[0] = x1

    h = _rms_modulate(x1, g_ref[...], mod_ref[0, 3:4, :], mod_ref[0, 4:5, :])

    h_hi = h.astype(BF16)
    h_lo = (h - h_hi.astype(F32)).astype(BF16)
    logits = (_nt_dot(wr_hi_ref[...], h_hi) + _nt_dot(wr_lo_ref[...], h_hi) + _nt_dot(wr_hi_ref[...], h_lo)
              + br_ref[...])
    sub = lax.broadcasted_iota(jnp.int32, (SUBLANES, tm), 0)
    lg = jnp.where(sub < N_EXPERT_GROUPS, logits[0:SUBLANES, :], NEG_INF)
    g_max = jnp.max(lg, axis=0, keepdims=True)
    top_p = 1.0 / jnp.sum(jnp.exp(lg - g_max), axis=0, keepdims=True)
    top_g = jnp.min(jnp.where(lg == g_max, sub, SUBLANES), axis=0, keepdims=True)
    le = logits[SUBLANES:2 * SUBLANES, :]
    for g in range(1, N_EXPERT_GROUPS):
        le = jnp.where(top_g == g, logits[(g + 1) * SUBLANES:(g + 2) * SUBLANES, :], le)
    v1 = jnp.max(le, axis=0, keepdims=True)
    i1 = jnp.min(jnp.where(le == v1, sub, SUBLANES), axis=0, keepdims=True)
    le2 = jnp.where(sub == i1, NEG_INF, le)
    v2 = jnp.max(le2, axis=0, keepdims=True)
    i2 = jnp.min(jnp.where(le2 == v2, sub, SUBLANES), axis=0, keepdims=True)
    e2 = jnp.exp(v2 - v1)
    w1 = top_p / (1.0 + e2)
    id0 = top_g * EXPERTS_PER_GROUP + i1
    id1 = top_g * EXPERTS_PER_GROUP + i2
    ids_ref[...] = jnp.concatenate([id0, id1], axis=0)
    wts_ref[...] = jnp.concatenate([w1, w1 * e2], axis=0)

    sub_e = lax.broadcasted_iota(jnp.int32, (N_EXPERTS, tm), 0)
    onehot = jnp.concatenate([sub_e == id0, sub_e == id1], axis=1)
    oh_f = onehot.astype(F32)
    rank = jnp.dot(onehot.astype(BF16), before_ref[...], preferred_element_type=F32)
    chunks = jnp.floor((jnp.sum(oh_f, axis=1, keepdims=True) + (CHUNK - 1.0)) * (1.0 / CHUNK))
    er = lax.broadcasted_iota(jnp.int32, (N_EXPERTS, N_EXPERTS), 0)
    ec = lax.broadcasted_iota(jnp.int32, (N_EXPERTS, N_EXPERTS), 1)
    first_chunk = jnp.dot((er > ec).astype(BF16), jnp.broadcast_to(chunks, (N_EXPERTS, LANES)).astype(BF16),
                          preferred_element_type=F32)[:, 0:1]
    pos = jnp.sum(oh_f * (rank + CHUNK * first_chunk), axis=0, keepdims=True)
    pos0 = pos[:, :tm]
    pos1 = pos[:, tm:]
    pos_ref[...] = jnp.concatenate([pos0, pos1], axis=0)
    r_iota = lax.broadcasted_iota(jnp.int32, (xs_ref.shape[0], tm), 0)
    perm = ((r_iota == pos0.astype(jnp.int32)) | (r_iota == pos1.astype(jnp.int32))).astype(BF16)
    xs_ref[...] = jnp.dot(perm, h_hi, preferred_element_type=F32).astype(BF16)


def _postmix(attn, diff, x, mod, wpool, pscale, wout, g, wr_hi, wr_lo, br, *, tm):
    B, S, D = x.shape
    T = B * S
    nst = S // tm
    rows = _sorted_rows(tm)
    before = jnp.asarray(np.triu(np.ones((TOP_K * tm, TOP_K * tm), np.float32), k=1), BF16)
    row_spec = lambda w: pl.BlockSpec((1, tm, w), lambda b, s: (b, s, 0))
    const = lambda a: pl.BlockSpec(a.shape, lambda b, s: (0,) * a.ndim)
    tok_spec = pl.BlockSpec((TOP_K, tm), lambda b, s: (0, b * nst + s))
    return pl.pallas_call(
        functools.partial(_postmix_kernel, tm=tm),
        grid=(B, nst),
        in_specs=[row_spec(ATTN_WIDTH), row_spec(POOL_WIDTH), row_spec(D),
                  pl.BlockSpec((1, N_MOD, D), lambda b, s: (b, 0, 0)),
                  const(wpool), const(pscale), const(wout), const(g), const(wr_hi), const(wr_lo), const(br),
                  const(before)],
        out_specs=[row_spec(D),
                   pl.BlockSpec((rows, D), lambda b, s: (b * nst + s, 0)),
                   tok_spec, tok_spec, tok_spec],
        out_shape=[
            jax.ShapeDtypeStruct((B, S, D), F32),
            jax.ShapeDtypeStruct((B * nst * rows, D), BF16),
            jax.ShapeDtypeStruct((TOP_K, T), jnp.int32),
            jax.ShapeDtypeStruct((TOP_K, T), F32),
            jax.ShapeDtypeStruct((TOP_K, T), F32),
        ],
        compiler_params=pltpu.CompilerParams(
            dimension_semantics=("arbitrary", "arbitrary"), vmem_limit_bytes=VMEM_LIMIT),
        name="postmix_router",
    )(attn, diff, x, mod, wpool, pscale, wout, g, wr_hi, wr_lo, br, before)


def _chunk_copy(src_hbm, src_chunk, dst, dst_chunk, sem):
    return pltpu.make_async_copy(src_hbm.at[src_chunk], dst.at[dst_chunk], sem)


def _gather_chunks(table_ref, tile, n_chunks, src_hbm, dst, sem):
    for c in range(n_chunks):
        _chunk_copy(src_hbm, table_ref[tile * n_chunks + c], dst, c, sem).start(priority=c % 2)


def _gather_wait(src_hbm, dst, sem):
    pltpu.make_async_copy(src_hbm.at[pl.ds(0, dst.shape[0])], dst, sem).wait()


def _expert_kernel(te_ref, tv_ref, src_ref, xs_hbm, wg_ref, wu_ref, wd_ref, o_ref, buf, sem, wg_b, wu_b, wd_b,
                   *, tm, nt):
    i = pl.program_id(0)
    n_chunks = tm // CHUNK
    slot = i % 2

    @pl.when(jnp.logical_and(i == 0, tv_ref[0] == 1))
    def _():
        _gather_chunks(src_ref, 0, n_chunks, xs_hbm, buf.at[0], sem.at[0])

    @pl.when(jnp.logical_or(i == 0, te_ref[i] != te_ref[jnp.maximum(i - 1, 0)]))
    def _():
        wg_b[...] = wg_ref[0].astype(BF16)
        wu_b[...] = wu_ref[0].astype(BF16)
        wd_b[...] = wd_ref[0].astype(BF16)

    @pl.when(tv_ref[i] == 1)
    def _():
        @pl.when(jnp.logical_and(i + 1 < nt, tv_ref[jnp.minimum(i + 1, nt - 1)] == 1))
        def _():
            _gather_chunks(src_ref, i + 1, n_chunks, xs_hbm, buf.at[1 - slot], sem.at[1 - slot])

        _gather_wait(xs_hbm, buf.at[slot], sem.at[slot])
        half = tm // 2
        gate_up = []
        for r0 in (0, half):
            x = buf[slot, r0 // CHUNK:(r0 + half) // CHUNK].reshape(half, buf.shape[-1])
            gate_up.append((jnp.dot(x, wg_b[...], preferred_element_type=F32),
                            jnp.dot(x, wu_b[...], preferred_element_type=F32)))
        for r0, (a, b) in zip((0, half), gate_up):
            act = (_silu(a) * b).astype(BF16)
            o_ref[r0:r0 + half, :] = jnp.dot(act, wd_b[...], preferred_element_type=F32).astype(BF16)

    @pl.when(tv_ref[i] == 0)
    def _():
        o_ref[...] = jnp.zeros(o_ref.shape, BF16)


def _experts(tile_expert, tile_valid, chunk_src, xs, wg, wu, wd, *, tm, layer):
    nt = tile_expert.shape[0]
    _, D, Fe = wg.shape
    expert = lambda i, te: layer * N_EXPERTS + te[i]
    grid_spec = pltpu.PrefetchScalarGridSpec(
        num_scalar_prefetch=3,
        grid=(nt,),
        in_specs=[
            pl.BlockSpec(memory_space=pl.ANY),
            pl.BlockSpec((1, D, Fe), lambda i, te, tv, cs: (expert(i, te), 0, 0)),
            pl.BlockSpec((1, D, Fe), lambda i, te, tv, cs: (expert(i, te), 0, 0)),
            pl.BlockSpec((1, Fe, D), lambda i, te, tv, cs: (expert(i, te), 0, 0)),
        ],
        out_specs=pl.BlockSpec((tm, D), lambda i, te, tv, cs: (i, 0)),
        scratch_shapes=[pltpu.VMEM((2, tm // CHUNK, CHUNK, D), BF16), pltpu.SemaphoreType.DMA((2,)),
                        pltpu.VMEM((D, Fe), BF16), pltpu.VMEM((D, Fe), BF16), pltpu.VMEM((Fe, D), BF16)],
    )
    return pl.pallas_call(
        functools.partial(_expert_kernel, tm=tm, nt=nt),
        grid_spec=grid_spec,
        out_shape=jax.ShapeDtypeStruct((nt * tm, D), BF16),
        compiler_params=pltpu.CompilerParams(dimension_semantics=("arbitrary",), vmem_limit_bytes=VMEM_LIMIT),
        name="moe_experts",
    )(tile_expert, tile_valid, chunk_src, xs.reshape(-1, CHUNK, D), wg, wu, wd)


def _combine_final_kernel(npair_ref, dst_ref, o_hbm, x1_ref, pos_ref, wts_ref, mod_ref, gf_ref, out_ref, buf, sem,
                          *, tm, nt):
    x2 = _combine_compute(npair_ref, dst_ref, o_hbm, x1_ref, pos_ref, wts_ref, mod_ref, buf, sem, tm=tm, nt=nt)
    ms = jnp.mean(x2 * x2, axis=-1, keepdims=True)
    out_ref[0] = x2 * lax.rsqrt(ms + EPS) * gf_ref[...]


def _combine_premix_kernel(npair_ref, dst_ref, o_hbm, x1_ref, pos_ref, wts_ref, mod_ref,
                           mod_next_ref, g_ref, wqkv_ref, wf_ref, bf_ref, wu_ref, sel_ref,
                           out_ref, qt_ref, kaug_ref, vt_ref, diff_ref, buf, sem, carry_ref, ubuf_ref, *, tm, nt, nst):
    x2 = _combine_compute(npair_ref, dst_ref, o_hbm, x1_ref, pos_ref, wts_ref, mod_ref, buf, sem, tm=tm, nt=nt)
    out_ref[0] = x2
    _premix_compute(x2, pl.program_id(0) % nst, mod_next_ref, g_ref, wqkv_ref, wf_ref, bf_ref, wu_ref, sel_ref,
                    qt_ref, kaug_ref, vt_ref, diff_ref, carry_ref, ubuf_ref, tm=tm)


def _combine_compute(npair_ref, dst_ref, o_hbm, x1_ref, pos_ref, wts_ref, mod_ref, buf, sem, *, tm, nt):
    i = pl.program_id(0)
    max_chunks = buf.shape[1]
    rows = max_chunks * CHUNK
    slot = i % 2

    def gather(t, s):
        def body(p, carry):
            for k in range(2):
                c = 2 * p + k
                _chunk_copy(o_hbm, dst_ref[t * max_chunks + c], buf.at[s], c, sem.at[s]).start(priority=k)
            return carry

        lax.fori_loop(0, npair_ref[t], body, 0)

    def gather_wait(t, s):
        def body(p, carry):
            for k in range(2):
                _chunk_copy(o_hbm, 0, buf.at[s], 2 * p + k, sem.at[s]).wait()
            return carry

        lax.fori_loop(0, npair_ref[t], body, 0)

    @pl.when(i == 0)
    def _():
        buf[...] = jnp.zeros(buf.shape, BF16)
        gather(0, 0)

    @pl.when(i + 1 < nt)
    def _():
        gather(jnp.minimum(i + 1, nt - 1), 1 - slot)

    gather_wait(i, slot)
    rep = lambda r: jnp.broadcast_to(r, (LANES, tm)).T
    p0, p1 = rep(pos_ref[0:1, :]), rep(pos_ref[1:2, :])
    w0, w1 = rep(wts_ref[0:1, :]), rep(wts_ref[1:2, :])
    lane = lax.broadcasted_iota(jnp.int32, (tm, LANES), 1).astype(F32)
    cols = []
    for c in range(rows // LANES):
        r = lane + float(c * LANES)
        cols.append((jnp.where(p0 == r, w0, 0.0) + jnp.where(p1 == r, w1, 0.0)).astype(BF16))
    comb = jnp.concatenate(cols, axis=1)
    y = jnp.dot(comb, buf[slot].reshape(rows, buf.shape[-1]), preferred_element_type=F32)
    return x1_ref[0] + mod_ref[0, 5:6, :] * y


def _combine(n_chunks, chunk_dst, o_sorted, x1, pos, wts, mod, *, tm, g_final=None, next_premix=None):
    B, S, D = x1.shape
    nst = S // tm
    nt = B * nst
    rows = _sorted_rows(tm)
    bs = lambda i, np_, cd: (i // nst, i % nst)
    x_spec = pl.BlockSpec((1, tm, D), lambda i, np_, cd: (i // nst, i % nst, 0))
    tok_spec = pl.BlockSpec((TOP_K, tm), lambda i, np_, cd: (0, i))
    in_specs = [pl.BlockSpec(memory_space=pl.ANY), x_spec, tok_spec, tok_spec,
                pl.BlockSpec((1, N_MOD, D), lambda i, np_, cd: (i // nst, 0, 0))]
    args = [o_sorted.reshape(-1, CHUNK, D), x1, pos, wts, mod]
    out_specs = [x_spec]
    out_shape = [jax.ShapeDtypeStruct((B, S, D), F32)]
    scratch = [pltpu.VMEM((2, rows // CHUNK, CHUNK, D), BF16), pltpu.SemaphoreType.DMA((2,))]
    if next_premix is None:
        body = functools.partial(_combine_final_kernel, tm=tm, nt=nt)
        in_specs.append(pl.BlockSpec((1, D), lambda i, np_, cd: (0, 0)))
        args.append(g_final)
        name = "moe_combine_final"
    else:
        mod_next, consts = next_premix
        body = functools.partial(_combine_premix_kernel, tm=tm, nt=nt, nst=nst)
        p_in, p_out, p_shape, p_scratch = _premix_specs(B, S, D, tm, consts, bs)
        in_specs += p_in
        args += [mod_next, *consts]
        out_specs += p_out
        out_shape += p_shape
        scratch += p_scratch
        name = "moe_combine_premix"
    grid_spec = pltpu.PrefetchScalarGridSpec(
        num_scalar_prefetch=2, grid=(nt,), in_specs=in_specs, out_specs=out_specs, scratch_shapes=scratch)
    return pl.pallas_call(
        body,
        grid_spec=grid_spec,
        out_shape=out_shape,
        compiler_params=pltpu.CompilerParams(dimension_semantics=("arbitrary",), vmem_limit_bytes=VMEM_LIMIT),
        name=name,
    )((n_chunks + 1) // 2, chunk_dst, *args)


def _dispatch_tables(ids, *, tm, tm_e, nt_e):
    T = ids.shape[1]
    nts = T // tm
    rows = _sorted_rows(tm)
    max_chunks = rows // CHUNK
    cpt = tm_e // CHUNK
    experts = jnp.arange(N_EXPERTS, dtype=jnp.int32)
    onehot = (ids.reshape(TOP_K, nts, tm)[..., None] == experts).astype(jnp.int32)
    seg_chunks = (jnp.sum(onehot, axis=(0, 2)) + CHUNK - 1) // CHUNK
    local_first = jnp.cumsum(seg_chunks, axis=1) - seg_chunks
    n_chunks = jnp.sum(seg_chunks, axis=1)
    expert_chunks = jnp.sum(seg_chunks, axis=0)
    region = ((expert_chunks + cpt - 1) // cpt) * cpt
    region_end = jnp.cumsum(region)
    seg_first = (region_end - region)[None, :] + jnp.cumsum(seg_chunks, axis=0) - seg_chunks
    ci = jnp.arange(max_chunks, dtype=jnp.int32)
    in_seg = (ci[None, :, None] >= local_first[:, None, :]) & (ci[None, :, None] < (local_first + seg_chunks)[:, None, :])
    gchunk = jnp.sum(in_seg * (seg_first - local_first)[:, None, :], axis=2) + ci[None, :]
    used = ci[None, :] < n_chunks[:, None]
    chunk_dst = jnp.where(used, gchunk, 0).reshape(-1).astype(jnp.int32)
    n_global = nt_e * cpt
    local_chunk = jnp.arange(nts, dtype=jnp.int32)[:, None] * max_chunks + ci[None, :]
    zero_chunk = max_chunks - 1
    chunk_src = jnp.full((n_global,), zero_chunk, jnp.int32).at[
        jnp.where(used, gchunk, n_global).reshape(-1)].set(local_chunk.reshape(-1), mode="drop")
    tile_start = jnp.arange(nt_e, dtype=jnp.int32) * cpt
    tile_expert = jnp.minimum(jnp.sum((tile_start[:, None] >= region_end[None, :]).astype(jnp.int32), axis=1),
                              N_EXPERTS - 1)
    tile_valid = (tile_start < region_end[-1]).astype(jnp.int32)
    return n_chunks.astype(jnp.int32), chunk_dst, chunk_src, tile_expert, tile_valid


def kernel(x, c, norm_mix_g, norm_ffn_g, norm_final_g, w_ada, b_ada, w_in, b_fgate, w_pool, pool_scale, w_out,
           w_router_group, b_router_group, w_router_expert, b_router_expert, w_expert_gate, w_expert_up,
           w_expert_down):
    B, S, D = x.shape
    L = w_ada.shape[0]
    T = B * S
    tm_mix = min(512, S)
    tk = min(256, S)
    tq = min(2 * tk, S)
    tm_e = 512
    chunks_per_tile = tm_e // CHUNK
    max_used = (T // tm_mix) * (_sorted_rows(tm_mix) // CHUNK - 1) + N_EXPERTS * (chunks_per_tile - 1)
    nt_e = -(-max_used // chunks_per_tile)

    mod_all = _ada_modulation(c, w_ada, b_ada).reshape(L, B, N_MOD, D)
    Fe = w_expert_gate.shape[-1]
    wg_all = w_expert_gate.reshape(L * N_EXPERTS, D, Fe)
    wu_all = w_expert_up.reshape(L * N_EXPERTS, D, Fe)
    wd_all = w_expert_down.reshape(L * N_EXPERTS, Fe, D)
    sel = _forget_routing()

    def premix_consts(l):
        w_in_l = w_in[l]
        wqkv = w_in_l[:, :3 * ATTN_WIDTH].astype(BF16)
        wf = jnp.pad(w_in_l[:, 3 * ATTN_WIDTH:3 * ATTN_WIDTH + ATTN_HEADS], ((0, 0), (0, LANES - ATTN_HEADS))).astype(BF16)
        bf = jnp.pad(b_fgate[l].astype(F32), (0, LANES - ATTN_HEADS)).reshape(1, LANES)
        wu = w_in_l[:, 3 * ATTN_WIDTH + ATTN_HEADS:].astype(BF16)
        return (norm_mix_g[l].reshape(1, D), wqkv, wf, bf, wu, sel)

    qt, kaug, vt, diff = _premix(x, mod_all[0], premix_consts(0), tm=tm_mix)
    for l in range(L):
        mod = mod_all[l]
        attn = _attention(qt, kaug, vt, tq=tq, tk=tk)

        wr = jnp.concatenate([
            jnp.pad(w_router_group[l].T, ((0, SUBLANES - N_EXPERT_GROUPS), (0, 0))),
            w_router_expert[l].transpose(0, 2, 1).reshape(N_EXPERTS, D)], axis=0)
        wr_hi = wr.astype(BF16)
        wr_lo = (wr - wr_hi.astype(F32)).astype(BF16)
        br = jnp.concatenate([jnp.pad(b_router_group[l], (0, SUBLANES - N_EXPERT_GROUPS)),
                              b_router_expert[l].reshape(N_EXPERTS)]).reshape(ROUTER_ROWS, 1).astype(F32)
        x1, xs, ids, wts, pos = _postmix(attn, diff, x, mod, w_pool[l].astype(BF16),
                                         pool_scale[l].reshape(1, POOL_WIDTH), w_out[l].astype(BF16),
                                         norm_ffn_g[l].reshape(1, D), wr_hi, wr_lo, br, tm=tm_mix)

        n_chunks, chunk_dst, chunk_src, tile_expert, tile_valid = _dispatch_tables(ids, tm=tm_mix, tm_e=tm_e, nt_e=nt_e)
        o_sorted = _experts(tile_expert, tile_valid, chunk_src, xs, wg_all, wu_all, wd_all, tm=tm_e, layer=l)
        if l == L - 1:
            (x,) = _combine(n_chunks, chunk_dst, o_sorted, x1, pos, wts, mod, tm=tm_mix,
                            g_final=norm_final_g.reshape(1, D))
        else:
            x, qt, kaug, vt, diff = _combine(n_chunks, chunk_dst, o_sorted, x1, pos, wts, mod, tm=tm_mix,
                                             next_premix=(mod_all[l + 1], premix_consts(l + 1)))
    return x
```

```python
import functools

import jax
import jax.numpy as jnp
import numpy as np
from jax import lax
from jax.experimental import pallas as pl
from jax.experimental.pallas import tpu as pltpu

ATTN_HEADS = 8
HEAD_DIM = 64
ATTN_WIDTH = ATTN_HEADS * HEAD_DIM
POOL_WINDOWS = (2, 4, 8, 16)
POOL_GROUP_DIM = 128
POOL_WIDTH = POOL_GROUP_DIM * len(POOL_WINDOWS)
POOL_HALO = 16
N_EXPERT_GROUPS = 4
EXPERTS_PER_GROUP = 8
N_EXPERTS = N_EXPERT_GROUPS * EXPERTS_PER_GROUP
N_MOD = 6
EPS = 1e-6
NEG_INF = -1e30
LOG2E = 1.4426950408889634

LANES = 128
SUBLANES = 8
AUG = 2 * LANES
DENOM_ROWS = 16
ATTN_UNROLL = 4
ROUTER_ROWS = 40
CHUNK = 8
TOP_K = 2
VMEM_LIMIT = 48 * 1024 * 1024


def _sorted_rows(tm):
    worst = TOP_K * tm + N_EXPERTS * (CHUNK - 1)
    return (worst // LANES + 1) * LANES

F32 = jnp.float32
BF16 = jnp.bfloat16


def _silu(a):
    return a * jax.nn.sigmoid(a)


def _nt_dot(a, b):
    return lax.dot_general(a, b, (((1,), (1,)), ((), ())), preferred_element_type=F32)


def _split3(a):
    t0 = a.astype(BF16)
    r1 = a - t0.astype(F32)
    t1 = r1.astype(BF16)
    t2 = (r1 - t1.astype(F32)).astype(BF16)
    return t0, t1, t2


def _pack_pairs(x):
    n = x.shape[1] // 2
    lo = pltpu.bitcast(x[:, :n], jnp.uint32) >> 16
    hi = pltpu.bitcast(x[:, n:], jnp.uint32) & jnp.uint32(0xFFFF0000)
    return hi | lo


def _unpack_pairs(w):
    lo = pltpu.bitcast(w << 16, F32)
    hi = pltpu.bitcast(w & jnp.uint32(0xFFFF0000), F32)
    return jnp.concatenate([lo, hi], axis=1).astype(BF16)


def _rms_modulate(x, g, shift, scale):
    ms = jnp.mean(x * x, axis=-1, keepdims=True)
    y = x * lax.rsqrt(ms + EPS) * g
    return y * (1.0 + scale) + shift


def _ada_kernel(c_ref, w_ref, b_ref, o_ref):
    ca = _silu(c_ref[...])
    o_ref[0] = jnp.dot(ca, w_ref[0], precision=lax.Precision.HIGHEST, preferred_element_type=F32) + b_ref[0]


def _ada_modulation(c, w_ada, b_ada):
    L, D, W = w_ada.shape
    B = c.shape[0]
    tn = W // 4
    return pl.pallas_call(
        _ada_kernel,
        grid=(L, W // tn),
        in_specs=[
            pl.BlockSpec((B, D), lambda l, n: (0, 0)),
            pl.BlockSpec((1, D, tn), lambda l, n: (l, 0, n)),
            pl.BlockSpec((1, 1, tn), lambda l, n: (l, 0, n)),
        ],
        out_specs=pl.BlockSpec((1, B, tn), lambda l, n: (l, 0, n)),
        out_shape=jax.ShapeDtypeStruct((L, B, W), F32),
        compiler_params=pltpu.CompilerParams(vmem_limit_bytes=VMEM_LIMIT),
        name="ada_modulation",
    )(c, w_ada, b_ada.reshape(L, 1, W))


def _premix_kernel(x_ref, mod_ref, g_ref, wqkv_ref, wf_ref, bf_ref, wu_ref, sel_ref,
                   qt_ref, kaug_ref, vt_ref, diff_ref, carry_ref, ubuf_ref, *, tm):
    _premix_compute(x_ref[0], pl.program_id(1), mod_ref, g_ref, wqkv_ref, wf_ref, bf_ref, wu_ref, sel_ref,
                    qt_ref, kaug_ref, vt_ref, diff_ref, carry_ref, ubuf_ref, tm=tm)


def _premix_compute(x, si, mod_ref, g_ref, wqkv_ref, wf_ref, bf_ref, wu_ref, sel_ref,
                    qt_ref, kaug_ref, vt_ref, diff_ref, carry_ref, ubuf_ref, *, tm):
    @pl.when(si == 0)
    def _():
        carry_ref[...] = jnp.zeros_like(carry_ref)
        ubuf_ref[0:POOL_HALO, :] = jnp.zeros((POOL_HALO, POOL_WIDTH), F32)

    h = _rms_modulate(x, g_ref[...], mod_ref[0, 0:1, :], mod_ref[0, 1:2, :]).astype(BF16)
    u = jnp.dot(h, wu_ref[...], preferred_element_type=F32)
    fl = jnp.dot(h, wf_ref[...], preferred_element_type=F32) + bf_ref[...]
    qkv = jnp.dot(h, wqkv_ref[...], preferred_element_type=F32)

    ubuf_ref[POOL_HALO:POOL_HALO + tm, :] = u
    pos = si * tm + lax.broadcasted_iota(jnp.int32, (tm, POOL_GROUP_DIM), 0)
    diffs = []
    for g, w in enumerate(POOL_WINDOWS):
        c0 = g * POOL_GROUP_DIM
        ug = u[:, c0:c0 + POOL_GROUP_DIM]
        acc = ug
        for j in range(1, w):
            acc = acc + ubuf_ref[POOL_HALO - j:POOL_HALO - j + tm, c0:c0 + POOL_GROUP_DIM]
        cnt = jnp.minimum(pos + 1, w).astype(F32)
        diffs.append((acc / cnt - ug).astype(BF16))
    diff_ref[0] = jnp.concatenate(diffs, axis=1)
    ubuf_ref[0:POOL_HALO, :] = u[tm - POOL_HALO:, :]

    lf = jnp.minimum(fl, 0.0) - jnp.log1p(jnp.exp(-jnp.abs(fl)))
    row = lax.broadcasted_iota(jnp.int32, (tm, tm), 0)
    col = lax.broadcasted_iota(jnp.int32, (tm, tm), 1)
    tri = (row >= col).astype(BF16)
    cs = None
    for term in _split3(lf):
        d = jnp.dot(tri, term, preferred_element_type=F32)
        cs = d if cs is None else cs + d
    f_cum = cs + carry_ref[...]
    carry_ref[...] = f_cum[tm - 1:tm, :]

    aug = None
    for i, term in enumerate(_split3(-LOG2E * f_cum)):
        d = jnp.dot(term, sel_ref[i], preferred_element_type=F32)
        aug = d if aug is None else aug + d
    kaug_ref[0] = jnp.concatenate([qkv[:, ATTN_WIDTH:2 * ATTN_WIDTH], aug], axis=1).astype(BF16)
    qt_ref[0] = (qkv[:, :ATTN_WIDTH] * (LOG2E * HEAD_DIM ** -0.5)).T.astype(BF16)
    vt_ref[0] = qkv[:, 2 * ATTN_WIDTH:].T.astype(BF16)


def _premix_specs(B, S, D, tm, consts, bs):
    def at(f):
        return lambda *idx: f(*bs(*idx))

    row_spec = lambda w: pl.BlockSpec((1, tm, w), at(lambda b, s: (b, s, 0)))
    col_spec = pl.BlockSpec((1, ATTN_WIDTH, tm), at(lambda b, s: (b, 0, s)))
    const = lambda a: pl.BlockSpec(a.shape, lambda *idx: (0,) * a.ndim)
    kaug_w = ATTN_WIDTH + LANES
    in_specs = [pl.BlockSpec((1, N_MOD, D), at(lambda b, s: (b, 0, 0)))] + [const(a) for a in consts]
    out_specs = [col_spec, row_spec(kaug_w), col_spec, row_spec(POOL_WIDTH)]
    out_shape = [
        jax.ShapeDtypeStruct((B, ATTN_WIDTH, S), BF16),
        jax.ShapeDtypeStruct((B, S, kaug_w), BF16),
        jax.ShapeDtypeStruct((B, ATTN_WIDTH, S), BF16),
        jax.ShapeDtypeStruct((B, S, POOL_WIDTH), BF16),
    ]
    scratch = [pltpu.VMEM((1, LANES), F32), pltpu.VMEM((POOL_HALO + tm, POOL_WIDTH), F32)]
    return in_specs, out_specs, out_shape, scratch


def _premix(x, mod, consts, *, tm):
    B, S, D = x.shape
    in_specs, out_specs, out_shape, scratch = _premix_specs(B, S, D, tm, consts, lambda b, s: (b, s))
    return pl.pallas_call(
        functools.partial(_premix_kernel, tm=tm),
        grid=(B, S // tm),
        in_specs=[pl.BlockSpec((1, tm, D), lambda b, s: (b, s, 0))] + in_specs,
        out_specs=out_specs,
        out_shape=out_shape,
        scratch_shapes=scratch,
        compiler_params=pltpu.CompilerParams(
            dimension_semantics=("arbitrary", "arbitrary"), vmem_limit_bytes=VMEM_LIMIT),
        name="premix",
    )(x, mod, *consts)


def _forget_routing():
    sel = np.zeros((3, LANES, LANES), np.float32)
    for i in range(3):
        for h in range(ATTN_HEADS):
            sel[i, h, 3 * h + i] = 1.0
    return jnp.asarray(sel, BF16)


def _attn_kernel(qt_in_ref, kaug_ref, vt_ref, o_ref, qt_ref, m_ref, acc_ref, s_ref, mb_ref, *, tq, n_pairs):
    tk = tq
    qi = pl.program_id(1)
    n_heads = 2 * n_pairs
    @pl.when(qi == 0)
    def _():
        r128 = lax.broadcasted_iota(jnp.int32, (LANES, tq), 0)
        for h in range(n_heads):
            qt_ref[h, 0:LANES, :] = jnp.zeros((LANES, tq), BF16)
            qt_ref[h, LANES:AUG, :] = ((r128 >= 3 * h) & (r128 < 3 * h + 3)).astype(BF16)

    for h in range(n_heads):
        r0 = HEAD_DIM * (h % 2)
        qt_ref[h, r0:r0 + HEAD_DIM, :] = qt_in_ref[0, h * HEAD_DIM:(h + 1) * HEAD_DIM, :]
    m_ref[...] = jnp.full(m_ref.shape, NEG_INF, F32)
    acc_ref[...] = jnp.zeros(acc_ref.shape, F32)
    ones = jnp.ones((DENOM_ROWS, tk), BF16)

    def step(new=None, cur=None):
        if new is not None:
            jn, slot_n, masked = new
            k0n = pl.multiple_of(jn * tk, tk)
            f_terms = kaug_ref[0, pl.ds(k0n, tk), n_pairs * LANES:(n_pairs + 1) * LANES]
        if cur is not None:
            jc, slot_c = cur
            k0c = pl.multiple_of(jc * tk, tk)
        for h in range(n_heads):
            if new is not None:
                k_pair = kaug_ref[0, pl.ds(k0n, tk), (h // 2) * LANES:(h // 2 + 1) * LANES]
                s = jnp.dot(jnp.concatenate([k_pair, f_terms], axis=1), qt_ref[h],
                            preferred_element_type=F32)
                if masked:
                    key = lax.broadcasted_iota(jnp.int32, (tk, tq), 0)
                    qry = lax.broadcasted_iota(jnp.int32, (tk, tq), 1)
                    s = jnp.where(key <= qry, s, NEG_INF)
                s_ref[slot_n, h] = s
                mb_ref[slot_n, h] = jnp.max(s, axis=0, keepdims=True)
            if cur is not None:
                m_prev = m_ref[h]
                m_new = jnp.maximum(m_prev, mb_ref[slot_c, h])
                pt = jnp.exp2(s_ref[slot_c, h] - m_new).astype(BF16)
                alpha = jnp.exp2(m_prev - m_new)
                vtb = vt_ref[0, pl.ds(h * HEAD_DIM, HEAD_DIM), pl.ds(k0c, tk)]
                lhs = jnp.concatenate([vtb, ones], axis=0)
                acc_ref[h] = alpha * acc_ref[h] + jnp.dot(lhs, pt, preferred_element_type=F32)
                m_ref[h] = m_new

    @pl.when(qi == 0)
    def _():
        step(new=(0, 0, True))

    @pl.when(qi > 0)
    def _():
        step(new=(0, 0, False))

    def body(jj, c):
        j = ATTN_UNROLL * jj
        for k in range(1, ATTN_UNROLL + 1):
            step(new=(j + k, k % 2, False), cur=(j + k - 1, (k - 1) % 2))
        return c

    n_loops = jnp.maximum(qi - 1, 0) // ATTN_UNROLL
    lax.fori_loop(0, n_loops, body, 0)
    j0 = ATTN_UNROLL * n_loops
    rem = qi - j0

    @pl.when(rem == 0)
    def _():
        step(cur=(0, 0))

    for r in range(1, ATTN_UNROLL + 1):
        @pl.when(rem == r)
        def _(r=r):
            for k in range(1, r + 1):
                step(new=(j0 + k, k % 2, k == r), cur=(j0 + k - 1, (k - 1) % 2))
            step(cur=(qi, r % 2))

    for p in range(n_pairs):
        outs = []
        for hh in range(2):
            a = acc_ref[2 * p + hh]
            outs.append(a[:HEAD_DIM, :] / a[HEAD_DIM:HEAD_DIM + 1, :])
        o_ref[0, :, p * LANES:(p + 1) * LANES] = jnp.concatenate(outs, axis=0).T.astype(BF16)


def _attention(qt, kaug, vt, *, tq):
    B, W, S = qt.shape
    n_pairs = W // LANES
    return pl.pallas_call(
        functools.partial(_attn_kernel, tq=tq, n_pairs=n_pairs),
        grid=(B, S // tq),
        in_specs=[pl.BlockSpec((1, W, tq), lambda b, i: (b, 0, i)),
                  pl.BlockSpec((1, S, (n_pairs + 1) * LANES), lambda b, i: (b, 0, 0)),
                  pl.BlockSpec((1, W, S), lambda b, i: (b, 0, 0))],
        out_specs=pl.BlockSpec((1, tq, W), lambda b, i: (b, i, 0)),
        out_shape=jax.ShapeDtypeStruct((B, S, W), BF16),
        scratch_shapes=[pltpu.VMEM((2 * n_pairs, AUG, tq), BF16),
                        pltpu.VMEM((2 * n_pairs, 1, tq), F32),
                        pltpu.VMEM((2 * n_pairs, HEAD_DIM + DENOM_ROWS, tq), F32),
                        pltpu.VMEM((2, 2 * n_pairs, tq, tq), F32),
                        pltpu.VMEM((2, 2 * n_pairs, 1, tq), F32)],
        compiler_params=pltpu.CompilerParams(
            dimension_semantics=("arbitrary", "arbitrary"), vmem_limit_bytes=VMEM_LIMIT),
        name="fox_attention",
    )(qt, kaug, vt)


def _postmix_kernel(attn_ref, diff_ref, x_ref, mod_ref, wpool_ref, pscale_ref, wout_ref, g_ref,
                    wr_hi_ref, wr_lo_ref, br_ref, before_ref,
                    x1_ref, xs_ref, ids_ref, wts_ref, pos_ref, *, tm):
    pooled = []
    for g in range(len(POOL_WINDOWS)):
        c0 = g * POOL_GROUP_DIM
        pooled.append(jnp.dot(diff_ref[0, :, c0:c0 + POOL_GROUP_DIM], wpool_ref[g], preferred_element_type=F32))
    pool_out = (jnp.concatenate(pooled, axis=1) * pscale_ref[...]).astype(BF16)
    cat = jnp.concatenate([attn_ref[0], pool_out], axis=1)
    mix = jnp.dot(cat, wout_ref[...], preferred_element_type=F32)
    x1 = x_ref[0] + mod_ref[0, 2:3, :] * mix
    x1_ref[0] = x1

    h = _rms_modulate(x1, g_ref[...], mod_ref[0, 3:4, :], mod_ref[0, 4:5, :])

    h_hi = h.astype(BF16)
    h_lo = (h - h_hi.astype(F32)).astype(BF16)
    logits = (_nt_dot(wr_hi_ref[...], h_hi) + _nt_dot(wr_lo_ref[...], h_hi) + _nt_dot(wr_hi_ref[...], h_lo)
              + br_ref[...])
    sub = lax.broadcasted_iota(jnp.int32, (SUBLANES, tm), 0)
    lg = jnp.where(sub < N_EXPERT_GROUPS, logits[0:SUBLANES, :], NEG_INF)
    g_max = jnp.max(lg, axis=0, keepdims=True)
    top_p = 1.0 / jnp.sum(jnp.exp(lg - g_max), axis=0, keepdims=True)
    top_g = jnp.min(jnp.where(lg == g_max, sub, SUBLANES), axis=0, keepdims=True)
    le = logits[SUBLANES:2 * SUBLANES, :]
    for g in range(1, N_EXPERT_GROUPS):
        le = jnp.where(top_g == g, logits[(g + 1) * SUBLANES:(g + 2) * SUBLANES, :], le)
    v1 = jnp.max(le, axis=0, keepdims=True)
    i1 = jnp.min(jnp.where(le == v1, sub, SUBLANES), axis=0, keepdims=True)
    le2 = jnp.where(sub == i1, NEG_INF, le)
    v2 = jnp.max(le2, axis=0, keepdims=True)
    i2 = jnp.min(jnp.where(le2 == v2, sub, SUBLANES), axis=0, keepdims=True)
    e2 = jnp.exp(v2 - v1)
    w1 = top_p / (1.0 + e2)
    id0 = top_g * EXPERTS_PER_GROUP + i1
    id1 = top_g * EXPERTS_PER_GROUP + i2
    ids_ref[...] = jnp.concatenate([id0, id1], axis=0)
    wts_ref[...] = jnp.concatenate([w1, w1 * e2], axis=0)

    sub_e = lax.broadcasted_iota(jnp.int32, (N_EXPERTS, tm), 0)
    onehot = jnp.concatenate([sub_e == id0, sub_e == id1], axis=1)
    oh_f = onehot.astype(F32)
    rank = jnp.dot(onehot.astype(BF16), before_ref[...], preferred_element_type=F32)
    chunks = jnp.floor((jnp.sum(oh_f, axis=1, keepdims=True) + (CHUNK - 1.0)) * (1.0 / CHUNK))
    er = lax.broadcasted_iota(jnp.int32, (N_EXPERTS, N_EXPERTS), 0)
    ec = lax.broadcasted_iota(jnp.int32, (N_EXPERTS, N_EXPERTS), 1)
    first_chunk = jnp.dot((er > ec).astype(BF16), jnp.broadcast_to(chunks, (N_EXPERTS, LANES)).astype(BF16),
                          preferred_element_type=F32)[:, 0:1]
    pos = jnp.sum(oh_f * (rank + CHUNK * first_chunk), axis=0, keepdims=True)
    pos0 = pos[:, :tm]
    pos1 = pos[:, tm:]
    pos_ref[...] = jnp.concatenate([pos0, pos1], axis=0)
    r_iota = lax.broadcasted_iota(jnp.int32, (xs_ref.shape[0], tm), 0)
    perm = ((r_iota == pos0.astype(jnp.int32)) | (r_iota == pos1.astype(jnp.int32))).astype(BF16)
    xs_ref[...] = _pack_pairs(jnp.dot(perm, h_hi, preferred_element_type=F32))


def _postmix(attn, diff, x, mod, wpool, pscale, wout, g, wr_hi, wr_lo, br, *, tm):
    B, S, D = x.shape
    T = B * S
    nst = S // tm
    rows = _sorted_rows(tm)
    before = jnp.asarray(np.triu(np.ones((TOP_K * tm, TOP_K * tm), np.float32), k=1), BF16)
    row_spec = lambda w: pl.BlockSpec((1, tm, w), lambda b, s: (b, s, 0))
    const = lambda a: pl.BlockSpec(a.shape, lambda b, s: (0,) * a.ndim)
    tok_spec = pl.BlockSpec((TOP_K, tm), lambda b, s: (0, b * nst + s))
    return pl.pallas_call(
        functools.partial(_postmix_kernel, tm=tm),
        grid=(B, nst),
        in_specs=[row_spec(ATTN_WIDTH), row_spec(POOL_WIDTH), row_spec(D),
                  pl.BlockSpec((1, N_MOD, D), lambda b, s: (b, 0, 0)),
                  const(wpool), const(pscale), const(wout), const(g), const(wr_hi), const(wr_lo), const(br),
                  const(before)],
        out_specs=[row_spec(D),
                   pl.BlockSpec((rows, D // 2), lambda b, s: (b * nst + s, 0)),
                   tok_spec, tok_spec, tok_spec],
        out_shape=[
            jax.ShapeDtypeStruct((B, S, D), F32),
            jax.ShapeDtypeStruct((B * nst * rows, D // 2), jnp.uint32),
            jax.ShapeDtypeStruct((TOP_K, T), jnp.int32),
            jax.ShapeDtypeStruct((TOP_K, T), F32),
            jax.ShapeDtypeStruct((TOP_K, T), F32),
        ],
        compiler_params=pltpu.CompilerParams(
            dimension_semantics=("arbitrary", "arbitrary"), vmem_limit_bytes=VMEM_LIMIT),
        name="postmix_router",
    )(attn, diff, x, mod, wpool, pscale, wout, g, wr_hi, wr_lo, br, before)


def _chunk_copy(src_hbm, src_chunk, dst, dst_chunk, sem):
    return pltpu.make_async_copy(src_hbm.at[src_chunk], dst.at[dst_chunk], sem)


def _gather_chunks(table_ref, tile, n_chunks, src_hbm, dst, sem):
    for c in range(n_chunks):
        _chunk_copy(src_hbm, table_ref[tile * n_chunks + c], dst, c, sem).start(priority=c % 2)


def _gather_wait(src_hbm, dst, sem):
    pltpu.make_async_copy(src_hbm.at[pl.ds(0, dst.shape[0])], dst, sem).wait()


def _expert_kernel(te_ref, tv_ref, src_ref, xs_hbm, wg_ref, wu_ref, wd_ref, o_ref, buf, sem, wg_b, wu_b, wd_b,
                   *, tm, nt):
    i = pl.program_id(0)
    n_chunks = tm // CHUNK
    slot = i % 2

    @pl.when(jnp.logical_and(i == 0, tv_ref[0] == 1))
    def _():
        _gather_chunks(src_ref, 0, n_chunks, xs_hbm, buf.at[0], sem.at[0])

    @pl.when(jnp.logical_or(i == 0, te_ref[i] != te_ref[jnp.maximum(i - 1, 0)]))
    def _():
        wg_b[...] = wg_ref[0].astype(BF16)
        wu_b[...] = wu_ref[0].astype(BF16)
        wd_b[...] = wd_ref[0].astype(BF16)

    @pl.when(tv_ref[i] == 1)
    def _():
        @pl.when(jnp.logical_and(i + 1 < nt, tv_ref[jnp.minimum(i + 1, nt - 1)] == 1))
        def _():
            _gather_chunks(src_ref, i + 1, n_chunks, xs_hbm, buf.at[1 - slot], sem.at[1 - slot])

        _gather_wait(xs_hbm, buf.at[slot], sem.at[slot])
        half = tm // 2
        gate_up = []
        for r0 in (0, half):
            x = _unpack_pairs(buf[slot, r0 // CHUNK:(r0 + half) // CHUNK].reshape(half, buf.shape[-1]))
            gate_up.append((jnp.dot(x, wg_b[...], preferred_element_type=F32),
                            jnp.dot(x, wu_b[...], preferred_element_type=F32)))
        for r0, (a, b) in zip((0, half), gate_up):
            act = (_silu(a) * b).astype(BF16)
            o = jnp.dot(act, wd_b[...], preferred_element_type=F32)
            o_ref[r0:r0 + half, :] = _pack_pairs(o.astype(BF16).astype(F32))

    @pl.when(tv_ref[i] == 0)
    def _():
        o_ref[...] = jnp.zeros(o_ref.shape, jnp.uint32)


def _experts(tile_expert, tile_valid, chunk_src, xs, wg, wu, wd, *, tm, layer):
    nt = tile_expert.shape[0]
    _, D, Fe = wg.shape
    expert = lambda i, te: layer * N_EXPERTS + te[i]
    grid_spec = pltpu.PrefetchScalarGridSpec(
        num_scalar_prefetch=3,
        grid=(nt,),
        in_specs=[
            pl.BlockSpec(memory_space=pl.ANY),
            pl.BlockSpec((1, D, Fe), lambda i, te, tv, cs: (expert(i, te), 0, 0)),
            pl.BlockSpec((1, D, Fe), lambda i, te, tv, cs: (expert(i, te), 0, 0)),
            pl.BlockSpec((1, Fe, D), lambda i, te, tv, cs: (expert(i, te), 0, 0)),
        ],
        out_specs=pl.BlockSpec((tm, D // 2), lambda i, te, tv, cs: (i, 0)),
        scratch_shapes=[pltpu.VMEM((2, tm // CHUNK, CHUNK, D // 2), jnp.uint32), pltpu.SemaphoreType.DMA((2,)),
                        pltpu.VMEM((D, Fe), BF16), pltpu.VMEM((D, Fe), BF16), pltpu.VMEM((Fe, D), BF16)],
    )
    return pl.pallas_call(
        functools.partial(_expert_kernel, tm=tm, nt=nt),
        grid_spec=grid_spec,
        out_shape=jax.ShapeDtypeStruct((nt * tm, D // 2), jnp.uint32),
        compiler_params=pltpu.CompilerParams(dimension_semantics=("arbitrary",), vmem_limit_bytes=VMEM_LIMIT),
        name="moe_experts",
    )(tile_expert, tile_valid, chunk_src, xs.reshape(-1, CHUNK, D // 2), wg, wu, wd)


def _combine_final_kernel(npair_ref, dst_ref, o_hbm, x1_ref, pos_ref, wts_ref, mod_ref, gf_ref, out_ref, buf, sem,
                          *, tm, nt):
    x2 = _combine_compute(npair_ref, dst_ref, o_hbm, x1_ref, pos_ref, wts_ref, mod_ref, buf, sem, tm=tm, nt=nt)
    ms = jnp.mean(x2 * x2, axis=-1, keepdims=True)
    out_ref[0] = x2 * lax.rsqrt(ms + EPS) * gf_ref[...]


def _combine_premix_kernel(npair_ref, dst_ref, o_hbm, x1_ref, pos_ref, wts_ref, mod_ref,
                           mod_next_ref, g_ref, wqkv_ref, wf_ref, bf_ref, wu_ref, sel_ref,
                           out_ref, qt_ref, kaug_ref, vt_ref, diff_ref, buf, sem, carry_ref, ubuf_ref, *, tm, nt, nst):
    x2 = _combine_compute(npair_ref, dst_ref, o_hbm, x1_ref, pos_ref, wts_ref, mod_ref, buf, sem, tm=tm, nt=nt)
    out_ref[0] = x2
    _premix_compute(x2, pl.program_id(0) % nst, mod_next_ref, g_ref, wqkv_ref, wf_ref, bf_ref, wu_ref, sel_ref,
                    qt_ref, kaug_ref, vt_ref, diff_ref, carry_ref, ubuf_ref, tm=tm)


def _combine_compute(npair_ref, dst_ref, o_hbm, x1_ref, pos_ref, wts_ref, mod_ref, buf, sem, *, tm, nt):
    i = pl.program_id(0)
    max_chunks = buf.shape[1]
    rows = max_chunks * CHUNK
    slot = i % 2

    def gather(t, s):
        def body(p, carry):
            for k in range(2):
                c = 2 * p + k
                _chunk_copy(o_hbm, dst_ref[t * max_chunks + c], buf.at[s], c, sem.at[s]).start(priority=k)
            return carry

        lax.fori_loop(0, npair_ref[t], body, 0)

    def gather_wait(t, s):
        def body(p, carry):
            for k in range(2):
                _chunk_copy(o_hbm, 0, buf.at[s], 2 * p + k, sem.at[s]).wait()
            return carry

        lax.fori_loop(0, npair_ref[t], body, 0)

    @pl.when(i == 0)
    def _():
        buf[...] = jnp.zeros(buf.shape, jnp.uint32)
        gather(0, 0)

    @pl.when(i + 1 < nt)
    def _():
        gather(jnp.minimum(i + 1, nt - 1), 1 - slot)

    gather_wait(i, slot)
    rep = lambda r: jnp.broadcast_to(r, (LANES, tm)).T
    p0, p1 = rep(pos_ref[0:1, :]), rep(pos_ref[1:2, :])
    w0, w1 = rep(wts_ref[0:1, :]), rep(wts_ref[1:2, :])
    lane = lax.broadcasted_iota(jnp.int32, (tm, LANES), 1).astype(F32)
    cols = []
    for c in range(rows // LANES):
        r = lane + float(c * LANES)
        cols.append((jnp.where(p0 == r, w0, 0.0) + jnp.where(p1 == r, w1, 0.0)).astype(BF16))
    comb = jnp.concatenate(cols, axis=1)
    y = jnp.dot(comb, _unpack_pairs(buf[slot].reshape(rows, buf.shape[-1])), preferred_element_type=F32)
    return x1_ref[0] + mod_ref[0, 5:6, :] * y


def _combine(n_chunks, chunk_dst, o_sorted, x1, pos, wts, mod, *, tm, g_final=None, next_premix=None):
    B, S, D = x1.shape
    nst = S // tm
    nt = B * nst
    rows = _sorted_rows(tm)
    bs = lambda i, np_, cd: (i // nst, i % nst)
    x_spec = pl.BlockSpec((1, tm, D), lambda i, np_, cd: (i // nst, i % nst, 0))
    tok_spec = pl.BlockSpec((TOP_K, tm), lambda i, np_, cd: (0, i))
    in_specs = [pl.BlockSpec(memory_space=pl.ANY), x_spec, tok_spec, tok_spec,
                pl.BlockSpec((1, N_MOD, D), lambda i, np_, cd: (i // nst, 0, 0))]
    args = [o_sorted.reshape(-1, CHUNK, D // 2), x1, pos, wts, mod]
    out_specs = [x_spec]
    out_shape = [jax.ShapeDtypeStruct((B, S, D), F32)]
    scratch = [pltpu.VMEM((2, rows // CHUNK, CHUNK, D // 2), jnp.uint32), pltpu.SemaphoreType.DMA((2,))]
    if next_premix is None:
        body = functools.partial(_combine_final_kernel, tm=tm, nt=nt)
        in_specs.append(pl.BlockSpec((1, D), lambda i, np_, cd: (0, 0)))
        args.append(g_final)
        name = "moe_combine_final"
    else:
        mod_next, consts = next_premix
        body = functools.partial(_combine_premix_kernel, tm=tm, nt=nt, nst=nst)
        p_in, p_out, p_shape, p_scratch = _premix_specs(B, S, D, tm, consts, bs)
        in_specs += p_in
        args += [mod_next, *consts]
        out_specs += p_out
        out_shape += p_shape
        scratch += p_scratch
        name = "moe_combine_premix"
    grid_spec = pltpu.PrefetchScalarGridSpec(
        num_scalar_prefetch=2, grid=(nt,), in_specs=in_specs, out_specs=out_specs, scratch_shapes=scratch)
    return pl.pallas_call(
        body,
        grid_spec=grid_spec,
        out_shape=out_shape,
        compiler_params=pltpu.CompilerParams(dimension_semantics=("arbitrary",), vmem_limit_bytes=VMEM_LIMIT),
        name=name,
    )((n_chunks + 1) // 2, chunk_dst, *args)


def _dispatch_tables(ids, *, tm, tm_e, nt_e):
    T = ids.shape[1]
    nts = T // tm
    rows = _sorted_rows(tm)
    max_chunks = rows // CHUNK
    cpt = tm_e // CHUNK
    experts = jnp.arange(N_EXPERTS, dtype=jnp.int32)
    onehot = (ids.reshape(TOP_K, nts, tm)[..., None] == experts).astype(jnp.int32)
    seg_chunks = (jnp.sum(onehot, axis=(0, 2)) + CHUNK - 1) // CHUNK
    local_first = jnp.cumsum(seg_chunks, axis=1) - seg_chunks
    n_chunks = jnp.sum(seg_chunks, axis=1)
    expert_chunks = jnp.sum(seg_chunks, axis=0)
    region = ((expert_chunks + cpt - 1) // cpt) * cpt
    region_end = jnp.cumsum(region)
    seg_first = (region_end - region)[None, :] + jnp.cumsum(seg_chunks, axis=0) - seg_chunks
    ci = jnp.arange(max_chunks, dtype=jnp.int32)
    in_seg = (ci[None, :, None] >= local_first[:, None, :]) & (ci[None, :, None] < (local_first + seg_chunks)[:, None, :])
    gchunk = jnp.sum(in_seg * (seg_first - local_first)[:, None, :], axis=2) + ci[None, :]
    used = ci[None, :] < n_chunks[:, None]
    chunk_dst = jnp.where(used, gchunk, 0).reshape(-1).astype(jnp.int32)
    n_global = nt_e * cpt
    local_chunk = jnp.arange(nts, dtype=jnp.int32)[:, None] * max_chunks + ci[None, :]
    zero_chunk = max_chunks - 1
    chunk_src = jnp.full((n_global,), zero_chunk, jnp.int32).at[
        jnp.where(used, gchunk, n_global).reshape(-1)].set(local_chunk.reshape(-1), mode="drop")
    tile_start = jnp.arange(nt_e, dtype=jnp.int32) * cpt
    tile_expert = jnp.minimum(jnp.sum((tile_start[:, None] >= region_end[None, :]).astype(jnp.int32), axis=1),
                              N_EXPERTS - 1)
    tile_valid = (tile_start < region_end[-1]).astype(jnp.int32)
    return n_chunks.astype(jnp.int32), chunk_dst, chunk_src, tile_expert, tile_valid


def kernel(x, c, norm_mix_g, norm_ffn_g, norm_final_g, w_ada, b_ada, w_in, b_fgate, w_pool, pool_scale, w_out,
           w_router_group, b_router_group, w_router_expert, b_router_expert, w_expert_gate, w_expert_up,
           w_expert_down):
    B, S, D = x.shape
    L = w_ada.shape[0]
    T = B * S
    tm_mix = min(512, S)
    tq = min(256, S)
    tm_e = 512
    chunks_per_tile = tm_e // CHUNK
    max_used = (T // tm_mix) * (_sorted_rows(tm_mix) // CHUNK - 1) + N_EXPERTS * (chunks_per_tile - 1)
    nt_e = -(-max_used // chunks_per_tile)

    mod_all = _ada_modulation(c, w_ada, b_ada).reshape(L, B, N_MOD, D)
    Fe = w_expert_gate.shape[-1]
    wg_all = w_expert_gate.reshape(L * N_EXPERTS, D, Fe)
    wu_all = w_expert_up.reshape(L * N_EXPERTS, D, Fe)
    wd_all = w_expert_down.reshape(L * N_EXPERTS, Fe, D)
    sel = _forget_routing()

    def premix_consts(l):
        w_in_l = w_in[l]
        wqkv = w_in_l[:, :3 * ATTN_WIDTH].astype(BF16)
        wf = jnp.pad(w_in_l[:, 3 * ATTN_WIDTH:3 * ATTN_WIDTH + ATTN_HEADS], ((0, 0), (0, LANES - ATTN_HEADS))).astype(BF16)
        bf = jnp.pad(b_fgate[l].astype(F32), (0, LANES - ATTN_HEADS)).reshape(1, LANES)
        wu = w_in_l[:, 3 * ATTN_WIDTH + ATTN_HEADS:].astype(BF16)
        return (norm_mix_g[l].reshape(1, D), wqkv, wf, bf, wu, sel)

    qt, kaug, vt, diff = _premix(x, mod_all[0], premix_consts(0), tm=tm_mix)
    for l in range(L):
        mod = mod_all[l]
        attn = _attention(qt, kaug, vt, tq=tq)

        wr = jnp.concatenate([
            jnp.pad(w_router_group[l].T, ((0, SUBLANES - N_EXPERT_GROUPS), (0, 0))),
            w_router_expert[l].transpose(0, 2, 1).reshape(N_EXPERTS, D)], axis=0)
        wr_hi = wr.astype(BF16)
        wr_lo = (wr - wr_hi.astype(F32)).astype(BF16)
        br = jnp.concatenate([jnp.pad(b_router_group[l], (0, SUBLANES - N_EXPERT_GROUPS)),
                              b_router_expert[l].reshape(N_EXPERTS)]).reshape(ROUTER_ROWS, 1).astype(F32)
        x1, xs, ids, wts, pos = _postmix(attn, diff, x, mod, w_pool[l].astype(BF16),
                                         pool_scale[l].reshape(1, POOL_WIDTH), w_out[l].astype(BF16),
                                         norm_ffn_g[l].reshape(1, D), wr_hi, wr_lo, br, tm=tm_mix)

        n_chunks, chunk_dst, chunk_src, tile_expert, tile_valid = _dispatch_tables(ids, tm=tm_mix, tm_e=tm_e, nt_e=nt_e)
        o_sorted = _experts(tile_expert, tile_valid, chunk_src, xs, wg_all, wu_all, wd_all, tm=tm_e, layer=l)
        if l == L - 1:
            (x,) = _combine(n_chunks, chunk_dst, o_sorted, x1, pos, wts, mod, tm=tm_mix,
                            g_final=norm_final_g.reshape(1, D))
        else:
            x, qt, kaug, vt, diff = _combine(n_chunks, chunk_dst, o_sorted, x1, pos, wts, mod, tm=tm_mix,
                                             next_premix=(mod_all[l + 1], premix_consts(l + 1)))
    return x
```

```python
import functools

import jax
import jax.numpy as jnp
import numpy as np
from jax import lax
from jax.experimental import pallas as pl
from jax.experimental.pallas import tpu as pltpu

ATTN_HEADS = 8
HEAD_DIM = 64
ATTN_WIDTH = ATTN_HEADS * HEAD_DIM
POOL_WINDOWS = (2, 4, 8, 16)
POOL_GROUP_DIM = 128
POOL_WIDTH = POOL_GROUP_DIM * len(POOL_WINDOWS)
POOL_HALO = 16
N_EXPERT_GROUPS = 4
EXPERTS_PER_GROUP = 8
N_EXPERTS = N_EXPERT_GROUPS * EXPERTS_PER_GROUP
N_MOD = 6
EPS = 1e-6
NEG_INF = -1e30
LOG2E = 1.4426950408889634

LANES = 128
SUBLANES = 8
AUG = 2 * LANES
DENOM_ROWS = 16
ATTN_UNROLL = 4
ROUTER_ROWS = 40
CHUNK = 16
TOP_K = 2
VMEM_LIMIT = 48 * 1024 * 1024


def _sorted_rows(tm):
    worst = TOP_K * tm + N_EXPERTS * (CHUNK - 1)
    return (worst // LANES + 1) * LANES

F32 = jnp.float32
BF16 = jnp.bfloat16


def _silu(a):
    return a * jax.nn.sigmoid(a)


def _nt_dot(a, b):
    return lax.dot_general(a, b, (((1,), (1,)), ((), ())), preferred_element_type=F32)


def _split3(a):
    t0 = a.astype(BF16)
    r1 = a - t0.astype(F32)
    t1 = r1.astype(BF16)
    t2 = (r1 - t1.astype(F32)).astype(BF16)
    return t0, t1, t2


def _rms_modulate(x, g, shift, scale):
    ms = jnp.mean(x * x, axis=-1, keepdims=True)
    y = x * lax.rsqrt(ms + EPS) * g
    return y * (1.0 + scale) + shift


def _ada_kernel(c_ref, w_ref, b_ref, o_ref):
    ca = _silu(c_ref[...])
    o_ref[0] = jnp.dot(ca, w_ref[0], precision=lax.Precision.HIGHEST, preferred_element_type=F32) + b_ref[0]


def _ada_modulation(c, w_ada, b_ada):
    L, D, W = w_ada.shape
    B = c.shape[0]
    tn = W // 4
    return pl.pallas_call(
        _ada_kernel,
        grid=(L, W // tn),
        in_specs=[
            pl.BlockSpec((B, D), lambda l, n: (0, 0)),
            pl.BlockSpec((1, D, tn), lambda l, n: (l, 0, n)),
            pl.BlockSpec((1, 1, tn), lambda l, n: (l, 0, n)),
        ],
        out_specs=pl.BlockSpec((1, B, tn), lambda l, n: (l, 0, n)),
        out_shape=jax.ShapeDtypeStruct((L, B, W), F32),
        compiler_params=pltpu.CompilerParams(vmem_limit_bytes=VMEM_LIMIT),
        name="ada_modulation",
    )(c, w_ada, b_ada.reshape(L, 1, W))


def _premix_kernel(x_ref, mod_ref, g_ref, wqkv_ref, wf_ref, bf_ref, wu_ref, sel_ref,
                   qt_ref, kaug_ref, vt_ref, diff_ref, carry_ref, ubuf_ref, *, tm):
    _premix_compute(x_ref[0], pl.program_id(1), mod_ref, g_ref, wqkv_ref, wf_ref, bf_ref, wu_ref, sel_ref,
                    qt_ref, kaug_ref, vt_ref, diff_ref, carry_ref, ubuf_ref, tm=tm)


def _premix_compute(x, si, mod_ref, g_ref, wqkv_ref, wf_ref, bf_ref, wu_ref, sel_ref,
                    qt_ref, kaug_ref, vt_ref, diff_ref, carry_ref, ubuf_ref, *, tm):
    @pl.when(si == 0)
    def _():
        carry_ref[...] = jnp.zeros_like(carry_ref)
        ubuf_ref[0:POOL_HALO, :] = jnp.zeros((POOL_HALO, POOL_WIDTH), F32)

    h = _rms_modulate(x, g_ref[...], mod_ref[0, 0:1, :], mod_ref[0, 1:2, :]).astype(BF16)
    u = jnp.dot(h, wu_ref[...], preferred_element_type=F32)
    fl = jnp.dot(h, wf_ref[...], preferred_element_type=F32) + bf_ref[...]
    qkv = jnp.dot(h, wqkv_ref[...], preferred_element_type=F32)

    ubuf_ref[POOL_HALO:POOL_HALO + tm, :] = u
    pos = si * tm + lax.broadcasted_iota(jnp.int32, (tm, POOL_GROUP_DIM), 0)
    diffs = []
    for g, w in enumerate(POOL_WINDOWS):
        c0 = g * POOL_GROUP_DIM
        ug = u[:, c0:c0 + POOL_GROUP_DIM]
        acc = ug
        for j in range(1, w):
            acc = acc + ubuf_ref[POOL_HALO - j:POOL_HALO - j + tm, c0:c0 + POOL_GROUP_DIM]
        cnt = jnp.minimum(pos + 1, w).astype(F32)
        diffs.append((acc / cnt - ug).astype(BF16))
    diff_ref[0] = jnp.concatenate(diffs, axis=1)
    ubuf_ref[0:POOL_HALO, :] = u[tm - POOL_HALO:, :]

    lf = jnp.minimum(fl, 0.0) - jnp.log1p(jnp.exp(-jnp.abs(fl)))
    row = lax.broadcasted_iota(jnp.int32, (tm, tm), 0)
    col = lax.broadcasted_iota(jnp.int32, (tm, tm), 1)
    tri = (row >= col).astype(BF16)
    cs = None
    for term in _split3(lf):
        d = jnp.dot(tri, term, preferred_element_type=F32)
        cs = d if cs is None else cs + d
    f_cum = cs + carry_ref[...]
    carry_ref[...] = f_cum[tm - 1:tm, :]

    aug = None
    for i, term in enumerate(_split3(-LOG2E * f_cum)):
        d = jnp.dot(term, sel_ref[i], preferred_element_type=F32)
        aug = d if aug is None else aug + d
    kaug_ref[0] = jnp.concatenate([qkv[:, ATTN_WIDTH:2 * ATTN_WIDTH], aug], axis=1).astype(BF16)
    qt_ref[0] = (qkv[:, :ATTN_WIDTH] * (LOG2E * HEAD_DIM ** -0.5)).T.astype(BF16)
    vt_ref[0] = qkv[:, 2 * ATTN_WIDTH:].T.astype(BF16)


def _premix_specs(B, S, D, tm, consts, bs):
    def at(f):
        return lambda *idx: f(*bs(*idx))

    row_spec = lambda w: pl.BlockSpec((1, tm, w), at(lambda b, s: (b, s, 0)))
    col_spec = pl.BlockSpec((1, ATTN_WIDTH, tm), at(lambda b, s: (b, 0, s)))
    const = lambda a: pl.BlockSpec(a.shape, lambda *idx: (0,) * a.ndim)
    kaug_w = ATTN_WIDTH + LANES
    in_specs = [pl.BlockSpec((1, N_MOD, D), at(lambda b, s: (b, 0, 0)))] + [const(a) for a in consts]
    out_specs = [col_spec, row_spec(kaug_w), col_spec, row_spec(POOL_WIDTH)]
    out_shape = [
        jax.ShapeDtypeStruct((B, ATTN_WIDTH, S), BF16),
        jax.ShapeDtypeStruct((B, S, kaug_w), BF16),
        jax.ShapeDtypeStruct((B, ATTN_WIDTH, S), BF16),
        jax.ShapeDtypeStruct((B, S, POOL_WIDTH), BF16),
    ]
    scratch = [pltpu.VMEM((1, LANES), F32), pltpu.VMEM((POOL_HALO + tm, POOL_WIDTH), F32)]
    return in_specs, out_specs, out_shape, scratch


def _premix(x, mod, consts, *, tm):
    B, S, D = x.shape
    in_specs, out_specs, out_shape, scratch = _premix_specs(B, S, D, tm, consts, lambda b, s: (b, s))
    return pl.pallas_call(
        functools.partial(_premix_kernel, tm=tm),
        grid=(B, S // tm),
        in_specs=[pl.BlockSpec((1, tm, D), lambda b, s: (b, s, 0))] + in_specs,
        out_specs=out_specs,
        out_shape=out_shape,
        scratch_shapes=scratch,
        compiler_params=pltpu.CompilerParams(
            dimension_semantics=("arbitrary", "arbitrary"), vmem_limit_bytes=VMEM_LIMIT),
        name="premix",
    )(x, mod, *consts)


def _forget_routing():
    sel = np.zeros((3, LANES, LANES), np.float32)
    for i in range(3):
        for h in range(ATTN_HEADS):
            sel[i, h, 3 * h + i] = 1.0
    return jnp.asarray(sel, BF16)


def _attn_kernel(qt_in_ref, kaug_ref, vt_ref, o_ref, qt_ref, m_ref, acc_ref, s_ref, mb_ref, *, tq, n_pairs):
    tk = tq
    qi = pl.program_id(1)
    n_heads = 2 * n_pairs
    @pl.when(qi == 0)
    def _():
        r128 = lax.broadcasted_iota(jnp.int32, (LANES, tq), 0)
        for h in range(n_heads):
            qt_ref[h, 0:LANES, :] = jnp.zeros((LANES, tq), BF16)
            qt_ref[h, LANES:AUG, :] = ((r128 >= 3 * h) & (r128 < 3 * h + 3)).astype(BF16)

    for h in range(n_heads):
        r0 = HEAD_DIM * (h % 2)
        qt_ref[h, r0:r0 + HEAD_DIM, :] = qt_in_ref[0, h * HEAD_DIM:(h + 1) * HEAD_DIM, :]
    m_ref[...] = jnp.full(m_ref.shape, NEG_INF, F32)
    acc_ref[...] = jnp.zeros(acc_ref.shape, F32)
    ones = jnp.ones((DENOM_ROWS, tk), BF16)

    def step(new=None, cur=None):
        if new is not None:
            jn, slot_n, masked = new
            k0n = pl.multiple_of(jn * tk, tk)
            f_terms = kaug_ref[0, pl.ds(k0n, tk), n_pairs * LANES:(n_pairs + 1) * LANES]
        if cur is not None:
            jc, slot_c = cur
            k0c = pl.multiple_of(jc * tk, tk)
        for h in range(n_heads):
            if new is not None:
                k_pair = kaug_ref[0, pl.ds(k0n, tk), (h // 2) * LANES:(h // 2 + 1) * LANES]
                s = jnp.dot(jnp.concatenate([k_pair, f_terms], axis=1), qt_ref[h],
                            preferred_element_type=F32)
                if masked:
                    key = lax.broadcasted_iota(jnp.int32, (tk, tq), 0)
                    qry = lax.broadcasted_iota(jnp.int32, (tk, tq), 1)
                    s = jnp.where(key <= qry, s, NEG_INF)
                s_ref[slot_n, h] = s
                mb_ref[slot_n, h] = jnp.max(s, axis=0, keepdims=True)
            if cur is not None:
                m_prev = m_ref[h]
                m_new = jnp.maximum(m_prev, mb_ref[slot_c, h])
                pt = jnp.exp2(s_ref[slot_c, h] - m_new).astype(BF16)
                alpha = jnp.exp2(m_prev - m_new)
                vtb = vt_ref[0, pl.ds(h * HEAD_DIM, HEAD_DIM), pl.ds(k0c, tk)]
                lhs = jnp.concatenate([vtb, ones], axis=0)
                acc_ref[h] = alpha * acc_ref[h] + jnp.dot(lhs, pt, preferred_element_type=F32)
                m_ref[h] = m_new

    @pl.when(qi == 0)
    def _():
        step(new=(0, 0, True))

    @pl.when(qi > 0)
    def _():
        step(new=(0, 0, False))

    def body(jj, c):
        j = ATTN_UNROLL * jj
        for k in range(1, ATTN_UNROLL + 1):
            step(new=(j + k, k % 2, False), cur=(j + k - 1, (k - 1) % 2))
        return c

    n_loops = jnp.maximum(qi - 1, 0) // ATTN_UNROLL
    lax.fori_loop(0, n_loops, body, 0)
    j0 = ATTN_UNROLL * n_loops
    rem = qi - j0

    @pl.when(rem == 0)
    def _():
        step(cur=(0, 0))

    for r in range(1, ATTN_UNROLL + 1):
        @pl.when(rem == r)
        def _(r=r):
            for k in range(1, r + 1):
                step(new=(j0 + k, k % 2, k == r), cur=(j0 + k - 1, (k - 1) % 2))
            step(cur=(qi, r % 2))

    for p in range(n_pairs):
        outs = []
        for hh in range(2):
            a = acc_ref[2 * p + hh]
            outs.append(a[:HEAD_DIM, :] / a[HEAD_DIM:HEAD_DIM + 1, :])
        o_ref[0, :, p * LANES:(p + 1) * LANES] = jnp.concatenate(outs, axis=0).T.astype(BF16)


def _attention(qt, kaug, vt, *, tq):
    B, W, S = qt.shape
    n_pairs = W // LANES
    return pl.pallas_call(
        functools.partial(_attn_kernel, tq=tq, n_pairs=n_pairs),
        grid=(B, S // tq),
        in_specs=[pl.BlockSpec((1, W, tq), lambda b, i: (b, 0, i)),
                  pl.BlockSpec((1, S, (n_pairs + 1) * LANES), lambda b, i: (b, 0, 0)),
                  pl.BlockSpec((1, W, S), lambda b, i: (b, 0, 0))],
        out_specs=pl.BlockSpec((1, tq, W), lambda b, i: (b, i, 0)),
        out_shape=jax.ShapeDtypeStruct((B, S, W), BF16),
        scratch_shapes=[pltpu.VMEM((2 * n_pairs, AUG, tq), BF16),
                        pltpu.VMEM((2 * n_pairs, 1, tq), F32),
                        pltpu.VMEM((2 * n_pairs, HEAD_DIM + DENOM_ROWS, tq), F32),
                        pltpu.VMEM((2, 2 * n_pairs, tq, tq), F32),
                        pltpu.VMEM((2, 2 * n_pairs, 1, tq), F32)],
        compiler_params=pltpu.CompilerParams(
            dimension_semantics=("arbitrary", "arbitrary"), vmem_limit_bytes=VMEM_LIMIT),
        name="fox_attention",
    )(qt, kaug, vt)


def _fold_pool_kernel(wpool_ref, pscale_ref, wout_ref, o_ref):
    o_ref[0, 0] = jnp.dot(wpool_ref[0, 0] * pscale_ref[0, 0], wout_ref[0, 0],
                          precision=lax.Precision.HIGHEST, preferred_element_type=F32)


def _fold_pool(w_pool, pool_scale, w_out):
    L, G, C, _ = w_pool.shape
    D = w_out.shape[-1]
    wout_pool = w_out[:, ATTN_WIDTH:, :].reshape(L, G, C, D)
    blk = lambda *shape: pl.BlockSpec((1, 1) + shape, lambda l, g: (l, g, 0, 0))
    return pl.pallas_call(
        _fold_pool_kernel,
        grid=(L, G),
        in_specs=[blk(C, C), blk(1, C), blk(C, D)],
        out_specs=blk(C, D),
        out_shape=jax.ShapeDtypeStruct((L, G, C, D), F32),
        name="fold_pool",
    )(w_pool, pool_scale.reshape(L, G, 1, C), wout_pool)


def _postmix_kernel(attn_ref, diff_ref, x_ref, mod_ref, wout_ref, g_ref,
                    wr_hi_ref, wr_lo_ref, br_ref, before_ref,
                    x1_ref, xs_ref, ids_ref, wts_ref, pos_ref, *, tm):
    cat = jnp.concatenate([attn_ref[0], diff_ref[0]], axis=1)
    mix = jnp.dot(cat, wout_ref[...], preferred_element_type=F32)
    x1 = x_ref[0] + mod_ref[0, 2:3, :] * mix
    x1_ref[0] = x1

    h = _rms_modulate(x1, g_ref[...], mod_ref[0, 3:4, :], mod_ref[0, 4:5, :])

    h_hi = h.astype(BF16)
    h_lo = (h - h_hi.astype(F32)).astype(BF16)
    logits = (_nt_dot(wr_hi_ref[...], h_hi) + _nt_dot(wr_lo_ref[...], h_hi) + _nt_dot(wr_hi_ref[...], h_lo)
              + br_ref[...])
    sub = lax.broadcasted_iota(jnp.int32, (SUBLANES, tm), 0)
    lg = jnp.where(sub < N_EXPERT_GROUPS, logits[0:SUBLANES, :], NEG_INF)
    g_max = jnp.max(lg, axis=0, keepdims=True)
    top_p = 1.0 / jnp.sum(jnp.exp(lg - g_max), axis=0, keepdims=True)
    top_g = jnp.min(jnp.where(lg == g_max, sub, SUBLANES), axis=0, keepdims=True)
    le = logits[SUBLANES:2 * SUBLANES, :]
    for g in range(1, N_EXPERT_GROUPS):
        le = jnp.where(top_g == g, logits[(g + 1) * SUBLANES:(g + 2) * SUBLANES, :], le)
    v1 = jnp.max(le, axis=0, keepdims=True)
    i1 = jnp.min(jnp.where(le == v1, sub, SUBLANES), axis=0, keepdims=True)
    le2 = jnp.where(sub == i1, NEG_INF, le)
    v2 = jnp.max(le2, axis=0, keepdims=True)
    i2 = jnp.min(jnp.where(le2 == v2, sub, SUBLANES), axis=0, keepdims=True)
    e2 = jnp.exp(v2 - v1)
    w1 = top_p / (1.0 + e2)
    id0 = top_g * EXPERTS_PER_GROUP + i1
    id1 = top_g * EXPERTS_PER_GROUP + i2
    ids_ref[...] = jnp.concatenate([id0, id1], axis=0)
    wts_ref[...] = jnp.concatenate([w1, w1 * e2], axis=0)

    sub_e = lax.broadcasted_iota(jnp.int32, (N_EXPERTS, tm), 0)
    onehot = jnp.concatenate([sub_e == id0, sub_e == id1], axis=1)
    oh_f = onehot.astype(F32)
    rank = jnp.dot(onehot.astype(BF16), before_ref[...], preferred_element_type=F32)
    chunks = jnp.floor((jnp.sum(oh_f, axis=1, keepdims=True) + (CHUNK - 1.0)) * (1.0 / CHUNK))
    er = lax.broadcasted_iota(jnp.int32, (N_EXPERTS, N_EXPERTS), 0)
    ec = lax.broadcasted_iota(jnp.int32, (N_EXPERTS, N_EXPERTS), 1)
    first_chunk = jnp.dot((er > ec).astype(BF16), jnp.broadcast_to(chunks, (N_EXPERTS, LANES)).astype(BF16),
                          preferred_element_type=F32)[:, 0:1]
    pos = jnp.sum(oh_f * (rank + CHUNK * first_chunk), axis=0, keepdims=True)
    pos0 = pos[:, :tm]
    pos1 = pos[:, tm:]
    pos_ref[...] = jnp.concatenate([pos0, pos1], axis=0)
    r_iota = lax.broadcasted_iota(jnp.int32, (xs_ref.shape[0], tm), 0)
    perm = ((r_iota == pos0.astype(jnp.int32)) | (r_iota == pos1.astype(jnp.int32))).astype(BF16)
    xs_ref[...] = jnp.dot(perm, h_hi, preferred_element_type=F32).astype(BF16)


def _postmix(attn, diff, x, mod, wout, g, wr_hi, wr_lo, br, *, tm):
    B, S, D = x.shape
    T = B * S
    nst = S // tm
    rows = _sorted_rows(tm)
    before = jnp.asarray(np.triu(np.ones((TOP_K * tm, TOP_K * tm), np.float32), k=1), BF16)
    row_spec = lambda w: pl.BlockSpec((1, tm, w), lambda b, s: (b, s, 0))
    const = lambda a: pl.BlockSpec(a.shape, lambda b, s: (0,) * a.ndim)
    tok_spec = pl.BlockSpec((TOP_K, tm), lambda b, s: (0, b * nst + s))
    return pl.pallas_call(
        functools.partial(_postmix_kernel, tm=tm),
        grid=(B, nst),
        in_specs=[row_spec(ATTN_WIDTH), row_spec(POOL_WIDTH), row_spec(D),
                  pl.BlockSpec((1, N_MOD, D), lambda b, s: (b, 0, 0)),
                  const(wout), const(g), const(wr_hi), const(wr_lo), const(br), const(before)],
        out_specs=[row_spec(D),
                   pl.BlockSpec((rows, D), lambda b, s: (b * nst + s, 0)),
                   tok_spec, tok_spec, tok_spec],
        out_shape=[
            jax.ShapeDtypeStruct((B, S, D), F32),
            jax.ShapeDtypeStruct((B * nst * rows, D), BF16),
            jax.ShapeDtypeStruct((TOP_K, T), jnp.int32),
            jax.ShapeDtypeStruct((TOP_K, T), F32),
            jax.ShapeDtypeStruct((TOP_K, T), F32),
        ],
        compiler_params=pltpu.CompilerParams(
            dimension_semantics=("arbitrary", "arbitrary"), vmem_limit_bytes=VMEM_LIMIT),
        name="postmix_router",
    )(attn, diff, x, mod, wout, g, wr_hi, wr_lo, br, before)


def _chunk_copy(src_hbm, src_chunk, dst, dst_chunk, sem):
    return pltpu.make_async_copy(src_hbm.at[src_chunk], dst.at[dst_chunk], sem)


def _gather_chunks(table_ref, tile, n_chunks, src_hbm, dst, sem):
    for c in range(n_chunks):
        _chunk_copy(src_hbm, table_ref[tile * n_chunks + c], dst, c, sem).start(priority=c % 2)


def _gather_wait(src_hbm, dst, sem):
    pltpu.make_async_copy(src_hbm.at[pl.ds(0, dst.shape[0])], dst, sem).wait()


def _expert_kernel(te_ref, tv_ref, src_ref, xs_hbm, wg_ref, wu_ref, wd_ref, o_ref, buf, sem, wg_b, wu_b, wd_b,
                   *, tm, nt):
    i = pl.program_id(0)
    n_chunks = tm // CHUNK
    slot = i % 2

    @pl.when(jnp.logical_and(i == 0, tv_ref[0] == 1))
    def _():
        _gather_chunks(src_ref, 0, n_chunks, xs_hbm, buf.at[0], sem.at[0])

    @pl.when(jnp.logical_or(i == 0, te_ref[i] != te_ref[jnp.maximum(i - 1, 0)]))
    def _():
        wg_b[...] = wg_ref[0].astype(BF16)
        wu_b[...] = wu_ref[0].astype(BF16)
        wd_b[...] = wd_ref[0].astype(BF16)

    @pl.when(tv_ref[i] == 1)
    def _():
        @pl.when(jnp.logical_and(i + 1 < nt, tv_ref[jnp.minimum(i + 1, nt - 1)] == 1))
        def _():
            _gather_chunks(src_ref, i + 1, n_chunks, xs_hbm, buf.at[1 - slot], sem.at[1 - slot])

        _gather_wait(xs_hbm, buf.at[slot], sem.at[slot])
        half = tm // 2
        gate_up = []
        for r0 in (0, half):
            x = buf[slot, r0 // CHUNK:(r0 + half) // CHUNK].reshape(half, buf.shape[-1])
            gate_up.append((jnp.dot(x, wg_b[...], preferred_element_type=F32),
                            jnp.dot(x, wu_b[...], preferred_element_type=F32)))
        for r0, (a, b) in zip((0, half), gate_up):
            act = (_silu(a) * b).astype(BF16)
            o_ref[r0:r0 + half, :] = jnp.dot(act, wd_b[...], preferred_element_type=F32).astype(BF16)

    @pl.when(tv_ref[i] == 0)
    def _():
        o_ref[...] = jnp.zeros(o_ref.shape, BF16)


def _experts(tile_expert, tile_valid, chunk_src, xs, wg, wu, wd, *, tm, layer):
    nt = tile_expert.shape[0]
    _, D, Fe = wg.shape
    expert = lambda i, te: layer * N_EXPERTS + te[i]
    grid_spec = pltpu.PrefetchScalarGridSpec(
        num_scalar_prefetch=3,
        grid=(nt,),
        in_specs=[
            pl.BlockSpec(memory_space=pl.ANY),
            pl.BlockSpec((1, D, Fe), lambda i, te, tv, cs: (expert(i, te), 0, 0)),
            pl.BlockSpec((1, D, Fe), lambda i, te, tv, cs: (expert(i, te), 0, 0)),
            pl.BlockSpec((1, Fe, D), lambda i, te, tv, cs: (expert(i, te), 0, 0)),
        ],
        out_specs=pl.BlockSpec((tm, D), lambda i, te, tv, cs: (i, 0)),
        scratch_shapes=[pltpu.VMEM((2, tm // CHUNK, CHUNK, D), BF16), pltpu.SemaphoreType.DMA((2,)),
                        pltpu.VMEM((D, Fe), BF16), pltpu.VMEM((D, Fe), BF16), pltpu.VMEM((Fe, D), BF16)],
    )
    return pl.pallas_call(
        functools.partial(_expert_kernel, tm=tm, nt=nt),
        grid_spec=grid_spec,
        out_shape=jax.ShapeDtypeStruct((nt * tm, D), BF16),
        compiler_params=pltpu.CompilerParams(dimension_semantics=("arbitrary",), vmem_limit_bytes=VMEM_LIMIT),
        name="moe_experts",
    )(tile_expert, tile_valid, chunk_src, xs.reshape(-1, CHUNK, D), wg, wu, wd)


def _combine_final_kernel(npair_ref, dst_ref, o_hbm, x1_ref, pos_ref, wts_ref, mod_ref, gf_ref, out_ref, buf, sem,
                          *, tm, nt):
    x2 = _combine_compute(npair_ref, dst_ref, o_hbm, x1_ref, pos_ref, wts_ref, mod_ref, buf, sem, tm=tm, nt=nt)
    ms = jnp.mean(x2 * x2, axis=-1, keepdims=True)
    out_ref[0] = x2 * lax.rsqrt(ms + EPS) * gf_ref[...]


def _combine_premix_kernel(npair_ref, dst_ref, o_hbm, x1_ref, pos_ref, wts_ref, mod_ref,
                           mod_next_ref, g_ref, wqkv_ref, wf_ref, bf_ref, wu_ref, sel_ref,
                           out_ref, qt_ref, kaug_ref, vt_ref, diff_ref, buf, sem, carry_ref, ubuf_ref, *, tm, nt, nst):
    x2 = _combine_compute(npair_ref, dst_ref, o_hbm, x1_ref, pos_ref, wts_ref, mod_ref, buf, sem, tm=tm, nt=nt)
    out_ref[0] = x2
    _premix_compute(x2, pl.program_id(0) % nst, mod_next_ref, g_ref, wqkv_ref, wf_ref, bf_ref, wu_ref, sel_ref,
                    qt_ref, kaug_ref, vt_ref, diff_ref, carry_ref, ubuf_ref, tm=tm)


def _combine_compute(npair_ref, dst_ref, o_hbm, x1_ref, pos_ref, wts_ref, mod_ref, buf, sem, *, tm, nt):
    i = pl.program_id(0)
    max_chunks = buf.shape[1]
    rows = max_chunks * CHUNK
    slot = i % 2

    def gather(t, s):
        def body(p, carry):
            for k in range(2):
                c = 2 * p + k
                _chunk_copy(o_hbm, dst_ref[t * max_chunks + c], buf.at[s], c, sem.at[s]).start(priority=k)
            return carry

        lax.fori_loop(0, npair_ref[t], body, 0)

    def gather_wait(t, s):
        def body(p, carry):
            for k in range(2):
                _chunk_copy(o_hbm, 0, buf.at[s], 2 * p + k, sem.at[s]).wait()
            return carry

        lax.fori_loop(0, npair_ref[t], body, 0)

    @pl.when(i == 0)
    def _():
        buf[...] = jnp.zeros(buf.shape, BF16)
        gather(0, 0)

    @pl.when(i + 1 < nt)
    def _():
        gather(jnp.minimum(i + 1, nt - 1), 1 - slot)

    gather_wait(i, slot)
    rep = lambda r: jnp.broadcast_to(r, (LANES, tm)).T
    p0, p1 = rep(pos_ref[0:1, :]), rep(pos_ref[1:2, :])
    w0, w1 = rep(wts_ref[0:1, :]), rep(wts_ref[1:2, :])
    lane = lax.broadcasted_iota(jnp.int32, (tm, LANES), 1).astype(F32)
    cols = []
    for c in range(rows // LANES):
        r = lane + float(c * LANES)
        cols.append((jnp.where(p0 == r, w0, 0.0) + jnp.where(p1 == r, w1, 0.0)).astype(BF16))
    comb = jnp.concatenate(cols, axis=1)
    y = jnp.dot(comb, buf[slot].reshape(rows, buf.shape[-1]), preferred_element_type=F32)
    return x1_ref[0] + mod_ref[0, 5:6, :] * y


def _combine(n_chunks, chunk_dst, o_sorted, x1, pos, wts, mod, *, tm, g_final=None, next_premix=None):
    B, S, D = x1.shape
    nst = S // tm
    nt = B * nst
    rows = _sorted_rows(tm)
    bs = lambda i, np_, cd: (i // nst, i % nst)
    x_spec = pl.BlockSpec((1, tm, D), lambda i, np_, cd: (i // nst, i % nst, 0))
    tok_spec = pl.BlockSpec((TOP_K, tm), lambda i, np_, cd: (0, i))
    in_specs = [pl.BlockSpec(memory_space=pl.ANY), x_spec, tok_spec, tok_spec,
                pl.BlockSpec((1, N_MOD, D), lambda i, np_, cd: (i // nst, 0, 0))]
    args = [o_sorted.reshape(-1, CHUNK, D), x1, pos, wts, mod]
    out_specs = [x_spec]
    out_shape = [jax.ShapeDtypeStruct((B, S, D), F32)]
    scratch = [pltpu.VMEM((2, rows // CHUNK, CHUNK, D), BF16), pltpu.SemaphoreType.DMA((2,))]
    if next_premix is None:
        body = functools.partial(_combine_final_kernel, tm=tm, nt=nt)
        in_specs.append(pl.BlockSpec((1, D), lambda i, np_, cd: (0, 0)))
        args.append(g_final)
        name = "moe_combine_final"
    else:
        mod_next, consts = next_premix
        body = functools.partial(_combine_premix_kernel, tm=tm, nt=nt, nst=nst)
        p_in, p_out, p_shape, p_scratch = _premix_specs(B, S, D, tm, consts, bs)
        in_specs += p_in
        args += [mod_next, *consts]
        out_specs += p_out
        out_shape += p_shape
        scratch += p_scratch
        name = "moe_combine_premix"
    grid_spec = pltpu.PrefetchScalarGridSpec(
        num_scalar_prefetch=2, grid=(nt,), in_specs=in_specs, out_specs=out_specs, scratch_shapes=scratch)
    return pl.pallas_call(
        body,
        grid_spec=grid_spec,
        out_shape=out_shape,
        compiler_params=pltpu.CompilerParams(dimension_semantics=("arbitrary",), vmem_limit_bytes=VMEM_LIMIT),
        name=name,
    )((n_chunks + 1) // 2, chunk_dst, *args)


def _dispatch_tables(ids, *, tm, tm_e, nt_e):
    T = ids.shape[1]
    nts = T // tm
    rows = _sorted_rows(tm)
    max_chunks = rows // CHUNK
    cpt = tm_e // CHUNK
    experts = jnp.arange(N_EXPERTS, dtype=jnp.int32)
    onehot = (ids.reshape(TOP_K, nts, tm)[..., None] == experts).astype(jnp.int32)
    seg_chunks = (jnp.sum(onehot, axis=(0, 2)) + CHUNK - 1) // CHUNK
    local_first = jnp.cumsum(seg_chunks, axis=1) - seg_chunks
    n_chunks = jnp.sum(seg_chunks, axis=1)
    expert_chunks = jnp.sum(seg_chunks, axis=0)
    region = ((expert_chunks + cpt - 1) // cpt) * cpt
    region_end = jnp.cumsum(region)
    seg_first = (region_end - region)[None, :] + jnp.cumsum(seg_chunks, axis=0) - seg_chunks
    ci = jnp.arange(max_chunks, dtype=jnp.int32)
    in_seg = (ci[None, :, None] >= local_first[:, None, :]) & (ci[None, :, None] < (local_first + seg_chunks)[:, None, :])
    gchunk = jnp.sum(in_seg * (seg_first - local_first)[:, None, :], axis=2) + ci[None, :]
    used = ci[None, :] < n_chunks[:, None]
    chunk_dst = jnp.where(used, gchunk, 0).reshape(-1).astype(jnp.int32)
    n_global = nt_e * cpt
    local_chunk = jnp.arange(nts, dtype=jnp.int32)[:, None] * max_chunks + ci[None, :]
    zero_chunk = max_chunks - 1
    chunk_src = jnp.full((n_global,), zero_chunk, jnp.int32).at[
        jnp.where(used, gchunk, n_global).reshape(-1)].set(local_chunk.reshape(-1), mode="drop")
    tile_start = jnp.arange(nt_e, dtype=jnp.int32) * cpt
    tile_expert = jnp.minimum(jnp.sum((tile_start[:, None] >= region_end[None, :]).astype(jnp.int32), axis=1),
                              N_EXPERTS - 1)
    tile_valid = (tile_start < region_end[-1]).astype(jnp.int32)
    return n_chunks.astype(jnp.int32), chunk_dst, chunk_src, tile_expert, tile_valid


def kernel(x, c, norm_mix_g, norm_ffn_g, norm_final_g, w_ada, b_ada, w_in, b_fgate, w_pool, pool_scale, w_out,
           w_router_group, b_router_group, w_router_expert, b_router_expert, w_expert_gate, w_expert_up,
           w_expert_down):
    B, S, D = x.shape
    L = w_ada.shape[0]
    T = B * S
    tm_mix = min(512, S)
    tq = min(256, S)
    tm_e = 512
    chunks_per_tile = tm_e // CHUNK
    max_used = (T // tm_mix) * (_sorted_rows(tm_mix) // CHUNK - 1) + N_EXPERTS * (chunks_per_tile - 1)
    nt_e = -(-max_used // chunks_per_tile)

    mod_all = _ada_modulation(c, w_ada, b_ada).reshape(L, B, N_MOD, D)
    Fe = w_expert_gate.shape[-1]
    wg_all = w_expert_gate.reshape(L * N_EXPERTS, D, Fe)
    wu_all = w_expert_up.reshape(L * N_EXPERTS, D, Fe)
    wd_all = w_expert_down.reshape(L * N_EXPERTS, Fe, D)
    sel = _forget_routing()
    pool_rows = _fold_pool(w_pool, pool_scale, w_out)

    def premix_consts(l):
        w_in_l = w_in[l]
        wqkv = w_in_l[:, :3 * ATTN_WIDTH].astype(BF16)
        wf = jnp.pad(w_in_l[:, 3 * ATTN_WIDTH:3 * ATTN_WIDTH + ATTN_HEADS], ((0, 0), (0, LANES - ATTN_HEADS))).astype(BF16)
        bf = jnp.pad(b_fgate[l].astype(F32), (0, LANES - ATTN_HEADS)).reshape(1, LANES)
        wu = w_in_l[:, 3 * ATTN_WIDTH + ATTN_HEADS:].astype(BF16)
        return (norm_mix_g[l].reshape(1, D), wqkv, wf, bf, wu, sel)

    qt, kaug, vt, diff = _premix(x, mod_all[0], premix_consts(0), tm=tm_mix)
    for l in range(L):
        mod = mod_all[l]
        attn = _attention(qt, kaug, vt, tq=tq)

        wr = jnp.concatenate([
            jnp.pad(w_router_group[l].T, ((0, SUBLANES - N_EXPERT_GROUPS), (0, 0))),
            w_router_expert[l].transpose(0, 2, 1).reshape(N_EXPERTS, D)], axis=0)
        wr_hi = wr.astype(BF16)
        wr_lo = (wr - wr_hi.astype(F32)).astype(BF16)
        br = jnp.concatenate([jnp.pad(b_router_group[l], (0, SUBLANES - N_EXPERT_GROUPS)),
                              b_router_expert[l].reshape(N_EXPERTS)]).reshape(ROUTER_ROWS, 1).astype(F32)
        wout = jnp.concatenate([w_out[l, :ATTN_WIDTH, :], pool_rows[l].reshape(POOL_WIDTH, D)], axis=0).astype(BF16)
        x1, xs, ids, wts, pos = _postmix(attn, diff, x, mod, wout, norm_ffn_g[l].reshape(1, D), wr_hi, wr_lo, br,
                                         tm=tm_mix)

        n_chunks, chunk_dst, chunk_src, tile_expert, tile_valid = _dispatch_tables(ids, tm=tm_mix, tm_e=tm_e, nt_e=nt_e)
        o_sorted = _experts(tile_expert, tile_valid, chunk_src, xs, wg_all, wu_all, wd_all, tm=tm_e, layer=l)
        if l == L - 1:
            (x,) = _combine(n_chunks, chunk_dst, o_sorted, x1, pos, wts, mod, tm=tm_mix,
                            g_final=norm_final_g.reshape(1, D))
        else:
            x, qt, kaug, vt, diff = _combine(n_chunks, chunk_dst, o_sorted, x1, pos, wts, mod, tm=tm_mix,
                                             next_premix=(mod_all[l + 1], premix_consts(l + 1)))
    return x
```

```python
import functools

import jax
import jax.numpy as jnp
import numpy as np
from jax import lax
from jax.experimental import pallas as pl
from jax.experimental.pallas import tpu as pltpu

ATTN_HEADS = 8
HEAD_DIM = 64
ATTN_WIDTH = ATTN_HEADS * HEAD_DIM
POOL_WINDOWS = (2, 4, 8, 16)
POOL_GROUP_DIM = 128
POOL_WIDTH = POOL_GROUP_DIM * len(POOL_WINDOWS)
POOL_HALO = 16
N_EXPERT_GROUPS = 4
EXPERTS_PER_GROUP = 8
N_EXPERTS = N_EXPERT_GROUPS * EXPERTS_PER_GROUP
N_MOD = 6
EPS = 1e-6
NEG_INF = -1e30
LOG2E = 1.4426950408889634

LANES = 128
SUBLANES = 8
AUG = 2 * LANES
DENOM_ROWS = 16
ATTN_UNROLL = 4
ROUTER_ROWS = 40
CHUNK = 16
TOP_K = 2
VMEM_LIMIT = 48 * 1024 * 1024


def _sorted_rows(tm):
    worst = TOP_K * tm + N_EXPERTS * (CHUNK - 1)
    return (worst // LANES + 1) * LANES

F32 = jnp.float32
BF16 = jnp.bfloat16


def _silu(a):
    return a * jax.nn.sigmoid(a)


def _nt_dot(a, b):
    return lax.dot_general(a, b, (((1,), (1,)), ((), ())), preferred_element_type=F32)


def _split3(a):
    t0 = a.astype(BF16)
    r1 = a - t0.astype(F32)
    t1 = r1.astype(BF16)
    t2 = (r1 - t1.astype(F32)).astype(BF16)
    return t0, t1, t2


def _rms_modulate(x, g, shift, scale):
    ms = jnp.mean(x * x, axis=-1, keepdims=True)
    y = x * lax.rsqrt(ms + EPS) * g
    return y * (1.0 + scale) + shift


def _ada_kernel(c_ref, w_ref, b_ref, o_ref):
    ca = _silu(c_ref[...])
    o_ref[0] = jnp.dot(ca, w_ref[0], precision=lax.Precision.HIGHEST, preferred_element_type=F32) + b_ref[0]


def _ada_modulation(c, w_ada, b_ada):
    L, D, W = w_ada.shape
    B = c.shape[0]
    tn = W // 4
    return pl.pallas_call(
        _ada_kernel,
        grid=(L, W // tn),
        in_specs=[
            pl.BlockSpec((B, D), lambda l, n: (0, 0)),
            pl.BlockSpec((1, D, tn), lambda l, n: (l, 0, n)),
            pl.BlockSpec((1, 1, tn), lambda l, n: (l, 0, n)),
        ],
        out_specs=pl.BlockSpec((1, B, tn), lambda l, n: (l, 0, n)),
        out_shape=jax.ShapeDtypeStruct((L, B, W), F32),
        compiler_params=pltpu.CompilerParams(vmem_limit_bytes=VMEM_LIMIT),
        name="ada_modulation",
    )(c, w_ada, b_ada.reshape(L, 1, W))


def _premix_kernel(x_ref, mod_ref, g_ref, wqkv_ref, wf_ref, bf_ref, wu_ref, sel_ref,
                   qt_ref, kaug_ref, vt_ref, diff_ref, carry_ref, ubuf_ref, *, tm):
    _premix_compute(x_ref[0], pl.program_id(1), mod_ref, g_ref, wqkv_ref, wf_ref, bf_ref, wu_ref, sel_ref,
                    qt_ref, kaug_ref, vt_ref, diff_ref, carry_ref, ubuf_ref, tm=tm)


def _premix_compute(x, si, mod_ref, g_ref, wqkv_ref, wf_ref, bf_ref, wu_ref, sel_ref,
                    qt_ref, kaug_ref, vt_ref, diff_ref, carry_ref, ubuf_ref, *, tm):
    @pl.when(si == 0)
    def _():
        carry_ref[...] = jnp.zeros_like(carry_ref)
        ubuf_ref[0:POOL_HALO, :] = jnp.zeros((POOL_HALO, POOL_WIDTH), F32)

    h = _rms_modulate(x, g_ref[...], mod_ref[0, 0:1, :], mod_ref[0, 1:2, :]).astype(BF16)
    u = jnp.dot(h, wu_ref[...], preferred_element_type=F32)
    fl = jnp.dot(h, wf_ref[...], preferred_element_type=F32) + bf_ref[...]
    qkv = jnp.dot(h, wqkv_ref[...], preferred_element_type=F32)

    ubuf_ref[POOL_HALO:POOL_HALO + tm, :] = u
    pos = si * tm + lax.broadcasted_iota(jnp.int32, (tm, POOL_GROUP_DIM), 0)
    diffs = []
    for g, w in enumerate(POOL_WINDOWS):
        c0 = g * POOL_GROUP_DIM
        ug = u[:, c0:c0 + POOL_GROUP_DIM]
        acc = ug
        for j in range(1, w):
            acc = acc + ubuf_ref[POOL_HALO - j:POOL_HALO - j + tm, c0:c0 + POOL_GROUP_DIM]
        cnt = jnp.minimum(pos + 1, w).astype(F32)
        diffs.append((acc / cnt - ug).astype(BF16))
    diff_ref[0] = jnp.concatenate(diffs, axis=1)
    ubuf_ref[0:POOL_HALO, :] = u[tm - POOL_HALO:, :]

    lf = jnp.minimum(fl, 0.0) - jnp.log1p(jnp.exp(-jnp.abs(fl)))
    row = lax.broadcasted_iota(jnp.int32, (LANES, LANES), 0)
    col = lax.broadcasted_iota(jnp.int32, (LANES, LANES), 1)
    tri = (row >= col).astype(BF16)
    terms = _split3(lf)
    run = carry_ref[...]
    blocks = []
    for b0 in range(0, tm, LANES):
        cs = None
        for term in terms:
            d = jnp.dot(tri, term[b0:b0 + LANES, :], preferred_element_type=F32)
            cs = d if cs is None else cs + d
        f_blk = cs + run
        run = f_blk[LANES - 1:LANES, :]
        blocks.append(f_blk)
    f_cum = jnp.concatenate(blocks, axis=0)
    carry_ref[...] = run

    aug = None
    for i, term in enumerate(_split3(-LOG2E * f_cum)):
        d = jnp.dot(term, sel_ref[i], preferred_element_type=F32)
        aug = d if aug is None else aug + d
    kaug_ref[0] = jnp.concatenate([qkv[:, ATTN_WIDTH:2 * ATTN_WIDTH], aug], axis=1).astype(BF16)
    qt_ref[0] = (qkv[:, :ATTN_WIDTH] * (LOG2E * HEAD_DIM ** -0.5)).T.astype(BF16)
    vt_ref[0] = qkv[:, 2 * ATTN_WIDTH:].T.astype(BF16)


def _premix_specs(B, S, D, tm, consts, bs):
    def at(f):
        return lambda *idx: f(*bs(*idx))

    row_spec = lambda w: pl.BlockSpec((1, tm, w), at(lambda b, s: (b, s, 0)))
    col_spec = pl.BlockSpec((1, ATTN_WIDTH, tm), at(lambda b, s: (b, 0, s)))
    const = lambda a: pl.BlockSpec(a.shape, lambda *idx: (0,) * a.ndim)
    kaug_w = ATTN_WIDTH + LANES
    in_specs = [pl.BlockSpec((1, N_MOD, D), at(lambda b, s: (b, 0, 0)))] + [const(a) for a in consts]
    out_specs = [col_spec, row_spec(kaug_w), col_spec, row_spec(POOL_WIDTH)]
    out_shape = [
        jax.ShapeDtypeStruct((B, ATTN_WIDTH, S), BF16),
        jax.ShapeDtypeStruct((B, S, kaug_w), BF16),
        jax.ShapeDtypeStruct((B, ATTN_WIDTH, S), BF16),
        jax.ShapeDtypeStruct((B, S, POOL_WIDTH), BF16),
    ]
    scratch = [pltpu.VMEM((1, LANES), F32), pltpu.VMEM((POOL_HALO + tm, POOL_WIDTH), F32)]
    return in_specs, out_specs, out_shape, scratch


def _premix(x, mod, consts, *, tm):
    B, S, D = x.shape
    in_specs, out_specs, out_shape, scratch = _premix_specs(B, S, D, tm, consts, lambda b, s: (b, s))
    return pl.pallas_call(
        functools.partial(_premix_kernel, tm=tm),
        grid=(B, S // tm),
        in_specs=[pl.BlockSpec((1, tm, D), lambda b, s: (b, s, 0))] + in_specs,
        out_specs=out_specs,
        out_shape=out_shape,
        scratch_shapes=scratch,
        compiler_params=pltpu.CompilerParams(
            dimension_semantics=("arbitrary", "arbitrary"), vmem_limit_bytes=VMEM_LIMIT),
        name="premix",
    )(x, mod, *consts)


def _forget_routing():
    sel = np.zeros((3, LANES, LANES), np.float32)
    for i in range(3):
        for h in range(ATTN_HEADS):
            sel[i, h, 3 * h + i] = 1.0
    return jnp.asarray(sel, BF16)


def _attn_kernel(qt_in_ref, kaug_ref, vt_ref, o_ref, qt_ref, m_ref, acc_ref, s_ref, mb_ref, *, tq, n_pairs):
    tk = tq
    qi = pl.program_id(1)
    n_heads = 2 * n_pairs
    @pl.when(qi == 0)
    def _():
        r128 = lax.broadcasted_iota(jnp.int32, (LANES, tq), 0)
        for h in range(n_heads):
            qt_ref[h, 0:LANES, :] = jnp.zeros((LANES, tq), BF16)
            qt_ref[h, LANES:AUG, :] = ((r128 >= 3 * h) & (r128 < 3 * h + 3)).astype(BF16)

    for h in range(n_heads):
        r0 = HEAD_DIM * (h % 2)
        qt_ref[h, r0:r0 + HEAD_DIM, :] = qt_in_ref[0, h * HEAD_DIM:(h + 1) * HEAD_DIM, :]
    m_ref[...] = jnp.full(m_ref.shape, NEG_INF, F32)
    acc_ref[...] = jnp.zeros(acc_ref.shape, F32)
    ones = jnp.ones((DENOM_ROWS, tk), BF16)

    def step(new=None, cur=None):
        if new is not None:
            jn, slot_n, masked = new
            k0n = pl.multiple_of(jn * tk, tk)
            f_terms = kaug_ref[0, pl.ds(k0n, tk), n_pairs * LANES:(n_pairs + 1) * LANES]
        if cur is not None:
            jc, slot_c = cur
            k0c = pl.multiple_of(jc * tk, tk)
        for h in range(n_heads):
            if new is not None:
                k_pair = kaug_ref[0, pl.ds(k0n, tk), (h // 2) * LANES:(h // 2 + 1) * LANES]
                s = jnp.dot(jnp.concatenate([k_pair, f_terms], axis=1), qt_ref[h],
                            preferred_element_type=F32)
                if masked:
                    key = lax.broadcasted_iota(jnp.int32, (tk, tq), 0)
                    qry = lax.broadcasted_iota(jnp.int32, (tk, tq), 1)
                    s = jnp.where(key <= qry, s, NEG_INF)
                s_ref[slot_n, h] = s
                mb_ref[slot_n, h] = jnp.max(s, axis=0, keepdims=True)
            if cur is not None:
                m_prev = m_ref[h]
                m_new = jnp.maximum(m_prev, mb_ref[slot_c, h])
                pt = jnp.exp2(s_ref[slot_c, h] - m_new).astype(BF16)
                alpha = jnp.exp2(m_prev - m_new)
                vtb = vt_ref[0, pl.ds(h * HEAD_DIM, HEAD_DIM), pl.ds(k0c, tk)]
                lhs = jnp.concatenate([vtb, ones], axis=0)
                acc_ref[h] = alpha * acc_ref[h] + jnp.dot(lhs, pt, preferred_element_type=F32)
                m_ref[h] = m_new

    @pl.when(qi == 0)
    def _():
        step(new=(0, 0, True))

    @pl.when(qi > 0)
    def _():
        step(new=(0, 0, False))

    def body(jj, c):
        j = ATTN_UNROLL * jj
        for k in range(1, ATTN_UNROLL + 1):
            step(new=(j + k, k % 2, False), cur=(j + k - 1, (k - 1) % 2))
        return c

    n_loops = jnp.maximum(qi - 1, 0) // ATTN_UNROLL
    lax.fori_loop(0, n_loops, body, 0)
    j0 = ATTN_UNROLL * n_loops
    rem = qi - j0

    @pl.when(rem == 0)
    def _():
        step(cur=(0, 0))

    for r in range(1, ATTN_UNROLL + 1):
        @pl.when(rem == r)
        def _(r=r):
            for k in range(1, r + 1):
                step(new=(j0 + k, k % 2, k == r), cur=(j0 + k - 1, (k - 1) % 2))
            step(cur=(qi, r % 2))

    for p in range(n_pairs):
        outs = []
        for hh in range(2):
            a = acc_ref[2 * p + hh]
            outs.append(a[:HEAD_DIM, :] / a[HEAD_DIM:HEAD_DIM + 1, :])
        o_ref[0, :, p * LANES:(p + 1) * LANES] = jnp.concatenate(outs, axis=0).T.astype(BF16)


def _attention(qt, kaug, vt, *, tq):
    B, W, S = qt.shape
    n_pairs = W // LANES
    return pl.pallas_call(
        functools.partial(_attn_kernel, tq=tq, n_pairs=n_pairs),
        grid=(B, S // tq),
        in_specs=[pl.BlockSpec((1, W, tq), lambda b, i: (b, 0, i)),
                  pl.BlockSpec((1, S, (n_pairs + 1) * LANES), lambda b, i: (b, 0, 0)),
                  pl.BlockSpec((1, W, S), lambda b, i: (b, 0, 0))],
        out_specs=pl.BlockSpec((1, tq, W), lambda b, i: (b, i, 0)),
        out_shape=jax.ShapeDtypeStruct((B, S, W), BF16),
        scratch_shapes=[pltpu.VMEM((2 * n_pairs, AUG, tq), BF16),
                        pltpu.VMEM((2 * n_pairs, 1, tq), F32),
                        pltpu.VMEM((2 * n_pairs, HEAD_DIM + DENOM_ROWS, tq), F32),
                        pltpu.VMEM((2, 2 * n_pairs, tq, tq), F32),
                        pltpu.VMEM((2, 2 * n_pairs, 1, tq), F32)],
        compiler_params=pltpu.CompilerParams(
            dimension_semantics=("arbitrary", "arbitrary"), vmem_limit_bytes=VMEM_LIMIT),
        name="fox_attention",
    )(qt, kaug, vt)


def _postmix_kernel(attn_ref, diff_ref, x_ref, mod_ref, wpool_ref, pscale_ref, wout_ref, g_ref,
                    wr_hi_ref, wr_lo_ref, br_ref, before_ref,
                    x1_ref, xs_ref, ids_ref, wts_ref, pos_ref, *, tm):
    pooled = []
    for g in range(len(POOL_WINDOWS)):
        c0 = g * POOL_GROUP_DIM
        pooled.append(jnp.dot(diff_ref[0, :, c0:c0 + POOL_GROUP_DIM], wpool_ref[g], preferred_element_type=F32))
    pool_out = (jnp.concatenate(pooled, axis=1) * pscale_ref[...]).astype(BF16)
    cat = jnp.concatenate([attn_ref[0], pool_out], axis=1)
    mix = jnp.dot(cat, wout_ref[...], preferred_element_type=F32)
    x1 = x_ref[0] + mod_ref[0, 2:3, :] * mix
    x1_ref[0] = x1

    h = _rms_modulate(x1, g_ref[...], mod_ref[0, 3:4, :], mod_ref[0, 4:5, :])

    h_hi = h.astype(BF16)
    h_lo = (h - h_hi.astype(F32)).astype(BF16)
    logits = (_nt_dot(wr_hi_ref[...], h_hi) + _nt_dot(wr_lo_ref[...], h_hi) + _nt_dot(wr_hi_ref[...], h_lo)
              + br_ref[...])
    sub = lax.broadcasted_iota(jnp.int32, (SUBLANES, tm), 0)
    lg = jnp.where(sub < N_EXPERT_GROUPS, logits[0:SUBLANES, :], NEG_INF)
    g_max = jnp.max(lg, axis=0, keepdims=True)
    top_p = 1.0 / jnp.sum(jnp.exp(lg - g_max), axis=0, keepdims=True)
    top_g = jnp.min(jnp.where(lg == g_max, sub, SUBLANES), axis=0, keepdims=True)
    le = logits[SUBLANES:2 * SUBLANES, :]
    for g in range(1, N_EXPERT_GROUPS):
        le = jnp.where(top_g == g, logits[(g + 1) * SUBLANES:(g + 2) * SUBLANES, :], le)
    v1 = jnp.max(le, axis=0, keepdims=True)
    i1 = jnp.min(jnp.where(le == v1, sub, SUBLANES), axis=0, keepdims=True)
    le2 = jnp.where(sub == i1, NEG_INF, le)
    v2 = jnp.max(le2, axis=0, keepdims=True)
    i2 = jnp.min(jnp.where(le2 == v2, sub, SUBLANES), axis=0, keepdims=True)
    e2 = jnp.exp(v2 - v1)
    w1 = top_p / (1.0 + e2)
    id0 = top_g * EXPERTS_PER_GROUP + i1
    id1 = top_g * EXPERTS_PER_GROUP + i2
    ids_ref[...] = jnp.concatenate([id0, id1], axis=0)
    wts_ref[...] = jnp.concatenate([w1, w1 * e2], axis=0)

    sub_e = lax.broadcasted_iota(jnp.int32, (N_EXPERTS, tm), 0)
    onehot = jnp.concatenate([sub_e == id0, sub_e == id1], axis=1)
    oh_f = onehot.astype(F32)
    rank = jnp.dot(onehot.astype(BF16), before_ref[...], preferred_element_type=F32)
    chunks = jnp.floor((jnp.sum(oh_f, axis=1, keepdims=True) + (CHUNK - 1.0)) * (1.0 / CHUNK))
    er = lax.broadcasted_iota(jnp.int32, (N_EXPERTS, N_EXPERTS), 0)
    ec = lax.broadcasted_iota(jnp.int32, (N_EXPERTS, N_EXPERTS), 1)
    first_chunk = jnp.dot((er > ec).astype(BF16), jnp.broadcast_to(chunks, (N_EXPERTS, LANES)).astype(BF16),
                          preferred_element_type=F32)[:, 0:1]
    pos = jnp.sum(oh_f * (rank + CHUNK * first_chunk), axis=0, keepdims=True)
    pos0 = pos[:, :tm]
    pos1 = pos[:, tm:]
    pos_ref[...] = jnp.concatenate([pos0, pos1], axis=0)
    r_iota = lax.broadcasted_iota(jnp.int32, (xs_ref.shape[0], tm), 0)
    perm = ((r_iota == pos0.astype(jnp.int32)) | (r_iota == pos1.astype(jnp.int32))).astype(BF16)
    xs_ref[...] = jnp.dot(perm, h_hi, preferred_element_type=F32).astype(BF16)


def _postmix(attn, diff, x, mod, wpool, pscale, wout, g, wr_hi, wr_lo, br, *, tm):
    B, S, D = x.shape
    T = B * S
    nst = S // tm
    rows = _sorted_rows(tm)
    before = jnp.asarray(np.triu(np.ones((TOP_K * tm, TOP_K * tm), np.float32), k=1), BF16)
    row_spec = lambda w: pl.BlockSpec((1, tm, w), lambda b, s: (b, s, 0))
    const = lambda a: pl.BlockSpec(a.shape, lambda b, s: (0,) * a.ndim)
    tok_spec = pl.BlockSpec((TOP_K, tm), lambda b, s: (0, b * nst + s))
    return pl.pallas_call(
        functools.partial(_postmix_kernel, tm=tm),
        grid=(B, nst),
        in_specs=[row_spec(ATTN_WIDTH), row_spec(POOL_WIDTH), row_spec(D),
                  pl.BlockSpec((1, N_MOD, D), lambda b, s: (b, 0, 0)),
                  const(wpool), const(pscale), const(wout), const(g), const(wr_hi), const(wr_lo), const(br),
                  const(before)],
        out_specs=[row_spec(D),
                   pl.BlockSpec((rows, D), lambda b, s: (b * nst + s, 0)),
                   tok_spec, tok_spec, tok_spec],
        out_shape=[
            jax.ShapeDtypeStruct((B, S, D), F32),
            jax.ShapeDtypeStruct((B * nst * rows, D), BF16),
            jax.ShapeDtypeStruct((TOP_K, T), jnp.int32),
            jax.ShapeDtypeStruct((TOP_K, T), F32),
            jax.ShapeDtypeStruct((TOP_K, T), F32),
        ],
        compiler_params=pltpu.CompilerParams(
            dimension_semantics=("arbitrary", "arbitrary"), vmem_limit_bytes=VMEM_LIMIT),
        name="postmix_router",
    )(attn, diff, x, mod, wpool, pscale, wout, g, wr_hi, wr_lo, br, before)


def _chunk_copy(src_hbm, src_chunk, dst, dst_chunk, sem):
    return pltpu.make_async_copy(src_hbm.at[src_chunk], dst.at[dst_chunk], sem)


def _gather_chunks(table_ref, tile, n_chunks, src_hbm, dst, sem):
    for c in range(n_chunks):
        _chunk_copy(src_hbm, table_ref[tile * n_chunks + c], dst, c, sem).start(priority=c % 2)


def _gather_wait(src_hbm, dst, sem):
    pltpu.make_async_copy(src_hbm.at[pl.ds(0, dst.shape[0])], dst, sem).wait()


def _expert_kernel(te_ref, tv_ref, src_ref, xs_hbm, wg_ref, wu_ref, wd_ref, o_ref, buf, sem, wg_b, wu_b, wd_b,
                   *, tm, nt):
    i = pl.program_id(0)
    n_chunks = tm // CHUNK
    slot = i % 2

    @pl.when(jnp.logical_and(i == 0, tv_ref[0] == 1))
    def _():
        _gather_chunks(src_ref, 0, n_chunks, xs_hbm, buf.at[0], sem.at[0])

    @pl.when(jnp.logical_or(i == 0, te_ref[i] != te_ref[jnp.maximum(i - 1, 0)]))
    def _():
        wg_b[...] = wg_ref[0].astype(BF16)
        wu_b[...] = wu_ref[0].astype(BF16)
        wd_b[...] = wd_ref[0].astype(BF16)

    @pl.when(tv_ref[i] == 1)
    def _():
        @pl.when(jnp.logical_and(i + 1 < nt, tv_ref[jnp.minimum(i + 1, nt - 1)] == 1))
        def _():
            _gather_chunks(src_ref, i + 1, n_chunks, xs_hbm, buf.at[1 - slot], sem.at[1 - slot])

        _gather_wait(xs_hbm, buf.at[slot], sem.at[slot])
        half = tm // 2
        gate_up = []
        for r0 in (0, half):
            x = buf[slot, r0 // CHUNK:(r0 + half) // CHUNK].reshape(half, buf.shape[-1])
            gate_up.append((jnp.dot(x, wg_b[...], preferred_element_type=F32),
                            jnp.dot(x, wu_b[...], preferred_element_type=F32)))
        for r0, (a, b) in zip((0, half), gate_up):
            act = (_silu(a) * b).astype(BF16)
            o_ref[r0:r0 + half, :] = jnp.dot(act, wd_b[...], preferred_element_type=F32).astype(BF16)

    @pl.when(tv_ref[i] == 0)
    def _():
        o_ref[...] = jnp.zeros(o_ref.shape, BF16)


def _experts(tile_expert, tile_valid, chunk_src, xs, wg, wu, wd, *, tm, layer):
    nt = tile_expert.shape[0]
    _, D, Fe = wg.shape
    expert = lambda i, te: layer * N_EXPERTS + te[i]
    grid_spec = pltpu.PrefetchScalarGridSpec(
        num_scalar_prefetch=3,
        grid=(nt,),
        in_specs=[
            pl.BlockSpec(memory_space=pl.ANY),
            pl.BlockSpec((1, D, Fe), lambda i, te, tv, cs: (expert(i, te), 0, 0)),
            pl.BlockSpec((1, D, Fe), lambda i, te, tv, cs: (expert(i, te), 0, 0)),
            pl.BlockSpec((1, Fe, D), lambda i, te, tv, cs: (expert(i, te), 0, 0)),
        ],
        out_specs=pl.BlockSpec((tm, D), lambda i, te, tv, cs: (i, 0)),
        scratch_shapes=[pltpu.VMEM((2, tm // CHUNK, CHUNK, D), BF16), pltpu.SemaphoreType.DMA((2,)),
                        pltpu.VMEM((D, Fe), BF16), pltpu.VMEM((D, Fe), BF16), pltpu.VMEM((Fe, D), BF16)],
    )
    return pl.pallas_call(
        functools.partial(_expert_kernel, tm=tm, nt=nt),
        grid_spec=grid_spec,
        out_shape=jax.ShapeDtypeStruct((nt * tm, D), BF16),
        compiler_params=pltpu.CompilerParams(dimension_semantics=("arbitrary",), vmem_limit_bytes=VMEM_LIMIT),
        name="moe_experts",
    )(tile_expert, tile_valid, chunk_src, xs.reshape(-1, CHUNK, D), wg, wu, wd)


def _combine_final_kernel(npair_ref, dst_ref, o_hbm, x1_ref, pos_ref, wts_ref, mod_ref, gf_ref, out_ref, buf, sem,
                          *, tm, nt):
    x2 = _combine_compute(npair_ref, dst_ref, o_hbm, x1_ref, pos_ref, wts_ref, mod_ref, buf, sem, tm=tm, nt=nt)
    ms = jnp.mean(x2 * x2, axis=-1, keepdims=True)
    out_ref[0] = x2 * lax.rsqrt(ms + EPS) * gf_ref[...]


def _combine_premix_kernel(npair_ref, dst_ref, o_hbm, x1_ref, pos_ref, wts_ref, mod_ref,
                           mod_next_ref, g_ref, wqkv_ref, wf_ref, bf_ref, wu_ref, sel_ref,
                           out_ref, qt_ref, kaug_ref, vt_ref, diff_ref, buf, sem, carry_ref, ubuf_ref, *, tm, nt, nst):
    x2 = _combine_compute(npair_ref, dst_ref, o_hbm, x1_ref, pos_ref, wts_ref, mod_ref, buf, sem, tm=tm, nt=nt)
    out_ref[0] = x2
    _premix_compute(x2, pl.program_id(0) % nst, mod_next_ref, g_ref, wqkv_ref, wf_ref, bf_ref, wu_ref, sel_ref,
                    qt_ref, kaug_ref, vt_ref, diff_ref, carry_ref, ubuf_ref, tm=tm)


def _combine_compute(npair_ref, dst_ref, o_hbm, x1_ref, pos_ref, wts_ref, mod_ref, buf, sem, *, tm, nt):
    i = pl.program_id(0)
    max_chunks = buf.shape[1]
    rows = max_chunks * CHUNK
    slot = i % 2

    def gather(t, s):
        def body(p, carry):
            for k in range(2):
                c = 2 * p + k
                _chunk_copy(o_hbm, dst_ref[t * max_chunks + c], buf.at[s], c, sem.at[s]).start(priority=k)
            return carry

        lax.fori_loop(0, npair_ref[t], body, 0)

    def gather_wait(t, s):
        def body(p, carry):
            for k in range(2):
                _chunk_copy(o_hbm, 0, buf.at[s], 2 * p + k, sem.at[s]).wait()
            return carry

        lax.fori_loop(0, npair_ref[t], body, 0)

    @pl.when(i == 0)
    def _():
        buf[...] = jnp.zeros(buf.shape, BF16)
        gather(0, 0)

    @pl.when(i + 1 < nt)
    def _():
        gather(jnp.minimum(i + 1, nt - 1), 1 - slot)

    gather_wait(i, slot)
    rep = lambda r: jnp.broadcast_to(r, (LANES, tm)).T
    p0, p1 = rep(pos_ref[0:1, :]), rep(pos_ref[1:2, :])
    w0, w1 = rep(wts_ref[0:1, :]), rep(wts_ref[1:2, :])
    lane = lax.broadcasted_iota(jnp.int32, (tm, LANES), 1).astype(F32)
    cols = []
    for c in range(rows // LANES):
        r = lane + float(c * LANES)
        cols.append((jnp.where(p0 == r, w0, 0.0) + jnp.where(p1 == r, w1, 0.0)).astype(BF16))
    comb = jnp.concatenate(cols, axis=1)
    y = jnp.dot(comb, buf[slot].reshape(rows, buf.shape[-1]), preferred_element_type=F32)
    return x1_ref[0] + mod_ref[0, 5:6, :] * y


def _combine(n_chunks, chunk_dst, o_sorted, x1, pos, wts, mod, *, tm, g_final=None, next_premix=None):
    B, S, D = x1.shape
    nst = S // tm
    nt = B * nst
    rows = _sorted_rows(tm)
    bs = lambda i, np_, cd: (i // nst, i % nst)
    x_spec = pl.BlockSpec((1, tm, D), lambda i, np_, cd: (i // nst, i % nst, 0))
    tok_spec = pl.BlockSpec((TOP_K, tm), lambda i, np_, cd: (0, i))
    in_specs = [pl.BlockSpec(memory_space=pl.ANY), x_spec, tok_spec, tok_spec,
                pl.BlockSpec((1, N_MOD, D), lambda i, np_, cd: (i // nst, 0, 0))]
    args = [o_sorted.reshape(-1, CHUNK, D), x1, pos, wts, mod]
    out_specs = [x_spec]
    out_shape = [jax.ShapeDtypeStruct((B, S, D), F32)]
    scratch = [pltpu.VMEM((2, rows // CHUNK, CHUNK, D), BF16), pltpu.SemaphoreType.DMA((2,))]
    if next_premix is None:
        body = functools.partial(_combine_final_kernel, tm=tm, nt=nt)
        in_specs.append(pl.BlockSpec((1, D), lambda i, np_, cd: (0, 0)))
        args.append(g_final)
        name = "moe_combine_final"
    else:
        mod_next, consts = next_premix
        body = functools.partial(_combine_premix_kernel, tm=tm, nt=nt, nst=nst)
        p_in, p_out, p_shape, p_scratch = _premix_specs(B, S, D, tm, consts, bs)
        in_specs += p_in
        args += [mod_next, *consts]
        out_specs += p_out
        out_shape += p_shape
        scratch += p_scratch
        name = "moe_combine_premix"
    grid_spec = pltpu.PrefetchScalarGridSpec(
        num_scalar_prefetch=2, grid=(nt,), in_specs=in_specs, out_specs=out_specs, scratch_shapes=scratch)
    return pl.pallas_call(
        body,
        grid_spec=grid_spec,
        out_shape=out_shape,
        compiler_params=pltpu.CompilerParams(dimension_semantics=("arbitrary",), vmem_limit_bytes=VMEM_LIMIT),
        name=name,
    )((n_chunks + 1) // 2, chunk_dst, *args)


def _dispatch_tables(ids, *, tm, tm_e, nt_e):
    T = ids.shape[1]
    nts = T // tm
    rows = _sorted_rows(tm)
    max_chunks = rows // CHUNK
    cpt = tm_e // CHUNK
    experts = jnp.arange(N_EXPERTS, dtype=jnp.int32)
    onehot = (ids.reshape(TOP_K, nts, tm)[..., None] == experts).astype(jnp.int32)
    seg_chunks = (jnp.sum(onehot, axis=(0, 2)) + CHUNK - 1) // CHUNK
    local_first = jnp.cumsum(seg_chunks, axis=1) - seg_chunks
    n_chunks = jnp.sum(seg_chunks, axis=1)
    expert_chunks = jnp.sum(seg_chunks, axis=0)
    region = ((expert_chunks + cpt - 1) // cpt) * cpt
    region_end = jnp.cumsum(region)
    seg_first = (region_end - region)[None, :] + jnp.cumsum(seg_chunks, axis=0) - seg_chunks
    ci = jnp.arange(max_chunks, dtype=jnp.int32)
    in_seg = (ci[None, :, None] >= local_first[:, None, :]) & (ci[None, :, None] < (local_first + seg_chunks)[:, None, :])
    gchunk = jnp.sum(in_seg * (seg_first - local_first)[:, None, :], axis=2) + ci[None, :]
    used = ci[None, :] < n_chunks[:, None]
    chunk_dst = jnp.where(used, gchunk, 0).reshape(-1).astype(jnp.int32)
    n_global = nt_e * cpt
    local_chunk = jnp.arange(nts, dtype=jnp.int32)[:, None] * max_chunks + ci[None, :]
    zero_chunk = max_chunks - 1
    chunk_src = jnp.full((n_global,), zero_chunk, jnp.int32).at[
        jnp.where(used, gchunk, n_global).reshape(-1)].set(local_chunk.reshape(-1), mode="drop")
    tile_start = jnp.arange(nt_e, dtype=jnp.int32) * cpt
    tile_expert = jnp.minimum(jnp.sum((tile_start[:, None] >= region_end[None, :]).astype(jnp.int32), axis=1),
                              N_EXPERTS - 1)
    tile_valid = (tile_start < region_end[-1]).astype(jnp.int32)
    return n_chunks.astype(jnp.int32), chunk_dst, chunk_src, tile_expert, tile_valid


def kernel(x, c, norm_mix_g, norm_ffn_g, norm_final_g, w_ada, b_ada, w_in, b_fgate, w_pool, pool_scale, w_out,
           w_router_group, b_router_group, w_router_expert, b_router_expert, w_expert_gate, w_expert_up,
           w_expert_down):
    B, S, D = x.shape
    L = w_ada.shape[0]
    T = B * S
    tm_mix = min(512, S)
    tq = min(256, S)
    tm_e = 512
    chunks_per_tile = tm_e // CHUNK
    max_used = (T // tm_mix) * (_sorted_rows(tm_mix) // CHUNK - 1) + N_EXPERTS * (chunks_per_tile - 1)
    nt_e = -(-max_used // chunks_per_tile)

    mod_all = _ada_modulation(c, w_ada, b_ada).reshape(L, B, N_MOD, D)
    Fe = w_expert_gate.shape[-1]
    wg_all = w_expert_gate.reshape(L * N_EXPERTS, D, Fe)
    wu_all = w_expert_up.reshape(L * N_EXPERTS, D, Fe)
    wd_all = w_expert_down.reshape(L * N_EXPERTS, Fe, D)
    sel = _forget_routing()

    def premix_consts(l):
        w_in_l = w_in[l]
        wqkv = w_in_l[:, :3 * ATTN_WIDTH].astype(BF16)
        wf = jnp.pad(w_in_l[:, 3 * ATTN_WIDTH:3 * ATTN_WIDTH + ATTN_HEADS], ((0, 0), (0, LANES - ATTN_HEADS))).astype(BF16)
        bf = jnp.pad(b_fgate[l].astype(F32), (0, LANES - ATTN_HEADS)).reshape(1, LANES)
        wu = w_in_l[:, 3 * ATTN_WIDTH + ATTN_HEADS:].astype(BF16)
        return (norm_mix_g[l].reshape(1, D), wqkv, wf, bf, wu, sel)

    qt, kaug, vt, diff = _premix(x, mod_all[0], premix_consts(0), tm=tm_mix)
    for l in range(L):
        mod = mod_all[l]
        attn = _attention(qt, kaug, vt, tq=tq)

        wr = jnp.concatenate([
            jnp.pad(w_router_group[l].T, ((0, SUBLANES - N_EXPERT_GROUPS), (0, 0))),
            w_router_expert[l].transpose(0, 2, 1).reshape(N_EXPERTS, D)], axis=0)
        wr_hi = wr.astype(BF16)
        wr_lo = (wr - wr_hi.astype(F32)).astype(BF16)
        br = jnp.concatenate([jnp.pad(b_router_group[l], (0, SUBLANES - N_EXPERT_GROUPS)),
                              b_router_expert[l].reshape(N_EXPERTS)]).reshape(ROUTER_ROWS, 1).astype(F32)
        x1, xs, ids, wts, pos = _postmix(attn, diff, x, mod, w_pool[l].astype(BF16),
                                         pool_scale[l].reshape(1, POOL_WIDTH), w_out[l].astype(BF16),
                                         norm_ffn_g[l].reshape(1, D), wr_hi, wr_lo, br, tm=tm_mix)

        n_chunks, chunk_dst, chunk_src, tile_expert, tile_valid = _dispatch_tables(ids, tm=tm_mix, tm_e=tm_e, nt_e=nt_e)
        o_sorted = _experts(tile_expert, tile_valid, chunk_src, xs, wg_all, wu_all, wd_all, tm=tm_e, layer=l)
        if l == L - 1:
            (x,) = _combine(n_chunks, chunk_dst, o_sorted, x1, pos, wts, mod, tm=tm_mix,
                            g_final=norm_final_g.reshape(1, D))
        else:
            x, qt, kaug, vt, diff = _combine(n_chunks, chunk_dst, o_sorted, x1, pos, wts, mod, tm=tm_mix,
                                             next_premix=(mod_all[l + 1], premix_consts(l + 1)))
    return x
```
